```python
import jax, jax.numpy as jnp
from jax import lax
import numpy as np

D_MODEL = 2048
BATCH = 4
SEQ = 4096
DEPTH = 2

GRID_W = 64
CTX_LEN = 256
EPS = 1e-6

MLA_HEADS = 8
MLA_Q_RANK = 512
MLA_KV_RANK = 512
MLA_NOPE = 128
MLA_ROPE = 64
MLA_V = 128
MLA_QK = MLA_NOPE + MLA_ROPE
MLA_SCALE = MLA_QK ** -0.5
ROPE_THETA = 10000.0
ATTN_BLOCK = 128

SGU_WIDTH = 512
SGU_GROUPS = 4
SGU_GROUP_DIM = SGU_WIDTH // SGU_GROUPS
SGU_CHUNK = 128

GLA_HEADS = 4
GLA_DK = 64
GLA_DV = 128
GLA_GATE_RANK = 16
GLA_GATE_NORM = 16.0
GLA_CHUNK = 64

MIX_WIDTH = MLA_HEADS * MLA_V + SGU_WIDTH + GLA_HEADS * GLA_DV

IN_SPLITS = (MLA_Q_RANK, MLA_KV_RANK, MLA_ROPE,
             SGU_WIDTH, SGU_WIDTH,
             GLA_HEADS * GLA_DK, GLA_HEADS * GLA_DK,
             GLA_HEADS * GLA_DV, GLA_HEADS * GLA_DV,
             GLA_GATE_RANK, GLA_GATE_RANK)
D_IN = sum(IN_SPLITS)

MOE_GROUPS = 4
MOE_PER_GROUP = 4
MOE_EXPERTS = MOE_GROUPS * MOE_PER_GROUP
MOE_TOPK = 2
D_EXPERT = 1024

kernel_name = 'hybrid_mla_sgu_gla_hmoe_dit'


def rmsnorm(x, gain):
    xf = x.astype(jnp.float32)
    y = xf * lax.rsqrt(jnp.mean(xf * xf, axis=-1, keepdims=True) + EPS)
    return (y * gain.astype(jnp.float32)).astype(x.dtype)


def modulate(h, shift, scale):
    return h * (1 + scale) + shift


def split_in(z):
    return jnp.split(z, np.cumsum(IN_SPLITS)[:-1].tolist(), axis=-1)


def axial_rope_tables(n_tokens):
    rows = n_tokens // GRID_W
    row = jnp.repeat(jnp.arange(rows, dtype=jnp.int32), GRID_W)
    col = jnp.tile(jnp.arange(GRID_W, dtype=jnp.int32), rows)
    n_freq = MLA_ROPE // 4
    inv_freq = ROPE_THETA ** (-jnp.arange(n_freq, dtype=jnp.float32) / n_freq)
    ang = jnp.concatenate([row[:, None].astype(jnp.float32) * inv_freq,
                           col[:, None].astype(jnp.float32) * inv_freq], axis=-1)
    return jnp.cos(ang), jnp.sin(ang)


def apply_axial_rope(t, cos, sin):
    n_freq = MLA_ROPE // 4

    def rot(u, c, s):
        u1, u2 = jnp.split(u, 2, axis=-1)
        return jnp.concatenate([u1 * c - u2 * s, u1 * s + u2 * c], axis=-1)

    t_row, t_col = jnp.split(t, 2, axis=-1)
    return jnp.concatenate([rot(t_row, cos[..., :n_freq], sin[..., :n_freq]),
                            rot(t_col, cos[..., n_freq:], sin[..., n_freq:])], axis=-1)


def mla_qkv(c_q, c_kv, k_pe_shared, q_norm, w_uq, kv_norm, w_ukv, q_gain, k_gain, rope):
    B, N, _ = c_q.shape
    q = (rmsnorm(c_q, q_norm) @ w_uq).reshape(B, N, MLA_HEADS, MLA_QK)
    kv = (rmsnorm(c_kv, kv_norm) @ w_ukv).reshape(B, N, MLA_HEADS, MLA_NOPE + MLA_V)
    k_nope, v = jnp.split(kv, [MLA_NOPE], axis=-1)
    k_pe = jnp.broadcast_to(k_pe_shared[:, :, None, :], (B, N, MLA_HEADS, MLA_ROPE))
    k = jnp.concatenate([k_nope, k_pe], axis=-1)
    q = rmsnorm(q, q_gain)
    k = rmsnorm(k, k_gain)
    if rope is not None:
        cos, sin = rope
        q = jnp.concatenate([q[..., :MLA_NOPE], apply_axial_rope(q[..., MLA_NOPE:], cos, sin)], axis=-1)
        k = jnp.concatenate([k[..., :MLA_NOPE], apply_axial_rope(k[..., MLA_NOPE:], cos, sin)], axis=-1)
    return q, k, v


def latent_attention(q, k, v, k_ctx, v_ctx):
    B, S, H, _ = q.shape
    keys = jnp.concatenate([k_ctx, k], axis=1)
    vals = jnp.concatenate([v_ctx, v], axis=1)
    n_blk = S // ATTN_BLOCK
    q_blocks = q.reshape(B, n_blk, ATTN_BLOCK, H, MLA_QK).transpose(1, 0, 2, 3, 4)

    def one_block(qb):
        s = jnp.einsum('bqhd,bkhd->bhqk', qb, keys).astype(jnp.float32) * MLA_SCALE
        p = jax.nn.softmax(s, axis=-1).astype(vals.dtype)
        return jnp.einsum('bhqk,bkhd->bqhd', p, vals)

    o = lax.map(one_block, q_blocks)
    return o.transpose(1, 0, 2, 3, 4).reshape(B, S, H * MLA_V)


def context_attention(q, k, v):
    B, L, H, _ = q.shape
    s = jnp.einsum('bqhd,bkhd->bhqk', q, k).astype(jnp.float32) * MLA_SCALE
    p = jax.nn.softmax(s, axis=-1).astype(v.dtype)
    return jnp.einsum('bhqk,bkhd->bqhd', p, v).reshape(B, L, H * MLA_V)


def spatial_gating(zu, zv, v_gain, w_s, b_s):
    B, N, _ = zu.shape
    shape = (B, N // SGU_CHUNK, SGU_CHUNK, SGU_GROUPS, SGU_GROUP_DIM)
    u = jax.nn.gelu(zu).reshape(shape)
    v = rmsnorm(jax.nn.gelu(zv).reshape(shape), v_gain)
    v = jnp.einsum('gts,bcsgd->bctgd', w_s, v) + b_s.T[:, :, None]
    return (u * v).reshape(B, N, SGU_WIDTH)


def gla_inputs(zq, zk, zv, zg_f, zg_b, wg_f, bg_f, wg_b, bg_b):
    B, N, _ = zq.shape
    q = zq.reshape(B, N, GLA_HEADS, GLA_DK) * GLA_DK ** -0.5
    k = zk.reshape(B, N, GLA_HEADS, GLA_DK)
    v = zv.reshape(B, N, GLA_HEADS, GLA_DV)
    g_f = (jax.nn.log_sigmoid((zg_f @ wg_f + bg_f).astype(jnp.float32)) / GLA_GATE_NORM).reshape(B, N, GLA_HEADS, GLA_DK)
    g_b = (jax.nn.log_sigmoid((zg_b @ wg_b + bg_b).astype(jnp.float32)) / GLA_GATE_NORM).reshape(B, N, GLA_HEADS, GLA_DK)
    return q, k, v, g_f, g_b


def gla_scan(q, k, v, g, s0):
    B, N, H, _ = q.shape
    n_chunk = N // GLA_CHUNK

    def to_chunks(t):
        return t.reshape(B, n_chunk, GLA_CHUNK, H, t.shape[-1]).transpose(1, 0, 3, 2, 4)

    incl = jnp.tril(jnp.ones((GLA_CHUNK, GLA_CHUNK), dtype=bool))[:, :, None]

    def step(S, inp):
        qi, ki, vi, gi = inp
        qf, kf, vf = qi.astype(jnp.float32), ki.astype(jnp.float32), vi.astype(jnp.float32)
        b = jnp.cumsum(gi, axis=2)
        o_inter = jnp.einsum('bhtd,bhde->bhte', qf * jnp.exp(b), S)
        diff = b[:, :, :, None, :] - b[:, :, None, :, :]
        decay = jnp.exp(jnp.where(incl, diff, -jnp.inf))
        a = jnp.einsum('bhtd,bhsd,bhtsd->bhts', qf, kf, decay)
        o = o_inter + jnp.einsum('bhts,bhse->bhte', a, vf)
        b_last = b[:, :, -1:, :]
        S = jnp.exp(b_last[:, :, 0, :])[..., None] * S + jnp.einsum('bhsd,bhse->bhde', kf * jnp.exp(b_last - b), vf)
        return S, o

    S, o = lax.scan(step, s0, (to_chunks(q), to_chunks(k), to_chunks(v), to_chunks(g)))
    o = o.transpose(1, 0, 3, 2, 4).reshape(B, N, H, GLA_DV).astype(v.dtype)
    return o, S


def gla_bidirectional(q, k, v, g_f, g_b, s0_f, s0_b):
    flip = lambda t: jnp.flip(t, axis=1)
    o_f, s_f = gla_scan(q, k, v, g_f, s0_f)
    o_b, s_b = gla_scan(flip(q), flip(k), flip(v), flip(g_b), s0_b)
    return o_f + flip(o_b), s_f, s_b


def gla_final_state(k, v, g):
    G = jnp.cumsum(g, axis=1)
    w = jnp.exp(G[:, -1:] - G)
    return jnp.einsum('bnhd,bnhe->bhde', k.astype(jnp.float32) * w, v.astype(jnp.float32))


def gla_output(o, r, gain):
    B, N = o.shape[:2]
    return rmsnorm(o, gain).reshape(B, N, GLA_HEADS * GLA_DV) * jax.nn.silu(r)


def hier_moe(h, w_group, b_group, w_expert, b_expert, w1, w3, w2):
    g_prob = jax.nn.softmax((h @ w_group + b_group).astype(jnp.float32), axis=-1)
    g_w, g_sel = lax.top_k(g_prob, 1)
    e_logits = (h @ w_expert + b_expert).astype(jnp.float32)
    e_logits = e_logits.reshape(e_logits.shape[:-1] + (MOE_GROUPS, MOE_PER_GROUP))
    e_in = jnp.take_along_axis(e_logits, g_sel[..., None], axis=-2)[..., 0, :]
    e_w, e_sel = lax.top_k(jax.nn.softmax(e_in, axis=-1), MOE_TOPK)
    e_w = e_w / jnp.sum(e_w, axis=-1, keepdims=True)
    expert_id = g_sel * MOE_PER_GROUP + e_sel
    combine = jnp.sum(jax.nn.one_hot(expert_id, MOE_EXPERTS, dtype=jnp.float32)
                      * (g_w * e_w)[..., None], axis=-2).astype(h.dtype)
    out = jnp.zeros_like(h)
    for e in range(MOE_EXPERTS):
        act = jax.nn.silu(h @ w1[e]) * (h @ w3[e])
        out = out + combine[..., e:e + 1] * (act @ w2[e])
    return out


def setup_inputs(seed: int = 0) -> dict:
    key = jax.random.key(seed)
    keys = jax.random.split(key, 40)
    counter = [0]

    def nrm(shape, scale=1.0):
        k = keys[counter[0]]
        counter[0] += 1
        return jax.random.normal(k, shape, jnp.float32) * scale

    def gain(shape):
        return 1.0 + nrm(shape, 0.02)

    D = D_MODEL
    return {
        'x': nrm((BATCH, SEQ, D)),
        'c': nrm((BATCH, D)),
        'ctx': nrm((BATCH, CTX_LEN, D)),
        'c_ctx': nrm((D,)),
        'norm1_g': gain((DEPTH, D)),
        'norm2_g': gain((DEPTH, D)),
        'ada_w': nrm((DEPTH, D, 6 * D), 0.5 * D ** -0.5),
        'ada_b': nrm((DEPTH, 6 * D), 0.02),
        'w_in': nrm((DEPTH, D, D_IN), D ** -0.5),
        'mla_q_norm': gain((DEPTH, MLA_Q_RANK)),
        'mla_w_uq': nrm((DEPTH, MLA_Q_RANK, MLA_HEADS * MLA_QK), MLA_Q_RANK ** -0.5),
        'mla_kv_norm': gain((DEPTH, MLA_KV_RANK)),
        'mla_w_ukv': nrm((DEPTH, MLA_KV_RANK, MLA_HEADS * (MLA_NOPE + MLA_V)), MLA_KV_RANK ** -0.5),
        'mla_q_gain': gain((DEPTH, MLA_QK)),
        'mla_k_gain': gain((DEPTH, MLA_QK)),
        'sgu_norm': gain((DEPTH, SGU_GROUPS, SGU_GROUP_DIM)),
        'sgu_w': nrm((DEPTH, SGU_GROUPS, SGU_CHUNK, SGU_CHUNK), SGU_CHUNK ** -0.5),
        'sgu_b': gain((DEPTH, SGU_GROUPS, SGU_CHUNK)),
        'gla_wg_f': nrm((DEPTH, GLA_GATE_RANK, GLA_HEADS * GLA_DK), GLA_GATE_RANK ** -0.5),
        'gla_bg_f': nrm((DEPTH, GLA_HEADS * GLA_DK), 0.1),
        'gla_wg_b': nrm((DEPTH, GLA_GATE_RANK, GLA_HEADS * GLA_DK), GLA_GATE_RANK ** -0.5),
        'gla_bg_b': nrm((DEPTH, GLA_HEADS * GLA_DK), 0.1),
        'gla_out_norm': gain((DEPTH, GLA_DV)),
        'w_out': nrm((DEPTH, MIX_WIDTH, D), MIX_WIDTH ** -0.5),
        'moe_w_group': nrm((DEPTH, D, MOE_GROUPS), D ** -0.5),
        'moe_b_group': nrm((DEPTH, MOE_GROUPS), 0.01),
        'moe_w_expert': nrm((DEPTH, D, MOE_EXPERTS), D ** -0.5),
        'moe_b_expert': nrm((DEPTH, MOE_EXPERTS), 0.01),
        'moe_w1': nrm((DEPTH, MOE_EXPERTS, D, D_EXPERT), D ** -0.5),
        'moe_w3': nrm((DEPTH, MOE_EXPERTS, D, D_EXPERT), D ** -0.5),
        'moe_w2': nrm((DEPTH, MOE_EXPERTS, D_EXPERT, D), D_EXPERT ** -0.5),
    }


def reference(x, c, ctx, c_ctx, norm1_g, norm2_g, ada_w, ada_b, w_in,
              mla_q_norm, mla_w_uq, mla_kv_norm, mla_w_ukv, mla_q_gain, mla_k_gain,
              sgu_norm, sgu_w, sgu_b,
              gla_wg_f, gla_bg_f, gla_wg_b, gla_bg_b, gla_out_norm,
              w_out, moe_w_group, moe_b_group, moe_w_expert, moe_b_expert, moe_w1, moe_w3, moe_w2):
    B, S, _ = x.shape
    cos, sin = axial_rope_tables(S)
    rope = (cos[:, None, :].astype(x.dtype), sin[:, None, :].astype(x.dtype))
    zero_state = jnp.zeros((B, GLA_HEADS, GLA_DK, GLA_DV), jnp.float32)

    for l in range(DEPTH):
        ctx_out = l < DEPTH - 1
        mod_x = jnp.split((jax.nn.silu(c) @ ada_w[l] + ada_b[l])[:, None, :], 6, axis=-1)
        mod_c = jnp.split((jax.nn.silu(c_ctx) @ ada_w[l] + ada_b[l])[None, None, :], 6, axis=-1)

        h = modulate(rmsnorm(x, norm1_g[l]), mod_x[0], mod_x[1])
        hc = modulate(rmsnorm(ctx, norm1_g[l]), mod_c[0], mod_c[1])
        (cq_x, ckv_x, kpe_x, zu_x, zv_x, gq_x, gk_x, gv_x, gr_x, ggf_x, ggb_x) = split_in(h @ w_in[l])
        (cq_c, ckv_c, kpe_c, zu_c, zv_c, gq_c, gk_c, gv_c, gr_c, ggf_c, ggb_c) = split_in(hc @ w_in[l])

        q_x, k_x, v_x = mla_qkv(cq_x, ckv_x, kpe_x, mla_q_norm[l], mla_w_uq[l], mla_kv_norm[l],
                                mla_w_ukv[l], mla_q_gain[l], mla_k_gain[l], rope)
        q_c, k_c, v_c = mla_qkv(cq_c, ckv_c, kpe_c, mla_q_norm[l], mla_w_uq[l], mla_kv_norm[l],
                                mla_w_ukv[l], mla_q_gain[l], mla_k_gain[l], None)
        a_x = latent_attention(q_x, k_x, v_x, k_c, v_c)

        b_x = spatial_gating(zu_x, zv_x, sgu_norm[l], sgu_w[l], sgu_b[l])

        q_gc, k_gc, v_gc, gf_c, gb_c = gla_inputs(gq_c, gk_c, gv_c, ggf_c, ggb_c,
                                                  gla_wg_f[l], gla_bg_f[l], gla_wg_b[l], gla_bg_b[l])
        if ctx_out:
            o_gc, s_f, s_b = gla_bidirectional(q_gc, k_gc, v_gc, gf_c, gb_c, zero_state, zero_state)
        else:
            s_f = gla_final_state(k_gc, v_gc, gf_c)
            s_b = gla_final_state(jnp.flip(k_gc, axis=1), jnp.flip(v_gc, axis=1), jnp.flip(gb_c, axis=1))
        q_gx, k_gx, v_gx, gf_x, gb_x = gla_inputs(gq_x, gk_x, gv_x, ggf_x, ggb_x,
                                                  gla_wg_f[l], gla_bg_f[l], gla_wg_b[l], gla_bg_b[l])
        o_gx, _, _ = gla_bidirectional(q_gx, k_gx, v_gx, gf_x, gb_x, s_f, s_b)
        c_x = gla_output(o_gx, gr_x, gla_out_norm[l])

        mix_x = jnp.concatenate([a_x, b_x, c_x], axis=-1)
        x = x + mod_x[2] * (mix_x @ w_out[l])

        h2 = modulate(rmsnorm(x, norm2_g[l]), mod_x[3], mod_x[4])
        x = x + mod_x[5] * hier_moe(h2, moe_w_group[l], moe_b_group[l], moe_w_expert[l], moe_b_expert[l],
                                    moe_w1[l], moe_w3[l], moe_w2[l])

        if ctx_out:
            a_c = context_attention(q_c, k_c, v_c)
            b_c = spatial_gating(zu_c, zv_c, sgu_norm[l], sgu_w[l], sgu_b[l])
            c_c = gla_output(o_gc, gr_c, gla_out_norm[l])
            mix_c = jnp.concatenate([a_c, b_c, c_c], axis=-1)
            ctx = ctx + mod_c[2] * (mix_c @ w_out[l])
            h2c = modulate(rmsnorm(ctx, norm2_g[l]), mod_c[3], mod_c[4])
            ctx = ctx + mod_c[5] * hier_moe(h2c, moe_w_group[l], moe_b_group[l], moe_w_expert[l], moe_b_expert[l],
                                            moe_w1[l], moe_w3[l], moe_w2[l])
    return x
```

```python
import functools

import numpy as np
import jax
import jax.numpy as jnp
from jax import lax
from jax.experimental import pallas as pl
from jax.experimental.pallas import tpu as pltpu

F32 = jnp.float32
BF16 = jnp.bfloat16

D_MODEL = 2048
GRID_W = 64
EPS = 1e-6

MLA_HEADS = 8
MLA_Q_RANK = 512
MLA_KV_RANK = 512
MLA_NOPE = 128
MLA_ROPE = 64
MLA_V = 128
MLA_QK = MLA_NOPE + MLA_ROPE
MLA_SCALE = MLA_QK ** -0.5
ROPE_THETA = 10000.0

SGU_WIDTH = 512
SGU_GROUPS = 4
SGU_GROUP_DIM = SGU_WIDTH // SGU_GROUPS
SGU_CHUNK = 128

GLA_HEADS = 4
GLA_DK = 64
GLA_DV = 128
GLA_GATE_RANK = 16
GLA_GATE_NORM = 16.0
GLA_CHUNK = 64
GLA_QK_W = GLA_HEADS * GLA_DK
GLA_V_W = GLA_HEADS * GLA_DV

MOE_GROUPS = 4
MOE_PER_GROUP = 4
MOE_EXPERTS = MOE_GROUPS * MOE_PER_GROUP
D_EXPERT = 1024

_IN_SPLITS = (512, 512, 64, 512, 512, 256, 256, 512, 512, 16, 16)
_IN_OFF = np.concatenate([[0], np.cumsum(_IN_SPLITS)]).tolist()

Z_CQ, Z_CKV, Z_ZU, Z_ZV = 0, 512, 1024, 1536
Z_GQ, Z_GK, Z_GV, Z_GR = 2048, 2304, 2560, 3072
Z_KPE, Z_GG = 3584, 3712
Z_WIDTH = 3840

LANES = 128
BF16_ROWS = 16
VMEM_LIMIT = 56 * 1024 * 1024

TM_IN = 256
TQ_ATTN = 256
T_LOC = 512
SLOTS = 2 * T_LOC + MOE_EXPERTS * BF16_ROWS
TM_FFN = 512
F_CHUNK = 256
_PIECE_SIZES = (512, 256, 128, 64, 32, 16)


def _cparams(sem, vmem=VMEM_LIMIT):
    return pltpu.CompilerParams(dimension_semantics=sem, vmem_limit_bytes=vmem)


def _silu(x):
    return x / (1.0 + jnp.exp(-x))


def _gelu_tanh(x):
    return 0.5 * x * (1.0 + jnp.tanh(np.sqrt(2.0 / np.pi).astype(np.float32) * (x + 0.044715 * (x * x * x))))


def _log_sigmoid(x):
    return jnp.minimum(x, 0.0) - jnp.log(1.0 + jnp.exp(-jnp.abs(x)))


def _dot(a, b):
    return jnp.dot(a, b, preferred_element_type=F32)


def _dot_nt(a, b):
    return lax.dot_general(a, b, (((1,), (1,)), ((), ())), preferred_element_type=F32)


def _dot_tn(a, b):
    return lax.dot_general(a, b, (((0,), (0,)), ((), ())), preferred_element_type=F32)


def _ada_kernel(c_ref, w_ref, b_ref, o_ref):
    a = _silu(c_ref[...]).astype(BF16)
    o_ref[0] = _dot(a, w_ref[0].astype(BF16)) + b_ref[0]


def _ada(c_rows, ada_w, ada_b):
    depth, d, n = ada_w.shape
    tn = 1536
    return pl.pallas_call(
        _ada_kernel,
        grid=(depth, n // tn),
        in_specs=[
            pl.BlockSpec((8, d), lambda l, j: (0, 0)),
            pl.BlockSpec((1, d, tn), lambda l, j: (l, 0, j)),
            pl.BlockSpec((1, 1, tn), lambda l, j: (l, 0, j)),
        ],
        out_specs=pl.BlockSpec((1, 8, tn), lambda l, j: (l, 0, j)),
        out_shape=jax.ShapeDtypeStruct((depth, 8, n), F32),
        compiler_params=_cparams(("arbitrary", "arbitrary")),
        name="ada",
    )(c_rows, ada_w, ada_b.reshape(depth, 1, n))


def _inproj_kernel(x_ref, mod_ref, n1_ref, win_ref, qn_ref, wuq_ref, kvn_ref, wukv_ref,
                   gqn_ref, gkn_ref, cq_ref, sq_ref, tk_ref,
                   sgn_ref, sgw_ref, sgb_ref, wg_ref, bg_ref,
                   q_out, k_out, v_out, b_out, gla_out, r_out):
    d = D_MODEL
    tm = x_ref.shape[0]
    x = x_ref[...]
    shift = mod_ref[0, :, 0:d]
    scale = mod_ref[0, :, d:2 * d]
    ms = jnp.mean(x * x, axis=-1, keepdims=True)
    h = (x * lax.rsqrt(ms + EPS) * n1_ref[...]) * (1.0 + scale) + shift
    z = _dot(h.astype(BF16), win_ref[...])

    cq = z[:, Z_CQ:Z_CQ + MLA_Q_RANK]
    ckv = z[:, Z_CKV:Z_CKV + MLA_KV_RANK]
    cqn = cq * lax.rsqrt(jnp.mean(cq * cq, axis=-1, keepdims=True) + EPS) * qn_ref[...]
    ckvn = ckv * lax.rsqrt(jnp.mean(ckv * ckv, axis=-1, keepdims=True) + EPS) * kvn_ref[...]
    qa = _dot(cqn.astype(BF16), wuq_ref[...])
    kv = _dot(ckvn.astype(BF16), wukv_ref[...])
    npe = MLA_HEADS * MLA_NOPE
    rw = MLA_HEADS * MLA_ROPE
    q_pe = qa[:, npe:npe + rw]
    q_rot = q_pe * cq_ref[...] + qa[:, npe + rw:npe + 2 * rw] * sq_ref[...]
    zk = z[:, Z_KPE:Z_KPE + 2 * MLA_ROPE]
    tkz = zk * tk_ref[...]
    k_rot = tkz[:, 0:MLA_ROPE] + tkz[:, MLA_ROPE:2 * MLA_ROPE]
    k_pe = zk[:, 0:MLA_ROPE]
    ss_kpe = jnp.sum(k_pe * k_pe, axis=-1, keepdims=True)
    gqn = gqn_ref[...]
    gkn = gkn_ref[...]
    for hh in range(MLA_HEADS):
        qn_h = qa[:, hh * MLA_NOPE:(hh + 1) * MLA_NOPE]
        qp_h = q_pe[:, hh * MLA_ROPE:(hh + 1) * MLA_ROPE]
        ssq = jnp.sum(qn_h * qn_h, axis=-1, keepdims=True) + jnp.sum(qp_h * qp_h, axis=-1, keepdims=True)
        rq = lax.rsqrt(ssq * (1.0 / MLA_QK) + EPS) * MLA_SCALE
        q_h = jnp.concatenate([qn_h * gqn * rq, q_rot[:, hh * MLA_ROPE:(hh + 1) * MLA_ROPE] * rq], axis=-1)
        q_out[0, hh] = q_h.astype(BF16)
        kn_h = kv[:, hh * MLA_NOPE:(hh + 1) * MLA_NOPE]
        ssk = jnp.sum(kn_h * kn_h, axis=-1, keepdims=True) + ss_kpe
        rk = lax.rsqrt(ssk * (1.0 / MLA_QK) + EPS)
        k_h = jnp.concatenate([kn_h * gkn * rk, k_rot * rk], axis=-1)
        k_out[0, hh] = k_h.astype(BF16)
        v_out[0, hh] = kv[:, npe + hh * MLA_V:npe + (hh + 1) * MLA_V].astype(BF16)

    u = _gelu_tanh(z[:, Z_ZU:Z_ZU + SGU_WIDTH])
    vv = _gelu_tanh(z[:, Z_ZV:Z_ZV + SGU_WIDTH])
    sgn = sgn_ref[...]
    for c in range(tm // SGU_CHUNK):
        rows = slice(c * SGU_CHUNK, (c + 1) * SGU_CHUNK)
        outs = []
        for g in range(SGU_GROUPS):
            cols = slice(g * SGU_GROUP_DIM, (g + 1) * SGU_GROUP_DIM)
            seg = vv[rows, cols]
            vn = seg * lax.rsqrt(jnp.mean(seg * seg, axis=-1, keepdims=True) + EPS) * sgn[:, cols]
            outs.append(_dot(sgw_ref[g], vn.astype(BF16)))
        sp = jnp.concatenate(outs, axis=-1) + sgb_ref[...]
        b_out[rows, :] = (u[rows, :] * sp).astype(BF16)

    gl = _dot(z[:, Z_GG:Z_GG + LANES].astype(BF16), wg_ref[...]) + bg_ref[...]
    gdec = _log_sigmoid(gl) * (1.0 / GLA_GATE_NORM)
    gla_out[:, 0:2 * GLA_QK_W] = z[:, Z_GQ:Z_GQ + 2 * GLA_QK_W]
    gla_out[:, 2 * GLA_QK_W:4 * GLA_QK_W] = gdec
    gla_out[:, 4 * GLA_QK_W:4 * GLA_QK_W + GLA_V_W] = z[:, Z_GV:Z_GV + GLA_V_W]
    r_out[...] = z[:, Z_GR:Z_GR + GLA_V_W]


def _inproj(x2, n_batch, mod_rows, mod_row0, lw, tabs):
    n_total = x2.shape[0]
    n_tok = n_total // n_batch
    tm = TM_IN
    tpb = n_tok // tm
    tab_blocks = tabs["cq"].shape[0] // tm
    if mod_row0 == 0:
        mod_map = lambda i: (i // tpb, 0, 0)
    else:
        mod_map = lambda i: (mod_row0, 0, 0)
    const2 = lambda i: (0, 0)
    tab_map = lambda i: (i % tab_blocks, 0)
    tok_map = lambda i: (i, 0)
    head_map = lambda i: (i // tpb, 0, i % tpb, 0)
    in_specs = [
        pl.BlockSpec((tm, D_MODEL), tok_map),
        pl.BlockSpec((1, 1, 6 * D_MODEL), mod_map),
        pl.BlockSpec((1, D_MODEL), const2),
        pl.BlockSpec((D_MODEL, Z_WIDTH), const2, pipeline_mode=pl.Buffered(1)),
        pl.BlockSpec((1, MLA_Q_RANK), const2),
        pl.BlockSpec((MLA_Q_RANK, 2048), const2, pipeline_mode=pl.Buffered(1)),
        pl.BlockSpec((1, MLA_KV_RANK), const2),
        pl.BlockSpec((MLA_KV_RANK, 2048), const2, pipeline_mode=pl.Buffered(1)),
        pl.BlockSpec((1, MLA_NOPE), const2),
        pl.BlockSpec((1, MLA_NOPE), const2),
        pl.BlockSpec((tm, MLA_HEADS * MLA_ROPE), tab_map),
        pl.BlockSpec((tm, MLA_HEADS * MLA_ROPE), tab_map),
        pl.BlockSpec((tm, 2 * MLA_ROPE), tab_map),
        pl.BlockSpec((1, SGU_WIDTH), const2),
        pl.BlockSpec((SGU_GROUPS, SGU_CHUNK, SGU_CHUNK), lambda i: (0, 0, 0)),
        pl.BlockSpec((SGU_CHUNK, SGU_WIDTH), const2),
        pl.BlockSpec((LANES, 2 * GLA_QK_W), const2),
        pl.BlockSpec((1, 2 * GLA_QK_W), const2),
    ]
    out_shape = [
        jax.ShapeDtypeStruct((n_batch, MLA_HEADS, n_tok, MLA_QK), BF16),
        jax.ShapeDtypeStruct((n_batch, MLA_HEADS, n_tok, MLA_QK), BF16),
        jax.ShapeDtypeStruct((n_batch, MLA_HEADS, n_tok, MLA_V), BF16),
        jax.ShapeDtypeStruct((n_total, SGU_WIDTH), BF16),
        jax.ShapeDtypeStruct((n_total, 4 * GLA_QK_W + GLA_V_W), F32),
        jax.ShapeDtypeStruct((n_total, GLA_V_W), F32),
    ]
    out_specs = [
        pl.BlockSpec((1, MLA_HEADS, tm, MLA_QK), head_map),
        pl.BlockSpec((1, MLA_HEADS, tm, MLA_QK), head_map),
        pl.BlockSpec((1, MLA_HEADS, tm, MLA_V), head_map),
        pl.BlockSpec((tm, SGU_WIDTH), tok_map),
        pl.BlockSpec((tm, 4 * GLA_QK_W + GLA_V_W), tok_map),
        pl.BlockSpec((tm, GLA_V_W), tok_map),
    ]
    return pl.pallas_call(
        _inproj_kernel,
        grid=(n_total // tm,),
        in_specs=in_specs,
        out_specs=out_specs,
        out_shape=out_shape,
        compiler_params=_cparams(("arbitrary",)),
        name="inproj",
    )(x2, mod_rows, lw["n1"], lw["w_in"], lw["q_norm"], lw["w_uq"], lw["kv_norm"], lw["w_ukv"],
      lw["gq_nope"], lw["gk_nope"], tabs["cq"], tabs["sq"], tabs["tk"],
      lw["sgu_norm"], lw["sgu_w"], lw["sgu_b"], lw["wg"], lw["bg"])


_GLA_LEVELS = (32, 16, 8, 4, 2, 1)
_N_LEV = len(_GLA_LEVELS)
_R_QI = 2 * _N_LEV * GLA_CHUNK
_R_KS = _R_QI + GLA_CHUNK
_R_TOT = _R_KS + GLA_CHUNK
_M_ROWS = _R_TOT + 8


def _gla_constants():
    c = GLA_CHUNK
    t = np.arange(c)[:, None]
    j = np.arange(c)[None, :]
    mq, mk, masks = [], [], []
    for m in _GLA_LEVELS:
        upper = (t // m) % 2 == 1
        bound = (t // (2 * m)) * 2 * m + m - 1
        mq.append((upper & (j > bound) & (j <= t)).astype(np.float32))
        mk.append((~upper & (j > t) & (j <= bound)).astype(np.float32))
        s = j
        masks.append(((t // (2 * m) == s // (2 * m)) & upper & ((s // m) % 2 == 0)).astype(np.float32))
    masks.append((t == j).astype(np.float32))
    mqi = (j <= t).astype(np.float32)
    mks = (j > t).astype(np.float32)
    tot = np.ones((8, c), np.float32)
    m_f = np.concatenate(mq + mk + [mqi, mks, tot], axis=0)
    flip = lambda a: a[::-1, ::-1]
    m_b = np.concatenate([flip(a) for a in mq] + [flip(a) for a in mk] + [flip(mqi), flip(mks), tot], axis=0)
    mask_f = np.stack([np.tile(a, (GLA_HEADS, 1)) for a in masks])
    mask_b = np.stack([np.tile(flip(a), (GLA_HEADS, 1)) for a in masks])
    return np.stack([m_f, m_b]), np.stack([mask_f, mask_b])


def _stack_heads(a, lane_head):
    return jnp.concatenate([jnp.where(lane_head == hh, a, jnp.zeros_like(a)) for hh in range(GLA_HEADS)], axis=0)


def _gla_chunk(q, k, g, v, st, mall, masks, lane_head):
    c = GLA_CHUNK
    g_hi = g.astype(BF16)
    r1 = g - g_hi.astype(F32)
    g_mid = r1.astype(BF16)
    g_lo = (r1 - g_mid.astype(F32)).astype(BF16)
    e3 = _dot(mall, jnp.concatenate([g_hi, g_mid, g_lo], axis=1))
    w = GLA_QK_W
    e = e3[:, 0:w] + e3[:, w:2 * w] + e3[:, 2 * w:3 * w]
    a = masks[_N_LEV] * _dot_nt(_stack_heads(q.astype(BF16), lane_head), k.astype(BF16))
    for lev in range(_N_LEV):
        qt = (q * jnp.exp(e[lev * c:(lev + 1) * c])).astype(BF16)
        kt = (k * jnp.exp(e[(_N_LEV + lev) * c:(_N_LEV + lev + 1) * c])).astype(BF16)
        a = a + masks[lev] * _dot_nt(_stack_heads(qt, lane_head), kt)
    qi = (q * jnp.exp(e[_R_QI:_R_QI + c])).astype(BF16)
    o_inter = _dot_nt(_stack_heads(qi, lane_head), st.astype(BF16))
    vb = v.astype(BF16)
    a16 = a.astype(BF16)
    outs = []
    for hh in range(GLA_HEADS):
        rows = slice(hh * c, (hh + 1) * c)
        outs.append(o_inter[rows] + _dot(a16[rows], vb[:, hh * GLA_DV:(hh + 1) * GLA_DV]))
    o = jnp.concatenate(outs, axis=-1)
    kd = (k * jnp.exp(e[_R_KS:_R_KS + c])).astype(BF16)
    upd = _dot_tn(vb, kd)
    st_new = st * jnp.exp(e[_R_TOT:_R_TOT + 1])
    for hh in range(GLA_HEADS):
        st_new = st_new + jnp.where(lane_head == hh, upd[hh * GLA_DV:(hh + 1) * GLA_DV], 0.0)
    return o, st_new


def _gla_kernel(fwd_ref, bwd_ref, s0_ref, mall_ref, mask_ref, of_ref, ob_ref, sfin_ref, st_ref):
    step = pl.program_id(0)
    n_batch = fwd_ref.shape[0]

    @pl.when(step == 0)
    def _():
        st_ref[...] = s0_ref[...]

    lane_head = lax.broadcasted_iota(jnp.int32, (1, GLA_QK_W), 1) // GLA_DK
    w = GLA_QK_W
    for direction, (in_ref, o_ref) in enumerate(((fwd_ref, of_ref), (bwd_ref, ob_ref))):
        mall = mall_ref[direction]
        masks = [mask_ref[direction, i] for i in range(_N_LEV + 1)]

        def body(b, carry, in_ref=in_ref, o_ref=o_ref, direction=direction, mall=mall, masks=masks):
            blk = in_ref[b]
            q = blk[:, 0:w]
            k = blk[:, w:2 * w]
            g = blk[:, (2 + direction) * w:(3 + direction) * w]
            v = blk[:, 4 * w:4 * w + GLA_V_W]
            o, st_new = _gla_chunk(q, k, g, v, st_ref[direction, b], mall, masks, lane_head)
            o_ref[b] = o
            st_ref[direction, b] = st_new
            return carry

        lax.fori_loop(0, n_batch, body, 0)

    @pl.when(step == pl.num_programs(0) - 1)
    def _():
        sfin_ref[...] = st_ref[...]


def _gla(gla_in, s0, consts):
    n_batch, n_tok, width = gla_in.shape
    nc = n_tok // GLA_CHUNK
    mall, masks = consts
    st_shape = (2, n_batch, GLA_DV, GLA_QK_W)
    return pl.pallas_call(
        _gla_kernel,
        grid=(nc,),
        in_specs=[
            pl.BlockSpec((n_batch, GLA_CHUNK, width), lambda c: (0, c, 0)),
            pl.BlockSpec((n_batch, GLA_CHUNK, width), lambda c: (0, nc - 1 - c, 0)),
            pl.BlockSpec(st_shape, lambda c: (0, 0, 0, 0)),
            pl.BlockSpec(mall.shape, lambda c: (0, 0, 0)),
            pl.BlockSpec(masks.shape, lambda c: (0, 0, 0, 0)),
        ],
        out_specs=[
            pl.BlockSpec((n_batch, GLA_CHUNK, GLA_V_W), lambda c: (0, c, 0)),
            pl.BlockSpec((n_batch, GLA_CHUNK, GLA_V_W), lambda c: (0, nc - 1 - c, 0)),
            pl.BlockSpec(st_shape, lambda c: (0, 0, 0, 0)),
        ],
        out_shape=[
            jax.ShapeDtypeStruct((n_batch, n_tok, GLA_V_W), F32),
            jax.ShapeDtypeStruct((n_batch, n_tok, GLA_V_W), F32),
            jax.ShapeDtypeStruct(st_shape, F32),
        ],
        scratch_shapes=[pltpu.VMEM(st_shape, F32)],
        compiler_params=_cparams(("arbitrary",)),
        name="gla",
    )(gla_in, gla_in, s0, mall, masks)


def _attn_kernel(*refs, n_seg):
    q_ref = refs[0]
    k_refs = refs[1:1 + n_seg]
    v_refs = refs[1 + n_seg:1 + 2 * n_seg]
    o_ref = refs[1 + 2 * n_seg]
    q = q_ref[0, 0]
    scores = [_dot_nt(q, k_ref[0, 0]) for k_ref in k_refs]
    m = functools.reduce(jnp.maximum, [jnp.max(s, axis=-1, keepdims=True) for s in scores])
    acc = None
    den = None
    for s, v_ref in zip(scores, v_refs):
        p = jnp.exp(s - m)
        l = jnp.sum(p, axis=-1, keepdims=True)
        pv = _dot(p.astype(BF16), v_ref[0, 0])
        acc = pv if acc is None else acc + pv
        den = l if den is None else den + l
    o_ref[0] = (acc / den).astype(BF16)


def _attention(q, ks, vs):
    n_batch, n_heads, nq, _ = q.shape
    tq = TQ_ATTN
    n_seg = len(ks)
    in_specs = [pl.BlockSpec((1, 1, tq, MLA_QK), lambda b, h, i: (b, h, i, 0))]
    for k in ks:
        in_specs.append(pl.BlockSpec((1, 1, k.shape[2], MLA_QK), lambda b, h, i: (b, h, 0, 0)))
    for v in vs:
        in_specs.append(pl.BlockSpec((1, 1, v.shape[2], MLA_V), lambda b, h, i: (b, h, 0, 0)))
    return pl.pallas_call(
        functools.partial(_attn_kernel, n_seg=n_seg),
        grid=(n_batch, n_heads, nq // tq),
        in_specs=in_specs,
        out_specs=pl.BlockSpec((1, tq, MLA_V), lambda b, h, i: (b, i, h)),
        out_shape=jax.ShapeDtypeStruct((n_batch, nq, n_heads * MLA_V), BF16),
        compiler_params=_cparams(("arbitrary", "arbitrary", "arbitrary")),
        name="attn",
    )(q, *ks, *vs)


def _outproj_kernel(a_ref, b_ref, of_ref, ob_ref, r_ref, x_ref, mod_ref, gon_ref, n2_ref, wout_ref,
                    wr_hi_ref, wr_lo_ref, br_ref, lst_ref, sl_ref,
                    x1_out, h2_out, meta_out, cnt_out):
    d = D_MODEL
    tm = x_ref.shape[0]
    gate1 = mod_ref[0, :, 2 * d:3 * d]
    shift2 = mod_ref[0, :, 3 * d:4 * d]
    scale2 = mod_ref[0, :, 4 * d:5 * d]

    og = of_ref[...] + ob_ref[...]
    gon = gon_ref[...]
    parts = []
    for hh in range(GLA_HEADS):
        seg = og[:, hh * GLA_DV:(hh + 1) * GLA_DV]
        parts.append(seg * lax.rsqrt(jnp.mean(seg * seg, axis=-1, keepdims=True) + EPS) * gon)
    cg = jnp.concatenate(parts, axis=-1) * _silu(r_ref[...])
    na = MLA_HEADS * MLA_V
    y = _dot(a_ref[...], wout_ref[0:na, :])
    y = y + _dot(b_ref[...], wout_ref[na:na + SGU_WIDTH, :])
    y = y + _dot(cg.astype(BF16), wout_ref[na + SGU_WIDTH:na + SGU_WIDTH + GLA_V_W, :])
    x1 = x_ref[...] + gate1 * y
    x1_out[...] = x1

    ms = jnp.mean(x1 * x1, axis=-1, keepdims=True)
    h2 = (x1 * lax.rsqrt(ms + EPS) * n2_ref[...]) * (1.0 + scale2) + shift2
    h_hi = h2.astype(BF16)
    h2_out[...] = h_hi
    h_lo = (h2 - h_hi.astype(F32)).astype(BF16)

    wr_hi = wr_hi_ref[...]
    lt = _dot_nt(wr_hi, h_hi) + _dot_nt(wr_hi, h_lo) + _dot_nt(wr_lo_ref[...], h_hi) + br_ref[:, 0:1]
    grp = [lt[MOE_EXPERTS + g:MOE_EXPERTS + g + 1, :] for g in range(MOE_GROUPS)]
    gmax = functools.reduce(jnp.maximum, grp)
    gsel = jnp.full(gmax.shape, MOE_GROUPS - 1, jnp.int32)
    for g in range(MOE_GROUPS - 2, -1, -1):
        gsel = jnp.where(grp[g] == gmax, g, gsel)
    gsum = functools.reduce(jnp.add, [jnp.exp(gl - gmax) for gl in grp])
    g_w = 1.0 / gsum
    e_in = []
    for j in range(MOE_PER_GROUP):
        ej = lt[j:j + 1, :]
        for g in range(1, MOE_GROUPS):
            ej = jnp.where(gsel == g, lt[g * MOE_PER_GROUP + j:g * MOE_PER_GROUP + j + 1, :], ej)
        e_in.append(ej)
    m1 = functools.reduce(jnp.maximum, e_in)
    i1 = jnp.full(m1.shape, MOE_PER_GROUP - 1, jnp.int32)
    for j in range(MOE_PER_GROUP - 2, -1, -1):
        i1 = jnp.where(e_in[j] == m1, j, i1)
    e_rest = [jnp.where(i1 == j, -jnp.inf, e_in[j]) for j in range(MOE_PER_GROUP)]
    m2 = functools.reduce(jnp.maximum, e_rest)
    i2 = jnp.full(m2.shape, MOE_PER_GROUP - 1, jnp.int32)
    for j in range(MOE_PER_GROUP - 2, -1, -1):
        i2 = jnp.where(e_rest[j] == m2, j, i2)
    p2 = jnp.exp(m2 - m1)
    w1 = g_w / (1.0 + p2)
    w2 = g_w * p2 / (1.0 + p2)
    eid1 = gsel * MOE_PER_GROUP + i1
    eid2 = gsel * MOE_PER_GROUP + i2

    e_iota = lax.broadcasted_iota(jnp.int32, (MOE_EXPERTS, tm), 0)
    hit1 = e_iota == eid1
    hit2 = e_iota == eid2
    onehot = jnp.where(hit1, 1.0, 0.0) + jnp.where(hit2, 1.0, 0.0)
    rank = _dot(onehot.astype(BF16), lst_ref[...])
    counts = jnp.sum(onehot, axis=1, keepdims=True)
    padded = jnp.ceil(counts * (1.0 / BF16_ROWS)) * BF16_ROWS
    padded_l = jnp.broadcast_to(padded, (MOE_EXPERTS, LANES))
    off = _dot(sl_ref[...], padded_l.astype(BF16))
    pos_e = off[:, 0:1] + rank
    pos1 = jnp.sum(jnp.where(hit1, pos_e, 0.0), axis=0, keepdims=True)
    pos2 = jnp.sum(jnp.where(hit2, pos_e, 0.0), axis=0, keepdims=True)
    zero = jnp.zeros_like(pos1)
    meta_out[0] = jnp.concatenate(
        [pos1, pos2, w1, w2, eid1.astype(F32), eid2.astype(F32), zero, zero], axis=0)
    cnt_out[0] = padded_l


def _outproj(a, b, o_f, o_b, r, x2, n_batch, mod_rows, mod_row0, lw, rconst):
    n_total = x2.shape[0]
    n_tok = n_total // n_batch
    tm = T_LOC
    tpb = n_tok // tm
    if mod_row0 == 0:
        mod_map = lambda i: (i // tpb, 0, 0)
    else:
        mod_map = lambda i: (mod_row0, 0, 0)
    const2 = lambda i: (0, 0)
    tok_map = lambda i: (i, 0)
    n_tiles = n_total // tm
    in_specs = [
        pl.BlockSpec((tm, MLA_HEADS * MLA_V), tok_map),
        pl.BlockSpec((tm, SGU_WIDTH), tok_map),
        pl.BlockSpec((tm, GLA_V_W), tok_map),
        pl.BlockSpec((tm, GLA_V_W), tok_map),
        pl.BlockSpec((tm, GLA_V_W), tok_map),
        pl.BlockSpec((tm, D_MODEL), tok_map),
        pl.BlockSpec((1, 1, 6 * D_MODEL), mod_map),
        pl.BlockSpec((1, GLA_DV), const2),
        pl.BlockSpec((1, D_MODEL), const2),
        pl.BlockSpec((D_MODEL, D_MODEL), const2, pipeline_mode=pl.Buffered(1)),
        pl.BlockSpec((32, D_MODEL), const2),
        pl.BlockSpec((32, D_MODEL), const2),
        pl.BlockSpec((32, LANES), const2),
        pl.BlockSpec((tm, tm), const2),
        pl.BlockSpec((MOE_EXPERTS, MOE_EXPERTS), const2),
    ]
    out_shape = [
        jax.ShapeDtypeStruct((n_total, D_MODEL), F32),
        jax.ShapeDtypeStruct((n_total, D_MODEL), BF16),
        jax.ShapeDtypeStruct((n_tiles, 8, tm), F32),
        jax.ShapeDtypeStruct((n_tiles, MOE_EXPERTS, LANES), F32),
    ]
    out_specs = [
        pl.BlockSpec((tm, D_MODEL), tok_map),
        pl.BlockSpec((tm, D_MODEL), tok_map),
        pl.BlockSpec((1, 8, tm), lambda i: (i, 0, 0)),
        pl.BlockSpec((1, MOE_EXPERTS, LANES), lambda i: (i, 0, 0)),
    ]
    return pl.pallas_call(
        _outproj_kernel,
        grid=(n_tiles,),
        in_specs=in_specs,
        out_specs=out_specs,
        out_shape=out_shape,
        compiler_params=_cparams(("arbitrary",)),
        name="outproj",
    )(a, b, o_f, o_b, r, x2, mod_rows, lw["gla_out_norm"], lw["n2"], lw["w_out"],
      lw["wr_hi"], lw["wr_lo"], lw["br"], rconst["lst"], rconst["sl"])


def _perm_kernel(h2_ref, meta_ref, xloc_out):
    tm = h2_ref.shape[0]
    pos1 = meta_ref[0, 0:1, :]
    pos2 = meta_ref[0, 1:2, :]
    s_iota = lax.broadcasted_iota(jnp.int32, (SLOTS, tm), 0).astype(F32)
    perm = jnp.where((s_iota == pos1) | (s_iota == pos2), 1.0, 0.0).astype(BF16)
    xloc_out[0] = _dot(perm, h2_ref[...]).astype(BF16)


def _perm(h2, meta):
    n_tiles = meta.shape[0]
    tm = T_LOC
    return pl.pallas_call(
        _perm_kernel,
        grid=(n_tiles,),
        in_specs=[pl.BlockSpec((tm, D_MODEL), lambda i: (i, 0)),
                  pl.BlockSpec((1, 8, tm), lambda i: (i, 0, 0))],
        out_specs=pl.BlockSpec((1, SLOTS, D_MODEL), lambda i: (i, 0, 0)),
        out_shape=jax.ShapeDtypeStruct((n_tiles, SLOTS, D_MODEL), BF16),
        compiler_params=_cparams(("arbitrary",)),
        name="perm",
    )(h2, meta)


def _piece_copies(src_ref, dst_ref, sem, src_row, dst_row, n_rows):
    copies = []
    for size in _PIECE_SIZES:
        done = n_rows & ~(2 * size - 1)
        pred = (n_rows & size) != 0
        s0 = pl.multiple_of(src_row + done, BF16_ROWS)
        d0 = pl.multiple_of(dst_row + done, BF16_ROWS)
        cp = pltpu.make_async_copy(src_ref.at[pl.ds(s0, size)], dst_ref.at[pl.ds(d0, size)], sem)
        copies.append((pred, cp))
    return copies


def _regroup_kernel(src_tab, dst_tab, cnt_tab, ztab_dst, ztab_cnt, n_used, xloc_ref, zeros_ref, xs_ref, sem):
    j = pl.program_id(0)

    @pl.when(j == 0)
    def _():
        def fill(t, carry):
            cp = pltpu.make_async_copy(zeros_ref, xs_ref.at[pl.ds(pl.multiple_of(t * TM_FFN, TM_FFN), TM_FFN)], sem)
            cp.start()
            cp.wait()
            return carry
        lax.fori_loop(n_used[0], xs_ref.shape[0] // TM_FFN, fill, 0)

    copies = []
    for e in range(MOE_EXPERTS):
        idx = j * MOE_EXPERTS + e
        copies += _piece_copies(xloc_ref, xs_ref, sem, src_tab[idx], dst_tab[idx], cnt_tab[idx])

    zcopies = []
    for e in range(MOE_EXPERTS):
        zcopies += _piece_copies(zeros_ref, xs_ref, sem, 0, ztab_dst[e], ztab_cnt[e])

    for pred, cp in copies:
        @pl.when(pred)
        def _(cp=cp):
            cp.start()
    for pred, cp in zcopies:
        @pl.when(pred & (j == 0))
        def _(cp=cp):
            cp.start()
    for pred, cp in copies:
        @pl.when(pred)
        def _(cp=cp):
            cp.wait()
    for pred, cp in zcopies:
        @pl.when(pred & (j == 0))
        def _(cp=cp):
            cp.wait()


def _regroup(xloc, zeros_rows, tabs, r_max):
    n_tiles = xloc.shape[0]
    xloc2 = xloc.reshape(n_tiles * SLOTS, D_MODEL)
    grid_spec = pltpu.PrefetchScalarGridSpec(
        num_scalar_prefetch=6,
        grid=(n_tiles,),
        in_specs=[pl.BlockSpec(memory_space=pl.ANY), pl.BlockSpec(memory_space=pl.ANY)],
        out_specs=pl.BlockSpec(memory_space=pl.ANY),
        scratch_shapes=[pltpu.SemaphoreType.DMA(())],
    )
    return pl.pallas_call(
        _regroup_kernel,
        grid_spec=grid_spec,
        out_shape=jax.ShapeDtypeStruct((r_max, D_MODEL), BF16),
        compiler_params=pltpu.CompilerParams(dimension_semantics=("arbitrary",)),
        name="regroup",
    )(tabs["src"], tabs["dst"], tabs["cnt"], tabs["zdst"], tabs["zcnt"], tabs["n_used"], xloc2, zeros_rows)


def _ffn_kernel(tile_e, tile_blk, n_used, x_ref, w1_ref, w3_ref, w2_ref, y_ref):
    i = pl.program_id(0)

    @pl.when(i < n_used[0])
    def _():
        x = x_ref[...]
        acc = None
        for f in range(D_EXPERT // F_CHUNK):
            cols = slice(f * F_CHUNK, (f + 1) * F_CHUNK)
            h1 = _dot(x, w1_ref[0, :, cols])
            h3 = _dot(x, w3_ref[0, :, cols])
            act = (_silu(h1) * h3).astype(BF16)
            part = _dot(act, w2_ref[0, cols, :])
            acc = part if acc is None else acc + part
        y_ref[...] = acc.astype(BF16)

    @pl.when(i >= n_used[0])
    def _():
        y_ref[...] = jnp.zeros_like(y_ref)


def _ffn(xs, w1, w3, w2, tabs):
    r_max = xs.shape[0]
    n_mt = r_max // TM_FFN
    grid_spec = pltpu.PrefetchScalarGridSpec(
        num_scalar_prefetch=3,
        grid=(n_mt,),
        in_specs=[
            pl.BlockSpec((TM_FFN, D_MODEL), lambda i, te, tb, nu: (tb[i], 0)),
            pl.BlockSpec((1, D_MODEL, D_EXPERT), lambda i, te, tb, nu: (te[i], 0, 0)),
            pl.BlockSpec((1, D_MODEL, D_EXPERT), lambda i, te, tb, nu: (te[i], 0, 0)),
            pl.BlockSpec((1, D_EXPERT, D_MODEL), lambda i, te, tb, nu: (te[i], 0, 0)),
        ],
        out_specs=pl.BlockSpec((TM_FFN, D_MODEL), lambda i, te, tb, nu: (i, 0)),
    )
    return pl.pallas_call(
        _ffn_kernel,
        grid_spec=grid_spec,
        out_shape=jax.ShapeDtypeStruct((r_max, D_MODEL), BF16),
        compiler_params=_cparams(("arbitrary",)),
        name="ffn",
    )(tabs["tile_e"], tabs["tile_blk"], tabs["n_used"], xs, w1, w3, w2)


def _combine_kernel(src_tab, dst_tab, cnt_tab, y_ref, metac_ref, x1_ref, mod_ref, out_ref, yloc_ref, sem):
    j = pl.program_id(0)
    d = D_MODEL
    tm = x1_ref.shape[0]

    @pl.when(j == 0)
    def _():
        yloc_ref[...] = jnp.zeros_like(yloc_ref)

    copies = []
    for e in range(MOE_EXPERTS):
        idx = j * MOE_EXPERTS + e
        copies += _piece_copies(y_ref, yloc_ref, sem, dst_tab[idx], src_tab[idx], cnt_tab[idx])
    for pred, cp in copies:
        @pl.when(pred)
        def _(cp=cp):
            cp.start()
    for pred, cp in copies:
        @pl.when(pred)
        def _(cp=cp):
            cp.wait()

    mc = metac_ref[0]
    s_iota = lax.broadcasted_iota(jnp.int32, (tm, SLOTS), 1).astype(F32)
    wc = jnp.where(s_iota == mc[:, 0:1], mc[:, 2:3], 0.0) + jnp.where(s_iota == mc[:, 1:2], mc[:, 3:4], 0.0)
    moe = _dot(wc.astype(BF16), yloc_ref[...])
    gate2 = mod_ref[0, :, 5 * d:6 * d]
    out_ref[...] = x1_ref[...] + gate2 * moe


def _combine(y, metac, x1, n_batch, mod_rows, mod_row0, tabs):
    n_total = x1.shape[0]
    n_tok = n_total // n_batch
    tm = T_LOC
    tpb = n_tok // tm
    n_tiles = n_total // tm
    if mod_row0 == 0:
        mod_map = lambda i, *_: (i // tpb, 0, 0)
    else:
        mod_map = lambda i, *_: (mod_row0, 0, 0)
    grid_spec = pltpu.PrefetchScalarGridSpec(
        num_scalar_prefetch=3,
        grid=(n_tiles,),
        in_specs=[
            pl.BlockSpec(memory_space=pl.ANY),
            pl.BlockSpec((1, tm, 8), lambda i, *_: (i, 0, 0)),
            pl.BlockSpec((tm, D_MODEL), lambda i, *_: (i, 0)),
            pl.BlockSpec((1, 1, 6 * D_MODEL), mod_map),
        ],
        out_specs=pl.BlockSpec((tm, D_MODEL), lambda i, *_: (i, 0)),
        scratch_shapes=[pltpu.VMEM((SLOTS, D_MODEL), BF16), pltpu.SemaphoreType.DMA(())],
    )
    return pl.pallas_call(
        _combine_kernel,
        grid_spec=grid_spec,
        out_shape=jax.ShapeDtypeStruct((n_total, D_MODEL), F32),
        compiler_params=_cparams(("arbitrary",)),
        name="combine",
    )(tabs["src"], tabs["dst"], tabs["cnt"], y, metac, x1, mod_rows)


def _rope_tables(n_tokens):
    rows = n_tokens // GRID_W
    row = jnp.repeat(jnp.arange(rows, dtype=jnp.int32), GRID_W)
    col = jnp.tile(jnp.arange(GRID_W, dtype=jnp.int32), rows)
    n_freq = MLA_ROPE // 4
    inv_freq = ROPE_THETA ** (-jnp.arange(n_freq, dtype=F32) / n_freq)
    ang_r = row[:, None].astype(F32) * inv_freq
    ang_c = col[:, None].astype(F32) * inv_freq
    cos = jnp.concatenate([jnp.cos(ang_r)] * 2 + [jnp.cos(ang_c)] * 2, axis=-1)
    sin = jnp.concatenate([-jnp.sin(ang_r), jnp.sin(ang_r), -jnp.sin(ang_c), jnp.sin(ang_c)], axis=-1)
    return cos, sin


_SWAP64 = np.concatenate([np.arange(16, 32), np.arange(0, 16), np.arange(48, 64), np.arange(32, 48)])


def _position_tables(cos, sin, q_gain, k_gain):
    gq = q_gain[MLA_NOPE:]
    gk = k_gain[MLA_NOPE:]
    cq = jnp.tile(cos * gq, (1, MLA_HEADS))
    sq = jnp.tile(sin * gq[_SWAP64], (1, MLA_HEADS))
    tk = jnp.concatenate([cos * gk, sin * gk[_SWAP64]], axis=-1)
    return {"cq": cq, "sq": sq, "tk": tk}


def _layer_weights(l, p):
    d = D_MODEL
    o = _IN_OFF
    w_in = p["w_in"][l]
    seg = lambda i: w_in[:, o[i]:o[i + 1]]
    kpe = seg(2)
    w_in_r = jnp.concatenate(
        [seg(0), seg(1), seg(3), seg(4), seg(5) * (GLA_DK ** -0.5), seg(6), seg(7), seg(8),
         kpe, kpe[:, _SWAP64], seg(9), seg(10), jnp.zeros((d, Z_WIDTH - Z_GG - 2 * GLA_GATE_RANK), F32)],
        axis=-1).astype(BF16)
    w_uq = p["mla_w_uq"][l].reshape(MLA_Q_RANK, MLA_HEADS, MLA_QK)
    q_nope = w_uq[:, :, :MLA_NOPE].reshape(MLA_Q_RANK, -1)
    q_pe = w_uq[:, :, MLA_NOPE:]
    w_uq_r = jnp.concatenate([q_nope, q_pe.reshape(MLA_Q_RANK, -1), q_pe[:, :, _SWAP64].reshape(MLA_Q_RANK, -1)],
                             axis=-1).astype(BF16)
    w_ukv = p["mla_w_ukv"][l].reshape(MLA_KV_RANK, MLA_HEADS, MLA_NOPE + MLA_V)
    w_ukv_r = jnp.concatenate([w_ukv[:, :, :MLA_NOPE].reshape(MLA_KV_RANK, -1),
                               w_ukv[:, :, MLA_NOPE:].reshape(MLA_KV_RANK, -1)], axis=-1).astype(BF16)
    wg = jnp.zeros((LANES, 2 * GLA_QK_W), F32)
    wg = wg.at[0:GLA_GATE_RANK, 0:GLA_QK_W].set(p["gla_wg_f"][l])
    wg = wg.at[GLA_GATE_RANK:2 * GLA_GATE_RANK, GLA_QK_W:].set(p["gla_wg_b"][l])
    bg = jnp.concatenate([p["gla_bg_f"][l], p["gla_bg_b"][l]])[None, :]
    sgu_b = jnp.repeat(p["sgu_b"][l].T, SGU_GROUP_DIM, axis=1)
    wr = jnp.zeros((32, d), F32)
    wr = wr.at[0:MOE_EXPERTS].set(p["moe_w_expert"][l].T)
    wr = wr.at[MOE_EXPERTS:MOE_EXPERTS + MOE_GROUPS].set(p["moe_w_group"][l].T)
    wr_hi = wr.astype(BF16)
    wr_lo = (wr - wr_hi.astype(F32)).astype(BF16)
    br = jnp.zeros((32,), F32).at[0:MOE_EXPERTS].set(p["moe_b_expert"][l])
    br = br.at[MOE_EXPERTS:MOE_EXPERTS + MOE_GROUPS].set(p["moe_b_group"][l])
    return {
        "n1": p["norm1_g"][l][None, :], "n2": p["norm2_g"][l][None, :],
        "w_in": w_in_r, "q_norm": p["mla_q_norm"][l][None, :], "kv_norm": p["mla_kv_norm"][l][None, :],
        "w_uq": w_uq_r, "w_ukv": w_ukv_r,
        "gq_nope": p["mla_q_gain"][l][None, :MLA_NOPE], "gk_nope": p["mla_k_gain"][l][None, :MLA_NOPE],
        "sgu_norm": p["sgu_norm"][l].reshape(1, SGU_WIDTH), "sgu_w": p["sgu_w"][l].astype(BF16), "sgu_b": sgu_b,
        "wg": wg.astype(BF16), "bg": bg,
        "gla_out_norm": p["gla_out_norm"][l][None, :], "w_out": p["w_out"][l].astype(BF16),
        "wr_hi": wr_hi, "wr_lo": wr_lo, "br": jnp.broadcast_to(br[:, None], (32, LANES)),
        "w1": p["moe_w1"][l].astype(BF16), "w3": p["moe_w3"][l].astype(BF16), "w2": p["moe_w2"][l].astype(BF16),
    }


def _router_constants():
    t = np.arange(T_LOC)
    lst = (t[:, None] < t[None, :]).astype(np.float32)
    e = np.arange(MOE_EXPERTS)
    sl = (e[None, :] < e[:, None]).astype(np.float32)
    return {"lst": jnp.asarray(lst, BF16), "sl": jnp.asarray(sl, BF16)}


def _dispatch_tables(cnt, r_max):
    n_tiles = cnt.shape[0]
    cnt = cnt.astype(jnp.int32)
    loc_off = jnp.cumsum(cnt, axis=1) - cnt
    src = loc_off + (jnp.arange(n_tiles, dtype=jnp.int32) * SLOTS)[:, None]
    rows_e = jnp.sum(cnt, axis=0)
    tiles_e = (rows_e + TM_FFN - 1) // TM_FFN
    tile_start = jnp.cumsum(tiles_e) - tiles_e
    base_e = tile_start * TM_FFN
    dst = base_e[None, :] + jnp.cumsum(cnt, axis=0) - cnt
    n_used = jnp.sum(tiles_e)
    n_mt = r_max // TM_FFN
    ti = jnp.arange(n_mt, dtype=jnp.int32)
    tile_blk = jnp.minimum(ti, n_used - 1)
    tile_e = jnp.sum(tile_blk[:, None] >= (tile_start + tiles_e)[None, :], axis=1).astype(jnp.int32)
    tile_e = jnp.minimum(tile_e, MOE_EXPERTS - 1)
    zdst = base_e + rows_e
    zcnt = tiles_e * TM_FFN - rows_e
    i32 = lambda a: a.astype(jnp.int32)
    return {"src": i32(src.reshape(-1)), "loc": i32(loc_off.reshape(-1)), "dst": i32(dst.reshape(-1)),
            "cnt": i32(cnt.reshape(-1)), "zdst": i32(zdst), "zcnt": i32(zcnt),
            "tile_e": tile_e, "tile_blk": i32(tile_blk), "n_used": i32(n_used.reshape(1))}


def _moe(h2, meta, cnt, x1, n_batch, mod_rows, mod_row0, lw, zeros_rows):
    n_tiles = meta.shape[0]
    worst_rows = n_tiles * SLOTS + MOE_EXPERTS * (TM_FFN - BF16_ROWS)
    r_max = -(-worst_rows // TM_FFN) * TM_FFN
    tabs = _dispatch_tables(cnt[:, :, 0], r_max)
    xloc = _perm(h2, meta)
    xs = _regroup(xloc, zeros_rows, tabs, r_max)
    y = _ffn(xs, lw["w1"], lw["w3"], lw["w2"], tabs)
    metac = jnp.swapaxes(meta, 1, 2)
    ctabs = {"src": tabs["loc"], "dst": tabs["dst"], "cnt": tabs["cnt"]}
    return _combine(y, metac, x1, n_batch, mod_rows, mod_row0, ctabs)


def kernel(x, c, ctx, c_ctx, norm1_g, norm2_g, ada_w, ada_b, w_in, mla_q_norm, mla_w_uq, mla_kv_norm, mla_w_ukv,
           mla_q_gain, mla_k_gain, sgu_norm, sgu_w, sgu_b, gla_wg_f, gla_bg_f, gla_wg_b, gla_bg_b, gla_out_norm,
           w_out, moe_w_group, moe_b_group, moe_w_expert, moe_b_expert, moe_w1, moe_w3, moe_w2):
    p = dict(norm1_g=norm1_g, norm2_g=norm2_g, w_in=w_in, mla_q_norm=mla_q_norm, mla_w_uq=mla_w_uq,
             mla_kv_norm=mla_kv_norm, mla_w_ukv=mla_w_ukv, mla_q_gain=mla_q_gain, mla_k_gain=mla_k_gain,
             sgu_norm=sgu_norm, sgu_w=sgu_w, sgu_b=sgu_b, gla_wg_f=gla_wg_f, gla_bg_f=gla_bg_f,
             gla_wg_b=gla_wg_b, gla_bg_b=gla_bg_b, gla_out_norm=gla_out_norm, w_out=w_out,
             moe_w_group=moe_w_group, moe_b_group=moe_b_group, moe_w_expert=moe_w_expert,
             moe_b_expert=moe_b_expert, moe_w1=moe_w1, moe_w3=moe_w3, moe_w2=moe_w2)
    n_batch, seq, d = x.shape
    n_ctx = ctx.shape[1]
    depth = ada_w.shape[0]
    assert d == D_MODEL and n_batch <= 4 and seq % T_LOC == 0 and n_ctx % TM_IN == 0
    assert (n_batch * n_ctx) % T_LOC == 0

    c_rows = jnp.zeros((8, d), F32).at[0:n_batch].set(c).at[4].set(c_ctx)
    mod = _ada(c_rows, ada_w, ada_b)

    cos, sin = _rope_tables(seq)
    ones_c = jnp.ones((TM_IN, MLA_ROPE), F32)
    gla_consts_np = _gla_constants()
    gla_consts = (jnp.asarray(gla_consts_np[0], BF16), jnp.asarray(gla_consts_np[1], F32))
    rconst = _router_constants()
    zeros_rows = jnp.zeros((TM_FFN, d), BF16)
    zero_state = jnp.zeros((2, n_batch, GLA_DV, GLA_QK_W), F32)

    x2 = x.reshape(n_batch * seq, d)
    ctx2 = ctx.reshape(n_batch * n_ctx, d)
    for l in range(depth):
        ctx_out = l < depth - 1
        lw = _layer_weights(l, p)
        mod_rows = mod[l].reshape(8, 1, 6 * d)
        tabs_x = _position_tables(cos, sin, mla_q_gain[l], mla_k_gain[l])
        tabs_c = _position_tables(ones_c, 0.0 * ones_c, mla_q_gain[l], mla_k_gain[l])

        q_x, k_x, v_x, b_x, gla_x, r_x = _inproj(x2, n_batch, mod_rows, 0, lw, tabs_x)
        q_c, k_c, v_c, b_c, gla_c, r_c = _inproj(ctx2, n_batch, mod_rows, 4, lw, tabs_c)

        of_c, ob_c, s_ctx = _gla(gla_c.reshape(n_batch, n_ctx, -1), zero_state, gla_consts)
        of_x, ob_x, _ = _gla(gla_x.reshape(n_batch, seq, -1), s_ctx, gla_consts)
        a_x = _attention(q_x, [k_x, k_c], [v_x, v_c])

        x1, h2, meta, cnt = _outproj(a_x.reshape(n_batch * seq, -1), b_x, of_x.reshape(n_batch * seq, -1),
                                     ob_x.reshape(n_batch * seq, -1), r_x, x2, n_batch, mod_rows, 0, lw, rconst)
        x2 = _moe(h2, meta, cnt, x1, n_batch, mod_rows, 0, lw, zeros_rows)

        if ctx_out:
            a_c = _attention(q_c, [k_c], [v_c])
            c1, h2c, meta_c, cnt_c = _outproj(a_c.reshape(n_batch * n_ctx, -1), b_c,
                                              of_c.reshape(n_batch * n_ctx, -1), ob_c.reshape(n_batch * n_ctx, -1),
                                              r_c, ctx2, n_batch, mod_rows, 4, lw, rconst)
            ctx2 = _moe(h2c, meta_c, cnt_c, c1, n_batch, mod_rows, 4, lw, zeros_rows)
    return x2.reshape(n_batch, seq, d)
```

```python
import functools

import numpy as np
import jax
import jax.numpy as jnp
from jax import lax
from jax.experimental import pallas as pl
from jax.experimental.pallas import tpu as pltpu

F32 = jnp.float32
BF16 = jnp.bfloat16

D_MODEL = 2048
GRID_W = 64
EPS = 1e-6

MLA_HEADS = 8
MLA_Q_RANK = 512
MLA_KV_RANK = 512
MLA_NOPE = 128
MLA_ROPE = 64
MLA_V = 128
MLA_QK = MLA_NOPE + MLA_ROPE
MLA_SCALE = MLA_QK ** -0.5
ROPE_THETA = 10000.0

SGU_WIDTH = 512
SGU_GROUPS = 4
SGU_GROUP_DIM = SGU_WIDTH // SGU_GROUPS
SGU_CHUNK = 128

GLA_HEADS = 4
GLA_DK = 64
GLA_DV = 128
GLA_GATE_RANK = 16
GLA_GATE_NORM = 16.0
GLA_CHUNK = 64
GLA_QK_W = GLA_HEADS * GLA_DK
GLA_V_W = GLA_HEADS * GLA_DV

MOE_GROUPS = 4
MOE_PER_GROUP = 4
MOE_EXPERTS = MOE_GROUPS * MOE_PER_GROUP
D_EXPERT = 1024

_IN_SPLITS = (512, 512, 64, 512, 512, 256, 256, 512, 512, 16, 16)
_IN_OFF = np.concatenate([[0], np.cumsum(_IN_SPLITS)]).tolist()

Z_CQ, Z_CKV, Z_ZU, Z_ZV = 0, 512, 1024, 1536
Z_GQ, Z_GK, Z_GV, Z_GR = 2048, 2304, 2560, 3072
Z_KPE, Z_GG = 3584, 3712
Z_WIDTH = 3840

LANES = 128
BF16_ROWS = 16
VMEM_LIMIT = 56 * 1024 * 1024

TM_IN = 256
TQ_ATTN = 256
T_LOC = 512
SLOTS = 2 * T_LOC + MOE_EXPERTS * BF16_ROWS
TM_FFN = 512
F_CHUNK = 256
_PIECE_SIZES = (512, 256, 128, 64, 32, 16)


def _cparams(sem, vmem=VMEM_LIMIT):
    return pltpu.CompilerParams(dimension_semantics=sem, vmem_limit_bytes=vmem)


def _silu(x):
    return x / (1.0 + jnp.exp(-x))


def _gelu_tanh(x):
    return 0.5 * x * (1.0 + jnp.tanh(np.sqrt(2.0 / np.pi).astype(np.float32) * (x + 0.044715 * (x * x * x))))


def _log_sigmoid(x):
    return jnp.minimum(x, 0.0) - jnp.log(1.0 + jnp.exp(-jnp.abs(x)))


def _dot(a, b):
    return jnp.dot(a, b, preferred_element_type=F32)


def _dot_nt(a, b):
    return lax.dot_general(a, b, (((1,), (1,)), ((), ())), preferred_element_type=F32)


def _dot_tn(a, b):
    return lax.dot_general(a, b, (((0,), (0,)), ((), ())), preferred_element_type=F32)


def _ada_kernel(c_ref, w_ref, b_ref, o_ref):
    a = _silu(c_ref[...]).astype(BF16)
    o_ref[0] = _dot(a, w_ref[0].astype(BF16)) + b_ref[0]


def _ada(c_rows, ada_w, ada_b):
    depth, d, n = ada_w.shape
    tn = 1536
    return pl.pallas_call(
        _ada_kernel,
        grid=(depth, n // tn),
        in_specs=[
            pl.BlockSpec((8, d), lambda l, j: (0, 0)),
            pl.BlockSpec((1, d, tn), lambda l, j: (l, 0, j)),
            pl.BlockSpec((1, 1, tn), lambda l, j: (l, 0, j)),
        ],
        out_specs=pl.BlockSpec((1, 8, tn), lambda l, j: (l, 0, j)),
        out_shape=jax.ShapeDtypeStruct((depth, 8, n), F32),
        compiler_params=_cparams(("arbitrary", "arbitrary")),
        name="ada",
    )(c_rows, ada_w, ada_b.reshape(depth, 1, n))


def _inproj_kernel(x_ref, mod_ref, n1_ref, win_ref, qn_ref, wuq_ref, kvn_ref, wukv_ref,
                   gqn_ref, gkn_ref, cq_ref, sq_ref, tk_ref,
                   sgn_ref, sgw_ref, sgb_ref, wg_ref, bg_ref,
                   q_out, k_out, v_out, b_out, gla_out, r_out):
    d = D_MODEL
    tm = x_ref.shape[0]
    x = x_ref[...]
    shift = mod_ref[0, :, 0:d]
    scale = mod_ref[0, :, d:2 * d]
    ms = jnp.mean(x * x, axis=-1, keepdims=True)
    h = (x * lax.rsqrt(ms + EPS) * n1_ref[...]) * (1.0 + scale) + shift
    z = _dot(h.astype(BF16), win_ref[...])

    cq = z[:, Z_CQ:Z_CQ + MLA_Q_RANK]
    ckv = z[:, Z_CKV:Z_CKV + MLA_KV_RANK]
    cqn = cq * lax.rsqrt(jnp.mean(cq * cq, axis=-1, keepdims=True) + EPS) * qn_ref[...]
    ckvn = ckv * lax.rsqrt(jnp.mean(ckv * ckv, axis=-1, keepdims=True) + EPS) * kvn_ref[...]
    qa = _dot(cqn.astype(BF16), wuq_ref[...])
    kv = _dot(ckvn.astype(BF16), wukv_ref[...])
    npe = MLA_HEADS * MLA_NOPE
    rw = MLA_HEADS * MLA_ROPE
    q_pe = qa[:, npe:npe + rw]
    q_rot = q_pe * cq_ref[...] + qa[:, npe + rw:npe + 2 * rw] * sq_ref[...]
    zk = z[:, Z_KPE:Z_KPE + 2 * MLA_ROPE]
    tkz = zk * tk_ref[...]
    k_rot = tkz[:, 0:MLA_ROPE] + tkz[:, MLA_ROPE:2 * MLA_ROPE]
    k_pe = zk[:, 0:MLA_ROPE]
    ss_kpe = jnp.sum(k_pe * k_pe, axis=-1, keepdims=True)
    gqn = gqn_ref[...]
    gkn = gkn_ref[...]
    for hh in range(MLA_HEADS):
        qn_h = qa[:, hh * MLA_NOPE:(hh + 1) * MLA_NOPE]
        qp_h = q_pe[:, hh * MLA_ROPE:(hh + 1) * MLA_ROPE]
        ssq = jnp.sum(qn_h * qn_h, axis=-1, keepdims=True) + jnp.sum(qp_h * qp_h, axis=-1, keepdims=True)
        rq = lax.rsqrt(ssq * (1.0 / MLA_QK) + EPS) * MLA_SCALE
        q_h = jnp.concatenate([qn_h * gqn * rq, q_rot[:, hh * MLA_ROPE:(hh + 1) * MLA_ROPE] * rq], axis=-1)
        q_out[0, hh] = q_h.astype(BF16)
        kn_h = kv[:, hh * MLA_NOPE:(hh + 1) * MLA_NOPE]
        ssk = jnp.sum(kn_h * kn_h, axis=-1, keepdims=True) + ss_kpe
        rk = lax.rsqrt(ssk * (1.0 / MLA_QK) + EPS)
        k_h = jnp.concatenate([kn_h * gkn * rk, k_rot * rk], axis=-1)
        k_out[0, hh] = k_h.astype(BF16)
        v_out[0, hh] = kv[:, npe + hh * MLA_V:npe + (hh + 1) * MLA_V].astype(BF16)

    u = _gelu_tanh(z[:, Z_ZU:Z_ZU + SGU_WIDTH])
    vv = _gelu_tanh(z[:, Z_ZV:Z_ZV + SGU_WIDTH])
    sgn = sgn_ref[...]
    for c in range(tm // SGU_CHUNK):
        rows = slice(c * SGU_CHUNK, (c + 1) * SGU_CHUNK)
        outs = []
        for g in range(SGU_GROUPS):
            cols = slice(g * SGU_GROUP_DIM, (g + 1) * SGU_GROUP_DIM)
            seg = vv[rows, cols]
            vn = seg * lax.rsqrt(jnp.mean(seg * seg, axis=-1, keepdims=True) + EPS) * sgn[:, cols]
            outs.append(_dot(sgw_ref[g], vn.astype(BF16)))
        sp = jnp.concatenate(outs, axis=-1) + sgb_ref[...]
        b_out[rows, :] = (u[rows, :] * sp).astype(BF16)

    gl = _dot(z[:, Z_GG:Z_GG + LANES].astype(BF16), wg_ref[...]) + bg_ref[...]
    gdec = _log_sigmoid(gl) * (1.0 / GLA_GATE_NORM)
    gla_out[:, 0:2 * GLA_QK_W] = z[:, Z_GQ:Z_GQ + 2 * GLA_QK_W]
    gla_out[:, 2 * GLA_QK_W:4 * GLA_QK_W] = gdec
    gla_out[:, 4 * GLA_QK_W:4 * GLA_QK_W + GLA_V_W] = z[:, Z_GV:Z_GV + GLA_V_W]
    r_out[...] = z[:, Z_GR:Z_GR + GLA_V_W]


def _inproj(x2, n_batch, mod_rows, mod_row0, lw, tabs):
    n_total = x2.shape[0]
    n_tok = n_total // n_batch
    tm = TM_IN
    tpb = n_tok // tm
    tab_blocks = tabs["cq"].shape[0] // tm
    if mod_row0 == 0:
        mod_map = lambda i: (i // tpb, 0, 0)
    else:
        mod_map = lambda i: (mod_row0, 0, 0)
    const2 = lambda i: (0, 0)
    tab_map = lambda i: (i % tab_blocks, 0)
    tok_map = lambda i: (i, 0)
    head_map = lambda i: (i // tpb, 0, i % tpb, 0)
    in_specs = [
        pl.BlockSpec((tm, D_MODEL), tok_map),
        pl.BlockSpec((1, 1, 6 * D_MODEL), mod_map),
        pl.BlockSpec((1, D_MODEL), const2),
        pl.BlockSpec((D_MODEL, Z_WIDTH), const2, pipeline_mode=pl.Buffered(1)),
        pl.BlockSpec((1, MLA_Q_RANK), const2),
        pl.BlockSpec((MLA_Q_RANK, 2048), const2, pipeline_mode=pl.Buffered(1)),
        pl.BlockSpec((1, MLA_KV_RANK), const2),
        pl.BlockSpec((MLA_KV_RANK, 2048), const2, pipeline_mode=pl.Buffered(1)),
        pl.BlockSpec((1, MLA_NOPE), const2),
        pl.BlockSpec((1, MLA_NOPE), const2),
        pl.BlockSpec((tm, MLA_HEADS * MLA_ROPE), tab_map),
        pl.BlockSpec((tm, MLA_HEADS * MLA_ROPE), tab_map),
        pl.BlockSpec((tm, 2 * MLA_ROPE), tab_map),
        pl.BlockSpec((1, SGU_WIDTH), const2),
        pl.BlockSpec((SGU_GROUPS, SGU_CHUNK, SGU_CHUNK), lambda i: (0, 0, 0)),
        pl.BlockSpec((SGU_CHUNK, SGU_WIDTH), const2),
        pl.BlockSpec((LANES, 2 * GLA_QK_W), const2),
        pl.BlockSpec((1, 2 * GLA_QK_W), const2),
    ]
    out_shape = [
        jax.ShapeDtypeStruct((n_batch, MLA_HEADS, n_tok, MLA_QK), BF16),
        jax.ShapeDtypeStruct((n_batch, MLA_HEADS, n_tok, MLA_QK), BF16),
        jax.ShapeDtypeStruct((n_batch, MLA_HEADS, n_tok, MLA_V), BF16),
        jax.ShapeDtypeStruct((n_total, SGU_WIDTH), BF16),
        jax.ShapeDtypeStruct((n_total, 4 * GLA_QK_W + GLA_V_W), F32),
        jax.ShapeDtypeStruct((n_total, GLA_V_W), F32),
    ]
    out_specs = [
        pl.BlockSpec((1, MLA_HEADS, tm, MLA_QK), head_map),
        pl.BlockSpec((1, MLA_HEADS, tm, MLA_QK), head_map),
        pl.BlockSpec((1, MLA_HEADS, tm, MLA_V), head_map),
        pl.BlockSpec((tm, SGU_WIDTH), tok_map),
        pl.BlockSpec((tm, 4 * GLA_QK_W + GLA_V_W), tok_map),
        pl.BlockSpec((tm, GLA_V_W), tok_map),
    ]
    return pl.pallas_call(
        _inproj_kernel,
        grid=(n_total // tm,),
        in_specs=in_specs,
        out_specs=out_specs,
        out_shape=out_shape,
        compiler_params=_cparams(("arbitrary",)),
        name="inproj",
    )(x2, mod_rows, lw["n1"], lw["w_in"], lw["q_norm"], lw["w_uq"], lw["kv_norm"], lw["w_ukv"],
      lw["gq_nope"], lw["gk_nope"], tabs["cq"], tabs["sq"], tabs["tk"],
      lw["sgu_norm"], lw["sgu_w"], lw["sgu_b"], lw["wg"], lw["bg"])


_GLA_LEVELS = (32, 16, 8, 4, 2, 1)
_N_LEV = len(_GLA_LEVELS)
_R_QI = 2 * _N_LEV * GLA_CHUNK
_R_KS = _R_QI + GLA_CHUNK
_R_TOT = _R_KS + GLA_CHUNK
_M_ROWS = _R_TOT + 8


def _gla_constants():
    c = GLA_CHUNK
    t = np.arange(c)[:, None]
    j = np.arange(c)[None, :]
    mq, mk, masks = [], [], []
    for m in _GLA_LEVELS:
        upper = (t // m) % 2 == 1
        bound = (t // (2 * m)) * 2 * m + m - 1
        mq.append((upper & (j > bound) & (j <= t)).astype(np.float32))
        mk.append((~upper & (j > t) & (j <= bound)).astype(np.float32))
        s = j
        masks.append(((t // (2 * m) == s // (2 * m)) & upper & ((s // m) % 2 == 0)).astype(np.float32))
    masks.append((t == j).astype(np.float32))
    mqi = (j <= t).astype(np.float32)
    mks = (j > t).astype(np.float32)
    tot = np.ones((8, c), np.float32)
    m_f = np.concatenate(mq + mk + [mqi, mks, tot], axis=0)
    flip = lambda a: a[::-1, ::-1]
    m_b = np.concatenate([flip(a) for a in mq] + [flip(a) for a in mk] + [flip(mqi), flip(mks), tot], axis=0)
    mask_f = np.stack([np.tile(a, (GLA_HEADS, 1)) for a in masks])
    mask_b = np.stack([np.tile(flip(a), (GLA_HEADS, 1)) for a in masks])
    return np.stack([m_f, m_b]), np.stack([mask_f, mask_b])


def _stack_heads(a, lane_head):
    return jnp.concatenate([jnp.where(lane_head == hh, a, jnp.zeros_like(a)) for hh in range(GLA_HEADS)], axis=0)


def _gla_chunk(q, k, g, v, st, mall, masks, lane_head):
    c = GLA_CHUNK
    g_hi = g.astype(BF16)
    r1 = g - g_hi.astype(F32)
    g_mid = r1.astype(BF16)
    g_lo = (r1 - g_mid.astype(F32)).astype(BF16)
    e3 = _dot(mall, jnp.concatenate([g_hi, g_mid, g_lo], axis=1))
    w = GLA_QK_W
    e = e3[:, 0:w] + e3[:, w:2 * w] + e3[:, 2 * w:3 * w]
    a = masks[_N_LEV] * _dot_nt(_stack_heads(q.astype(BF16), lane_head), k.astype(BF16))
    for lev in range(_N_LEV):
        qt = (q * jnp.exp(e[lev * c:(lev + 1) * c])).astype(BF16)
        kt = (k * jnp.exp(e[(_N_LEV + lev) * c:(_N_LEV + lev + 1) * c])).astype(BF16)
        a = a + masks[lev] * _dot_nt(_stack_heads(qt, lane_head), kt)
    qi = (q * jnp.exp(e[_R_QI:_R_QI + c])).astype(BF16)
    o_inter = _dot_nt(_stack_heads(qi, lane_head), st.astype(BF16))
    vb = v.astype(BF16)
    a16 = a.astype(BF16)
    outs = []
    for hh in range(GLA_HEADS):
        rows = slice(hh * c, (hh + 1) * c)
        outs.append(o_inter[rows] + _dot(a16[rows], vb[:, hh * GLA_DV:(hh + 1) * GLA_DV]))
    o = jnp.concatenate(outs, axis=-1)
    kd = (k * jnp.exp(e[_R_KS:_R_KS + c])).astype(BF16)
    upd = _dot_tn(vb, kd)
    st_new = st * jnp.exp(e[_R_TOT:_R_TOT + 1])
    for hh in range(GLA_HEADS):
        st_new = st_new + jnp.where(lane_head == hh, upd[hh * GLA_DV:(hh + 1) * GLA_DV], 0.0)
    return o, st_new


def _gla_kernel(fwd_ref, bwd_ref, s0_ref, mall_ref, mask_ref, of_ref, ob_ref, sfin_ref, st_ref):
    step = pl.program_id(0)
    n_batch = fwd_ref.shape[0]

    @pl.when(step == 0)
    def _():
        st_ref[...] = s0_ref[...]

    lane_head = lax.broadcasted_iota(jnp.int32, (1, GLA_QK_W), 1) // GLA_DK
    w = GLA_QK_W
    for direction, (in_ref, o_ref) in enumerate(((fwd_ref, of_ref), (bwd_ref, ob_ref))):
        mall = mall_ref[direction]
        masks = [mask_ref[direction, i] for i in range(_N_LEV + 1)]

        def body(b, carry, in_ref=in_ref, o_ref=o_ref, direction=direction, mall=mall, masks=masks):
            blk = in_ref[b]
            q = blk[:, 0:w]
            k = blk[:, w:2 * w]
            g = blk[:, (2 + direction) * w:(3 + direction) * w]
            v = blk[:, 4 * w:4 * w + GLA_V_W]
            o, st_new = _gla_chunk(q, k, g, v, st_ref[direction, b], mall, masks, lane_head)
            o_ref[b] = o
            st_ref[direction, b] = st_new
            return carry

        lax.fori_loop(0, n_batch, body, 0)

    @pl.when(step == pl.num_programs(0) - 1)
    def _():
        sfin_ref[...] = st_ref[...]


def _gla(gla_in, s0, consts):
    n_batch, n_tok, width = gla_in.shape
    nc = n_tok // GLA_CHUNK
    mall, masks = consts
    st_shape = (2, n_batch, GLA_DV, GLA_QK_W)
    return pl.pallas_call(
        _gla_kernel,
        grid=(nc,),
        in_specs=[
            pl.BlockSpec((n_batch, GLA_CHUNK, width), lambda c: (0, c, 0)),
            pl.BlockSpec((n_batch, GLA_CHUNK, width), lambda c: (0, nc - 1 - c, 0)),
            pl.BlockSpec(st_shape, lambda c: (0, 0, 0, 0)),
            pl.BlockSpec(mall.shape, lambda c: (0, 0, 0)),
            pl.BlockSpec(masks.shape, lambda c: (0, 0, 0, 0)),
        ],
        out_specs=[
            pl.BlockSpec((n_batch, GLA_CHUNK, GLA_V_W), lambda c: (0, c, 0)),
            pl.BlockSpec((n_batch, GLA_CHUNK, GLA_V_W), lambda c: (0, nc - 1 - c, 0)),
            pl.BlockSpec(st_shape, lambda c: (0, 0, 0, 0)),
        ],
        out_shape=[
            jax.ShapeDtypeStruct((n_batch, n_tok, GLA_V_W), F32),
            jax.ShapeDtypeStruct((n_batch, n_tok, GLA_V_W), F32),
            jax.ShapeDtypeStruct(st_shape, F32),
        ],
        scratch_shapes=[pltpu.VMEM(st_shape, F32)],
        compiler_params=_cparams(("arbitrary",)),
        name="gla",
    )(gla_in, gla_in, s0, mall, masks)


def _attn_kernel(*refs, n_seg):
    q_ref = refs[0]
    k_refs = refs[1:1 + n_seg]
    v_refs = refs[1 + n_seg:1 + 2 * n_seg]
    o_ref = refs[1 + 2 * n_seg]
    q = q_ref[0, 0]
    scores = [_dot_nt(q, k_ref[0, 0]) for k_ref in k_refs]
    m = functools.reduce(jnp.maximum, [jnp.max(s, axis=-1, keepdims=True) for s in scores])
    acc = None
    den = None
    for s, v_ref in zip(scores, v_refs):
        p = jnp.exp(s - m)
        l = jnp.sum(p, axis=-1, keepdims=True)
        pv = _dot(p.astype(BF16), v_ref[0, 0])
        acc = pv if acc is None else acc + pv
        den = l if den is None else den + l
    o_ref[0] = (acc / den).astype(BF16)


def _attention(q, ks, vs):
    n_batch, n_heads, nq, _ = q.shape
    tq = TQ_ATTN
    n_seg = len(ks)
    in_specs = [pl.BlockSpec((1, 1, tq, MLA_QK), lambda b, h, i: (b, h, i, 0))]
    for k in ks:
        in_specs.append(pl.BlockSpec((1, 1, k.shape[2], MLA_QK), lambda b, h, i: (b, h, 0, 0)))
    for v in vs:
        in_specs.append(pl.BlockSpec((1, 1, v.shape[2], MLA_V), lambda b, h, i: (b, h, 0, 0)))
    return pl.pallas_call(
        functools.partial(_attn_kernel, n_seg=n_seg),
        grid=(n_batch, n_heads, nq // tq),
        in_specs=in_specs,
        out_specs=pl.BlockSpec((1, tq, MLA_V), lambda b, h, i: (b, i, h)),
        out_shape=jax.ShapeDtypeStruct((n_batch, nq, n_heads * MLA_V), BF16),
        compiler_params=_cparams(("arbitrary", "arbitrary", "arbitrary")),
        name="attn",
    )(q, *ks, *vs)


def _outproj_kernel(a_ref, b_ref, of_ref, ob_ref, r_ref, x_ref, mod_ref, gon_ref, n2_ref, wout_ref,
                    wr_hi_ref, wr_lo_ref, br_ref, lst_ref, sl_ref,
                    x1_out, h2_out, meta_out, cnt_out):
    d = D_MODEL
    tm = x_ref.shape[0]
    gate1 = mod_ref[0, :, 2 * d:3 * d]
    shift2 = mod_ref[0, :, 3 * d:4 * d]
    scale2 = mod_ref[0, :, 4 * d:5 * d]

    og = of_ref[...] + ob_ref[...]
    gon = gon_ref[...]
    parts = []
    for hh in range(GLA_HEADS):
        seg = og[:, hh * GLA_DV:(hh + 1) * GLA_DV]
        parts.append(seg * lax.rsqrt(jnp.mean(seg * seg, axis=-1, keepdims=True) + EPS) * gon)
    cg = jnp.concatenate(parts, axis=-1) * _silu(r_ref[...])
    na = MLA_HEADS * MLA_V
    y = _dot(a_ref[...], wout_ref[0:na, :])
    y = y + _dot(b_ref[...], wout_ref[na:na + SGU_WIDTH, :])
    y = y + _dot(cg.astype(BF16), wout_ref[na + SGU_WIDTH:na + SGU_WIDTH + GLA_V_W, :])
    x1 = x_ref[...] + gate1 * y
    x1_out[...] = x1

    ms = jnp.mean(x1 * x1, axis=-1, keepdims=True)
    h2 = (x1 * lax.rsqrt(ms + EPS) * n2_ref[...]) * (1.0 + scale2) + shift2
    h_hi = h2.astype(BF16)
    h2_out[...] = h_hi
    h_lo = (h2 - h_hi.astype(F32)).astype(BF16)

    wr_hi = wr_hi_ref[...]
    lt = _dot_nt(wr_hi, h_hi) + _dot_nt(wr_hi, h_lo) + _dot_nt(wr_lo_ref[...], h_hi) + br_ref[:, 0:1]
    grp = [lt[MOE_EXPERTS + g:MOE_EXPERTS + g + 1, :] for g in range(MOE_GROUPS)]
    gmax = functools.reduce(jnp.maximum, grp)
    gsel = jnp.full(gmax.shape, MOE_GROUPS - 1, jnp.int32)
    for g in range(MOE_GROUPS - 2, -1, -1):
        gsel = jnp.where(grp[g] == gmax, g, gsel)
    gsum = functools.reduce(jnp.add, [jnp.exp(gl - gmax) for gl in grp])
    g_w = 1.0 / gsum
    e_in = []
    for j in range(MOE_PER_GROUP):
        ej = lt[j:j + 1, :]
        for g in range(1, MOE_GROUPS):
            ej = jnp.where(gsel == g, lt[g * MOE_PER_GROUP + j:g * MOE_PER_GROUP + j + 1, :], ej)
        e_in.append(ej)
    m1 = functools.reduce(jnp.maximum, e_in)
    i1 = jnp.full(m1.shape, MOE_PER_GROUP - 1, jnp.int32)
    for j in range(MOE_PER_GROUP - 2, -1, -1):
        i1 = jnp.where(e_in[j] == m1, j, i1)
    e_rest = [jnp.where(i1 == j, -jnp.inf, e_in[j]) for j in range(MOE_PER_GROUP)]
    m2 = functools.reduce(jnp.maximum, e_rest)
    i2 = jnp.full(m2.shape, MOE_PER_GROUP - 1, jnp.int32)
    for j in range(MOE_PER_GROUP - 2, -1, -1):
        i2 = jnp.where(e_rest[j] == m2, j, i2)
    p2 = jnp.exp(m2 - m1)
    w1 = g_w / (1.0 + p2)
    w2 = g_w * p2 / (1.0 + p2)
    eid1 = gsel * MOE_PER_GROUP + i1
    eid2 = gsel * MOE_PER_GROUP + i2

    e_iota = lax.broadcasted_iota(jnp.int32, (MOE_EXPERTS, tm), 0)
    hit1 = e_iota == eid1
    hit2 = e_iota == eid2
    onehot = jnp.where(hit1, 1.0, 0.0) + jnp.where(hit2, 1.0, 0.0)
    rank = _dot(onehot.astype(BF16), lst_ref[...])
    counts = jnp.sum(onehot, axis=1, keepdims=True)
    padded = jnp.ceil(counts * (1.0 / BF16_ROWS)) * BF16_ROWS
    padded_l = jnp.broadcast_to(padded, (MOE_EXPERTS, LANES))
    off = _dot(sl_ref[...], padded_l.astype(BF16))
    pos_e = off[:, 0:1] + rank
    pos1 = jnp.sum(jnp.where(hit1, pos_e, 0.0), axis=0, keepdims=True)
    pos2 = jnp.sum(jnp.where(hit2, pos_e, 0.0), axis=0, keepdims=True)
    zero = jnp.zeros_like(pos1)
    meta_out[0] = jnp.concatenate(
        [pos1, pos2, w1, w2, eid1.astype(F32), eid2.astype(F32), zero, zero], axis=0)
    cnt_out[0] = padded_l


def _outproj(a, b, o_f, o_b, r, x2, n_batch, mod_rows, mod_row0, lw, rconst):
    n_total = x2.shape[0]
    n_tok = n_total // n_batch
    tm = T_LOC
    tpb = n_tok // tm
    if mod_row0 == 0:
        mod_map = lambda i: (i // tpb, 0, 0)
    else:
        mod_map = lambda i: (mod_row0, 0, 0)
    const2 = lambda i: (0, 0)
    tok_map = lambda i: (i, 0)
    n_tiles = n_total // tm
    in_specs = [
        pl.BlockSpec((tm, MLA_HEADS * MLA_V), tok_map),
        pl.BlockSpec((tm, SGU_WIDTH), tok_map),
        pl.BlockSpec((tm, GLA_V_W), tok_map),
        pl.BlockSpec((tm, GLA_V_W), tok_map),
        pl.BlockSpec((tm, GLA_V_W), tok_map),
        pl.BlockSpec((tm, D_MODEL), tok_map),
        pl.BlockSpec((1, 1, 6 * D_MODEL), mod_map),
        pl.BlockSpec((1, GLA_DV), const2),
        pl.BlockSpec((1, D_MODEL), const2),
        pl.BlockSpec((D_MODEL, D_MODEL), const2, pipeline_mode=pl.Buffered(1)),
        pl.BlockSpec((32, D_MODEL), const2),
        pl.BlockSpec((32, D_MODEL), const2),
        pl.BlockSpec((32, LANES), const2),
        pl.BlockSpec((tm, tm), const2),
        pl.BlockSpec((MOE_EXPERTS, MOE_EXPERTS), const2),
    ]
    out_shape = [
        jax.ShapeDtypeStruct((n_total, D_MODEL), F32),
        jax.ShapeDtypeStruct((n_total, D_MODEL), BF16),
        jax.ShapeDtypeStruct((n_tiles, 8, tm), F32),
        jax.ShapeDtypeStruct((n_tiles, MOE_EXPERTS, LANES), F32),
    ]
    out_specs = [
        pl.BlockSpec((tm, D_MODEL), tok_map),
        pl.BlockSpec((tm, D_MODEL), tok_map),
        pl.BlockSpec((1, 8, tm), lambda i: (i, 0, 0)),
        pl.BlockSpec((1, MOE_EXPERTS, LANES), lambda i: (i, 0, 0)),
    ]
    return pl.pallas_call(
        _outproj_kernel,
        grid=(n_tiles,),
        in_specs=in_specs,
        out_specs=out_specs,
        out_shape=out_shape,
        compiler_params=_cparams(("arbitrary",)),
        name="outproj",
    )(a, b, o_f, o_b, r, x2, mod_rows, lw["gla_out_norm"], lw["n2"], lw["w_out"],
      lw["wr_hi"], lw["wr_lo"], lw["br"], rconst["lst"], rconst["sl"])


def _piece_copies(src_ref, dst_ref, sem, src_row, dst_row, n_rows):
    copies = []
    for size in _PIECE_SIZES:
        done = n_rows & ~(2 * size - 1)
        pred = (n_rows & size) != 0
        s0 = pl.multiple_of(src_row + done, BF16_ROWS)
        d0 = pl.multiple_of(dst_row + done, BF16_ROWS)
        cp = pltpu.make_async_copy(src_ref.at[pl.ds(s0, size)], dst_ref.at[pl.ds(d0, size)], sem)
        copies.append((pred, cp))
    return copies


def _dispatch_kernel(loc_tab, dst_tab, cnt_tab, ztab_dst, ztab_cnt, n_used, h2_ref, meta_ref, xs_ref,
                     xloc_ref, zeros_ref, sem):
    j = pl.program_id(0)
    tm = h2_ref.shape[0]

    @pl.when(j == 0)
    def _():
        zeros_ref[...] = jnp.zeros_like(zeros_ref)

        def fill(t, carry):
            cp = pltpu.make_async_copy(zeros_ref, xs_ref.at[pl.ds(pl.multiple_of(t * TM_FFN, TM_FFN), TM_FFN)], sem)
            cp.start()
            cp.wait()
            return carry
        lax.fori_loop(n_used[0], xs_ref.shape[0] // TM_FFN, fill, 0)
        zcopies = []
        for e in range(MOE_EXPERTS):
            zcopies += _piece_copies(zeros_ref, xs_ref, sem, 0, ztab_dst[e], ztab_cnt[e])
        for pred, cp in zcopies:
            @pl.when(pred)
            def _(cp=cp):
                cp.start()
        for pred, cp in zcopies:
            @pl.when(pred)
            def _(cp=cp):
                cp.wait()

    pos1 = meta_ref[0, 0:1, :]
    pos2 = meta_ref[0, 1:2, :]
    s_iota = lax.broadcasted_iota(jnp.int32, (SLOTS, tm), 0).astype(F32)
    perm = jnp.where((s_iota == pos1) | (s_iota == pos2), 1.0, 0.0).astype(BF16)
    xloc_ref[...] = _dot(perm, h2_ref[...]).astype(BF16)

    copies = []
    for e in range(MOE_EXPERTS):
        idx = j * MOE_EXPERTS + e
        copies += _piece_copies(xloc_ref, xs_ref, sem, loc_tab[idx], dst_tab[idx], cnt_tab[idx])
    for pred, cp in copies:
        @pl.when(pred)
        def _(cp=cp):
            cp.start()
    for pred, cp in copies:
        @pl.when(pred)
        def _(cp=cp):
            cp.wait()


def _dispatch(h2, meta, tabs, r_max):
    n_tiles = meta.shape[0]
    tm = T_LOC
    grid_spec = pltpu.PrefetchScalarGridSpec(
        num_scalar_prefetch=6,
        grid=(n_tiles,),
        in_specs=[pl.BlockSpec((tm, D_MODEL), lambda i, *_: (i, 0)),
                  pl.BlockSpec((1, 8, tm), lambda i, *_: (i, 0, 0))],
        out_specs=pl.BlockSpec(memory_space=pl.ANY),
        scratch_shapes=[pltpu.VMEM((SLOTS, D_MODEL), BF16), pltpu.VMEM((TM_FFN, D_MODEL), BF16),
                        pltpu.SemaphoreType.DMA(())],
    )
    return pl.pallas_call(
        _dispatch_kernel,
        grid_spec=grid_spec,
        out_shape=jax.ShapeDtypeStruct((r_max, D_MODEL), BF16),
        compiler_params=_cparams(("arbitrary",)),
        name="dispatch",
    )(tabs["loc"], tabs["dst"], tabs["cnt"], tabs["zdst"], tabs["zcnt"], tabs["n_used"], h2, meta)


def _ffn_kernel(tile_e, tile_blk, n_used, x_ref, w1_ref, w3_ref, w2_ref, y_ref):
    i = pl.program_id(0)

    @pl.when(i < n_used[0])
    def _():
        x = x_ref[...]
        acc = None
        for f in range(D_EXPERT // F_CHUNK):
            cols = slice(f * F_CHUNK, (f + 1) * F_CHUNK)
            h1 = _dot(x, w1_ref[0, :, cols])
            h3 = _dot(x, w3_ref[0, :, cols])
            act = (_silu(h1) * h3).astype(BF16)
            part = _dot(act, w2_ref[0, cols, :])
            acc = part if acc is None else acc + part
        y_ref[...] = acc.astype(BF16)

    @pl.when(i >= n_used[0])
    def _():
        y_ref[...] = jnp.zeros_like(y_ref)


def _ffn(xs, w1, w3, w2, tabs):
    r_max = xs.shape[0]
    n_mt = r_max // TM_FFN
    grid_spec = pltpu.PrefetchScalarGridSpec(
        num_scalar_prefetch=3,
        grid=(n_mt,),
        in_specs=[
            pl.BlockSpec((TM_FFN, D_MODEL), lambda i, te, tb, nu: (tb[i], 0)),
            pl.BlockSpec((1, D_MODEL, D_EXPERT), lambda i, te, tb, nu: (te[i], 0, 0)),
            pl.BlockSpec((1, D_MODEL, D_EXPERT), lambda i, te, tb, nu: (te[i], 0, 0)),
            pl.BlockSpec((1, D_EXPERT, D_MODEL), lambda i, te, tb, nu: (te[i], 0, 0)),
        ],
        out_specs=pl.BlockSpec((TM_FFN, D_MODEL), lambda i, te, tb, nu: (i, 0)),
    )
    return pl.pallas_call(
        _ffn_kernel,
        grid_spec=grid_spec,
        out_shape=jax.ShapeDtypeStruct((r_max, D_MODEL), BF16),
        compiler_params=_cparams(("arbitrary",)),
        name="ffn",
    )(tabs["tile_e"], tabs["tile_blk"], tabs["n_used"], xs, w1, w3, w2)


def _combine_kernel(loc_tab, dst_tab, cnt_tab, y_ref, metac_ref, x1_ref, mod_ref, out_ref, yloc_ref, sem):
    j = pl.program_id(0)
    d = D_MODEL
    tm = x1_ref.shape[0]

    @pl.when(j == 0)
    def _():
        yloc_ref[...] = jnp.zeros_like(yloc_ref)

    copies = []
    for e in range(MOE_EXPERTS):
        idx = j * MOE_EXPERTS + e
        copies += _piece_copies(y_ref, yloc_ref, sem, dst_tab[idx], loc_tab[idx], cnt_tab[idx])
    for pred, cp in copies:
        @pl.when(pred)
        def _(cp=cp):
            cp.start()
    for pred, cp in copies:
        @pl.when(pred)
        def _(cp=cp):
            cp.wait()

    mc = metac_ref[0]
    s_iota = lax.broadcasted_iota(jnp.int32, (tm, SLOTS), 1).astype(F32)
    wc = jnp.where(s_iota == mc[:, 0:1], mc[:, 2:3], 0.0) + jnp.where(s_iota == mc[:, 1:2], mc[:, 3:4], 0.0)
    moe = _dot(wc.astype(BF16), yloc_ref[...])
    gate2 = mod_ref[0, :, 5 * d:6 * d]
    out_ref[...] = x1_ref[...] + gate2 * moe


def _combine(y, metac, x1, n_batch, mod_rows, mod_row0, tabs):
    n_total = x1.shape[0]
    n_tok = n_total // n_batch
    tm = T_LOC
    tpb = n_tok // tm
    n_tiles = n_total // tm
    if mod_row0 == 0:
        mod_map = lambda i, *_: (i // tpb, 0, 0)
    else:
        mod_map = lambda i, *_: (mod_row0, 0, 0)
    grid_spec = pltpu.PrefetchScalarGridSpec(
        num_scalar_prefetch=3,
        grid=(n_tiles,),
        in_specs=[
            pl.BlockSpec(memory_space=pl.ANY),
            pl.BlockSpec((1, tm, 8), lambda i, *_: (i, 0, 0)),
            pl.BlockSpec((tm, D_MODEL), lambda i, *_: (i, 0)),
            pl.BlockSpec((1, 1, 6 * D_MODEL), mod_map),
        ],
        out_specs=pl.BlockSpec((tm, D_MODEL), lambda i, *_: (i, 0)),
        scratch_shapes=[pltpu.VMEM((SLOTS, D_MODEL), BF16), pltpu.SemaphoreType.DMA(())],
    )
    return pl.pallas_call(
        _combine_kernel,
        grid_spec=grid_spec,
        out_shape=jax.ShapeDtypeStruct((n_total, D_MODEL), F32),
        compiler_params=_cparams(("arbitrary",)),
        name="combine",
    )(tabs["loc"], tabs["dst"], tabs["cnt"], y, metac, x1, mod_rows)


def _rope_tables(n_tokens):
    rows = n_tokens // GRID_W
    row = jnp.repeat(jnp.arange(rows, dtype=jnp.int32), GRID_W)
    col = jnp.tile(jnp.arange(GRID_W, dtype=jnp.int32), rows)
    n_freq = MLA_ROPE // 4
    inv_freq = ROPE_THETA ** (-jnp.arange(n_freq, dtype=F32) / n_freq)
    ang_r = row[:, None].astype(F32) * inv_freq
    ang_c = col[:, None].astype(F32) * inv_freq
    cos = jnp.concatenate([jnp.cos(ang_r)] * 2 + [jnp.cos(ang_c)] * 2, axis=-1)
    sin = jnp.concatenate([-jnp.sin(ang_r), jnp.sin(ang_r), -jnp.sin(ang_c), jnp.sin(ang_c)], axis=-1)
    return cos, sin


_SWAP64 = np.concatenate([np.arange(16, 32), np.arange(0, 16), np.arange(48, 64), np.arange(32, 48)])


def _position_tables(cos, sin, q_gain, k_gain):
    gq = q_gain[MLA_NOPE:]
    gk = k_gain[MLA_NOPE:]
    cq = jnp.tile(cos * gq, (1, MLA_HEADS))
    sq = jnp.tile(sin * gq[_SWAP64], (1, MLA_HEADS))
    tk = jnp.concatenate([cos * gk, sin * gk[_SWAP64]], axis=-1)
    return {"cq": cq, "sq": sq, "tk": tk}


def _layer_weights(l, p):
    d = D_MODEL
    o = _IN_OFF
    w_in = p["w_in"][l]
    seg = lambda i: w_in[:, o[i]:o[i + 1]]
    kpe = seg(2)
    w_in_r = jnp.concatenate(
        [seg(0), seg(1), seg(3), seg(4), seg(5) * (GLA_DK ** -0.5), seg(6), seg(7), seg(8),
         kpe, kpe[:, _SWAP64], seg(9), seg(10), jnp.zeros((d, Z_WIDTH - Z_GG - 2 * GLA_GATE_RANK), F32)],
        axis=-1).astype(BF16)
    w_uq = p["mla_w_uq"][l].reshape(MLA_Q_RANK, MLA_HEADS, MLA_QK)
    q_nope = w_uq[:, :, :MLA_NOPE].reshape(MLA_Q_RANK, -1)
    q_pe = w_uq[:, :, MLA_NOPE:]
    w_uq_r = jnp.concatenate([q_nope, q_pe.reshape(MLA_Q_RANK, -1), q_pe[:, :, _SWAP64].reshape(MLA_Q_RANK, -1)],
                             axis=-1).astype(BF16)
    w_ukv = p["mla_w_ukv"][l].reshape(MLA_KV_RANK, MLA_HEADS, MLA_NOPE + MLA_V)
    w_ukv_r = jnp.concatenate([w_ukv[:, :, :MLA_NOPE].reshape(MLA_KV_RANK, -1),
                               w_ukv[:, :, MLA_NOPE:].reshape(MLA_KV_RANK, -1)], axis=-1).astype(BF16)
    wg = jnp.zeros((LANES, 2 * GLA_QK_W), F32)
    wg = wg.at[0:GLA_GATE_RANK, 0:GLA_QK_W].set(p["gla_wg_f"][l])
    wg = wg.at[GLA_GATE_RANK:2 * GLA_GATE_RANK, GLA_QK_W:].set(p["gla_wg_b"][l])
    bg = jnp.concatenate([p["gla_bg_f"][l], p["gla_bg_b"][l]])[None, :]
    sgu_b = jnp.repeat(p["sgu_b"][l].T, SGU_GROUP_DIM, axis=1)
    wr = jnp.zeros((32, d), F32)
    wr = wr.at[0:MOE_EXPERTS].set(p["moe_w_expert"][l].T)
    wr = wr.at[MOE_EXPERTS:MOE_EXPERTS + MOE_GROUPS].set(p["moe_w_group"][l].T)
    wr_hi = wr.astype(BF16)
    wr_lo = (wr - wr_hi.astype(F32)).astype(BF16)
    br = jnp.zeros((32,), F32).at[0:MOE_EXPERTS].set(p["moe_b_expert"][l])
    br = br.at[MOE_EXPERTS:MOE_EXPERTS + MOE_GROUPS].set(p["moe_b_group"][l])
    return {
        "n1": p["norm1_g"][l][None, :], "n2": p["norm2_g"][l][None, :],
        "w_in": w_in_r, "q_norm": p["mla_q_norm"][l][None, :], "kv_norm": p["mla_kv_norm"][l][None, :],
        "w_uq": w_uq_r, "w_ukv": w_ukv_r,
        "gq_nope": p["mla_q_gain"][l][None, :MLA_NOPE], "gk_nope": p["mla_k_gain"][l][None, :MLA_NOPE],
        "sgu_norm": p["sgu_norm"][l].reshape(1, SGU_WIDTH), "sgu_w": p["sgu_w"][l].astype(BF16), "sgu_b": sgu_b,
        "wg": wg.astype(BF16), "bg": bg,
        "gla_out_norm": p["gla_out_norm"][l][None, :], "w_out": p["w_out"][l].astype(BF16),
        "wr_hi": wr_hi, "wr_lo": wr_lo, "br": jnp.broadcast_to(br[:, None], (32, LANES)),
        "w1": p["moe_w1"][l].astype(BF16), "w3": p["moe_w3"][l].astype(BF16), "w2": p["moe_w2"][l].astype(BF16),
    }


def _router_constants():
    t = np.arange(T_LOC)
    lst = (t[:, None] < t[None, :]).astype(np.float32)
    e = np.arange(MOE_EXPERTS)
    sl = (e[None, :] < e[:, None]).astype(np.float32)
    return {"lst": jnp.asarray(lst, BF16), "sl": jnp.asarray(sl, BF16)}


def _dispatch_tables(cnt, r_max):
    n_tiles = cnt.shape[0]
    cnt = cnt.astype(jnp.int32)
    loc_off = jnp.cumsum(cnt, axis=1) - cnt
    rows_e = jnp.sum(cnt, axis=0)
    tiles_e = (rows_e + TM_FFN - 1) // TM_FFN
    tile_start = jnp.cumsum(tiles_e) - tiles_e
    base_e = tile_start * TM_FFN
    dst = base_e[None, :] + jnp.cumsum(cnt, axis=0) - cnt
    n_used = jnp.sum(tiles_e)
    n_mt = r_max // TM_FFN
    ti = jnp.arange(n_mt, dtype=jnp.int32)
    tile_blk = jnp.minimum(ti, n_used - 1)
    tile_e = jnp.sum(tile_blk[:, None] >= (tile_start + tiles_e)[None, :], axis=1).astype(jnp.int32)
    tile_e = jnp.minimum(tile_e, MOE_EXPERTS - 1)
    zdst = base_e + rows_e
    zcnt = tiles_e * TM_FFN - rows_e
    i32 = lambda a: a.astype(jnp.int32)
    return {"loc": i32(loc_off.reshape(-1)), "dst": i32(dst.reshape(-1)),
            "cnt": i32(cnt.reshape(-1)), "zdst": i32(zdst), "zcnt": i32(zcnt),
            "tile_e": tile_e, "tile_blk": i32(tile_blk), "n_used": i32(n_used.reshape(1))}


def _moe(h2, meta, cnt, x1, n_batch, mod_rows, mod_row0, lw):
    n_tiles = meta.shape[0]
    worst_rows = n_tiles * SLOTS + MOE_EXPERTS * (TM_FFN - BF16_ROWS)
    r_max = -(-worst_rows // TM_FFN) * TM_FFN
    tabs = _dispatch_tables(cnt[:, :, 0], r_max)
    xs = _dispatch(h2, meta, tabs, r_max)
    y = _ffn(xs, lw["w1"], lw["w3"], lw["w2"], tabs)
    metac = jnp.swapaxes(meta, 1, 2)
    return _combine(y, metac, x1, n_batch, mod_rows, mod_row0, tabs)


def kernel(x, c, ctx, c_ctx, norm1_g, norm2_g, ada_w, ada_b, w_in, mla_q_norm, mla_w_uq, mla_kv_norm, mla_w_ukv,
           mla_q_gain, mla_k_gain, sgu_norm, sgu_w, sgu_b, gla_wg_f, gla_bg_f, gla_wg_b, gla_bg_b, gla_out_norm,
           w_out, moe_w_group, moe_b_group, moe_w_expert, moe_b_expert, moe_w1, moe_w3, moe_w2):
    p = dict(norm1_g=norm1_g, norm2_g=norm2_g, w_in=w_in, mla_q_norm=mla_q_norm, mla_w_uq=mla_w_uq,
             mla_kv_norm=mla_kv_norm, mla_w_ukv=mla_w_ukv, mla_q_gain=mla_q_gain, mla_k_gain=mla_k_gain,
             sgu_norm=sgu_norm, sgu_w=sgu_w, sgu_b=sgu_b, gla_wg_f=gla_wg_f, gla_bg_f=gla_bg_f,
             gla_wg_b=gla_wg_b, gla_bg_b=gla_bg_b, gla_out_norm=gla_out_norm, w_out=w_out,
             moe_w_group=moe_w_group, moe_b_group=moe_b_group, moe_w_expert=moe_w_expert,
             moe_b_expert=moe_b_expert, moe_w1=moe_w1, moe_w3=moe_w3, moe_w2=moe_w2)
    n_batch, seq, d = x.shape
    n_ctx = ctx.shape[1]
    depth = ada_w.shape[0]
    assert d == D_MODEL and n_batch <= 4 and seq % T_LOC == 0 and n_ctx % TM_IN == 0
    assert (n_batch * n_ctx) % T_LOC == 0

    c_rows = jnp.zeros((8, d), F32).at[0:n_batch].set(c).at[4].set(c_ctx)
    mod = _ada(c_rows, ada_w, ada_b)

    cos, sin = _rope_tables(seq)
    ones_c = jnp.ones((TM_IN, MLA_ROPE), F32)
    gla_consts_np = _gla_constants()
    gla_consts = (jnp.asarray(gla_consts_np[0], BF16), jnp.asarray(gla_consts_np[1], F32))
    rconst = _router_constants()
    zero_state = jnp.zeros((2, n_batch, GLA_DV, GLA_QK_W), F32)

    x2 = x.reshape(n_batch * seq, d)
    ctx2 = ctx.reshape(n_batch * n_ctx, d)
    for l in range(depth):
        ctx_out = l < depth - 1
        lw = _layer_weights(l, p)
        mod_rows = mod[l].reshape(8, 1, 6 * d)
        tabs_x = _position_tables(cos, sin, mla_q_gain[l], mla_k_gain[l])
        tabs_c = _position_tables(ones_c, 0.0 * ones_c, mla_q_gain[l], mla_k_gain[l])

        q_x, k_x, v_x, b_x, gla_x, r_x = _inproj(x2, n_batch, mod_rows, 0, lw, tabs_x)
        q_c, k_c, v_c, b_c, gla_c, r_c = _inproj(ctx2, n_batch, mod_rows, 4, lw, tabs_c)

        of_c, ob_c, s_ctx = _gla(gla_c.reshape(n_batch, n_ctx, -1), zero_state, gla_consts)
        of_x, ob_x, _ = _gla(gla_x.reshape(n_batch, seq, -1), s_ctx, gla_consts)
        a_x = _attention(q_x, [k_x, k_c], [v_x, v_c])

        x1, h2, meta, cnt = _outproj(a_x.reshape(n_batch * seq, -1), b_x, of_x.reshape(n_batch * seq, -1),
                                     ob_x.reshape(n_batch * seq, -1), r_x, x2, n_batch, mod_rows, 0, lw, rconst)
        x2 = _moe(h2, meta, cnt, x1, n_batch, mod_rows, 0, lw)

        if ctx_out:
            a_c = _attention(q_c, [k_c], [v_c])
            c1, h2c, meta_c, cnt_c = _outproj(a_c.reshape(n_batch * n_ctx, -1), b_c,
                                              of_c.reshape(n_batch * n_ctx, -1), ob_c.reshape(n_batch * n_ctx, -1),
                                              r_c, ctx2, n_batch, mod_rows, 4, lw, rconst)
            ctx2 = _moe(h2c, meta_c, cnt_c, c1, n_batch, mod_rows, 4, lw)
    return x2.reshape(n_batch, seq, d)
```

```python
import functools

import numpy as np
import jax
import jax.numpy as jnp
from jax import lax
from jax.experimental import pallas as pl
from jax.experimental.pallas import tpu as pltpu

F32 = jnp.float32
BF16 = jnp.bfloat16

D_MODEL = 2048
GRID_W = 64
EPS = 1e-6

MLA_HEADS = 8
MLA_Q_RANK = 512
MLA_KV_RANK = 512
MLA_NOPE = 128
MLA_ROPE = 64
MLA_V = 128
MLA_QK = MLA_NOPE + MLA_ROPE
MLA_SCALE = MLA_QK ** -0.5
ROPE_THETA = 10000.0
LOG2_E = float(np.log2(np.e))

SGU_WIDTH = 512
SGU_GROUPS = 4
SGU_GROUP_DIM = SGU_WIDTH // SGU_GROUPS
SGU_CHUNK = 128

GLA_HEADS = 4
GLA_DK = 64
GLA_DV = 128
GLA_GATE_RANK = 16
GLA_GATE_NORM = 16.0
GLA_CHUNK = 64
GLA_QK_W = GLA_HEADS * GLA_DK
GLA_V_W = GLA_HEADS * GLA_DV

MOE_GROUPS = 4
MOE_PER_GROUP = 4
MOE_EXPERTS = MOE_GROUPS * MOE_PER_GROUP
D_EXPERT = 1024

_IN_SPLITS = (512, 512, 64, 512, 512, 256, 256, 512, 512, 16, 16)
_IN_OFF = np.concatenate([[0], np.cumsum(_IN_SPLITS)]).tolist()

Z_CQ, Z_CKV, Z_ZU, Z_ZV = 0, 512, 1024, 1536
Z_GQ, Z_GK, Z_GV, Z_GR = 2048, 2304, 2560, 3072
Z_KPE, Z_GG = 3584, 3712
Z_WIDTH = 3840

LANES = 128
BF16_ROWS = 16
VMEM_LIMIT = 56 * 1024 * 1024

TM_IN = 256
TQ_ATTN = 512
TK_ATTN = 512
T_LOC = 512
SLOTS = 2 * T_LOC + MOE_EXPERTS * BF16_ROWS
TM_FFN = 512
F_CHUNK = 256
_PIECE_SIZES = (512, 256, 128, 64, 32, 16)


def _cparams(sem, vmem=VMEM_LIMIT):
    return pltpu.CompilerParams(dimension_semantics=sem, vmem_limit_bytes=vmem)


def _silu(x):
    return x / (1.0 + jnp.exp(-x))


def _gelu_tanh(x):
    return 0.5 * x * (1.0 + jnp.tanh(np.sqrt(2.0 / np.pi).astype(np.float32) * (x + 0.044715 * (x * x * x))))


def _log_sigmoid(x):
    return jnp.minimum(x, 0.0) - jnp.log(1.0 + jnp.exp(-jnp.abs(x)))


def _dot(a, b):
    return jnp.dot(a, b, preferred_element_type=F32)


def _dot_nt(a, b):
    return lax.dot_general(a, b, (((1,), (1,)), ((), ())), preferred_element_type=F32)


def _dot_tn(a, b):
    return lax.dot_general(a, b, (((0,), (0,)), ((), ())), preferred_element_type=F32)


def _ada_kernel(c_ref, w_ref, b_ref, o_ref):
    a = _silu(c_ref[...]).astype(BF16)
    o_ref[0] = _dot(a, w_ref[0].astype(BF16)) + b_ref[0]


def _ada(c_rows, ada_w, ada_b):
    depth, d, n = ada_w.shape
    tn = 1536
    return pl.pallas_call(
        _ada_kernel,
        grid=(depth, n // tn),
        in_specs=[
            pl.BlockSpec((8, d), lambda l, j: (0, 0)),
            pl.BlockSpec((1, d, tn), lambda l, j: (l, 0, j)),
            pl.BlockSpec((1, 1, tn), lambda l, j: (l, 0, j)),
        ],
        out_specs=pl.BlockSpec((1, 8, tn), lambda l, j: (l, 0, j)),
        out_shape=jax.ShapeDtypeStruct((depth, 8, n), F32),
        compiler_params=_cparams(("arbitrary", "arbitrary")),
        name="ada",
    )(c_rows, ada_w, ada_b.reshape(depth, 1, n))


def _inproj_kernel(x_ref, mod_ref, n1_ref, win_ref, qn_ref, wuq_ref, kvn_ref, wukv_ref,
                   gqn_ref, gkn_ref, cq_ref, sq_ref, tk_ref,
                   sgn_ref, sgw_ref, sgb_ref, wg_ref, bg_ref,
                   q_out, k_out, v_out, b_out, gla_out, r_out):
    d = D_MODEL
    tm = x_ref.shape[0]
    x = x_ref[...]
    shift = mod_ref[0, :, 0:d]
    scale = mod_ref[0, :, d:2 * d]
    ms = jnp.mean(x * x, axis=-1, keepdims=True)
    h = (x * lax.rsqrt(ms + EPS) * n1_ref[...]) * (1.0 + scale) + shift
    z = _dot(h.astype(BF16), win_ref[...])

    cq = z[:, Z_CQ:Z_CQ + MLA_Q_RANK]
    ckv = z[:, Z_CKV:Z_CKV + MLA_KV_RANK]
    cqn = cq * lax.rsqrt(jnp.mean(cq * cq, axis=-1, keepdims=True) + EPS) * qn_ref[...]
    ckvn = ckv * lax.rsqrt(jnp.mean(ckv * ckv, axis=-1, keepdims=True) + EPS) * kvn_ref[...]
    qa = _dot(cqn.astype(BF16), wuq_ref[...])
    kv = _dot(ckvn.astype(BF16), wukv_ref[...])
    npe = MLA_HEADS * MLA_NOPE
    rw = MLA_HEADS * MLA_ROPE
    q_pe = qa[:, npe:npe + rw]
    q_rot = q_pe * cq_ref[...] + qa[:, npe + rw:npe + 2 * rw] * sq_ref[...]
    zk = z[:, Z_KPE:Z_KPE + 2 * MLA_ROPE]
    tkz = zk * tk_ref[...]
    k_rot = tkz[:, 0:MLA_ROPE] + tkz[:, MLA_ROPE:2 * MLA_ROPE]
    k_pe = zk[:, 0:MLA_ROPE]
    ss_kpe = jnp.sum(k_pe * k_pe, axis=-1, keepdims=True)
    gqn = gqn_ref[...]
    gkn = gkn_ref[...]
    for hh in range(MLA_HEADS):
        qn_h = qa[:, hh * MLA_NOPE:(hh + 1) * MLA_NOPE]
        qp_h = q_pe[:, hh * MLA_ROPE:(hh + 1) * MLA_ROPE]
        ssq = jnp.sum(qn_h * qn_h, axis=-1, keepdims=True) + jnp.sum(qp_h * qp_h, axis=-1, keepdims=True)
        rq = lax.rsqrt(ssq * (1.0 / MLA_QK) + EPS) * (MLA_SCALE * LOG2_E)
        q_h = jnp.concatenate([qn_h * gqn * rq, q_rot[:, hh * MLA_ROPE:(hh + 1) * MLA_ROPE] * rq], axis=-1)
        q_out[0, hh] = q_h.astype(BF16)
        kn_h = kv[:, hh * MLA_NOPE:(hh + 1) * MLA_NOPE]
        ssk = jnp.sum(kn_h * kn_h, axis=-1, keepdims=True) + ss_kpe
        rk = lax.rsqrt(ssk * (1.0 / MLA_QK) + EPS)
        k_h = jnp.concatenate([kn_h * gkn * rk, k_rot * rk], axis=-1)
        k_out[0, hh] = k_h.astype(BF16)
        v_out[0, hh] = kv[:, npe + hh * MLA_V:npe + (hh + 1) * MLA_V].astype(BF16)

    u = _gelu_tanh(z[:, Z_ZU:Z_ZU + SGU_WIDTH])
    vv = _gelu_tanh(z[:, Z_ZV:Z_ZV + SGU_WIDTH])
    sgn = sgn_ref[...]
    for c in range(tm // SGU_CHUNK):
        rows = slice(c * SGU_CHUNK, (c + 1) * SGU_CHUNK)
        outs = []
        for g in range(SGU_GROUPS):
            cols = slice(g * SGU_GROUP_DIM, (g + 1) * SGU_GROUP_DIM)
            seg = vv[rows, cols]
            vn = seg * lax.rsqrt(jnp.mean(seg * seg, axis=-1, keepdims=True) + EPS) * sgn[:, cols]
            outs.append(_dot(sgw_ref[g], vn.astype(BF16)))
        sp = jnp.concatenate(outs, axis=-1) + sgb_ref[...]
        b_out[rows, :] = (u[rows, :] * sp).astype(BF16)

    gl = _dot(z[:, Z_GG:Z_GG + LANES].astype(BF16), wg_ref[...]) + bg_ref[...]
    gdec = _log_sigmoid(gl) * (1.0 / GLA_GATE_NORM)
    gla_out[:, 0:2 * GLA_QK_W] = z[:, Z_GQ:Z_GQ + 2 * GLA_QK_W]
    gla_out[:, 2 * GLA_QK_W:4 * GLA_QK_W] = gdec
    gla_out[:, 4 * GLA_QK_W:4 * GLA_QK_W + GLA_V_W] = z[:, Z_GV:Z_GV + GLA_V_W]
    r_out[...] = z[:, Z_GR:Z_GR + GLA_V_W]


def _inproj(x2, n_batch, mod_rows, mod_row0, lw, tabs):
    n_total = x2.shape[0]
    n_tok = n_total // n_batch
    tm = TM_IN
    tpb = n_tok // tm
    tab_blocks = tabs["cq"].shape[0] // tm
    if mod_row0 == 0:
        mod_map = lambda i: (i // tpb, 0, 0)
    else:
        mod_map = lambda i: (mod_row0, 0, 0)
    const2 = lambda i: (0, 0)
    tab_map = lambda i: (i % tab_blocks, 0)
    tok_map = lambda i: (i, 0)
    head_map = lambda i: (i // tpb, 0, i % tpb, 0)
    in_specs = [
        pl.BlockSpec((tm, D_MODEL), tok_map),
        pl.BlockSpec((1, 1, 6 * D_MODEL), mod_map),
        pl.BlockSpec((1, D_MODEL), const2),
        pl.BlockSpec((D_MODEL, Z_WIDTH), const2, pipeline_mode=pl.Buffered(1)),
        pl.BlockSpec((1, MLA_Q_RANK), const2),
        pl.BlockSpec((MLA_Q_RANK, 2048), const2, pipeline_mode=pl.Buffered(1)),
        pl.BlockSpec((1, MLA_KV_RANK), const2),
        pl.BlockSpec((MLA_KV_RANK, 2048), const2, pipeline_mode=pl.Buffered(1)),
        pl.BlockSpec((1, MLA_NOPE), const2),
        pl.BlockSpec((1, MLA_NOPE), const2),
        pl.BlockSpec((tm, MLA_HEADS * MLA_ROPE), tab_map),
        pl.BlockSpec((tm, MLA_HEADS * MLA_ROPE), tab_map),
        pl.BlockSpec((tm, 2 * MLA_ROPE), tab_map),
        pl.BlockSpec((1, SGU_WIDTH), const2),
        pl.BlockSpec((SGU_GROUPS, SGU_CHUNK, SGU_CHUNK), lambda i: (0, 0, 0)),
        pl.BlockSpec((SGU_CHUNK, SGU_WIDTH), const2),
        pl.BlockSpec((LANES, 2 * GLA_QK_W), const2),
        pl.BlockSpec((1, 2 * GLA_QK_W), const2),
    ]
    out_shape = [
        jax.ShapeDtypeStruct((n_batch, MLA_HEADS, n_tok, MLA_QK), BF16),
        jax.ShapeDtypeStruct((n_batch, MLA_HEADS, n_tok, MLA_QK), BF16),
        jax.ShapeDtypeStruct((n_batch, MLA_HEADS, n_tok, MLA_V), BF16),
        jax.ShapeDtypeStruct((n_total, SGU_WIDTH), BF16),
        jax.ShapeDtypeStruct((n_total, 4 * GLA_QK_W + GLA_V_W), F32),
        jax.ShapeDtypeStruct((n_total, GLA_V_W), F32),
    ]
    out_specs = [
        pl.BlockSpec((1, MLA_HEADS, tm, MLA_QK), head_map),
        pl.BlockSpec((1, MLA_HEADS, tm, MLA_QK), head_map),
        pl.BlockSpec((1, MLA_HEADS, tm, MLA_V), head_map),
        pl.BlockSpec((tm, SGU_WIDTH), tok_map),
        pl.BlockSpec((tm, 4 * GLA_QK_W + GLA_V_W), tok_map),
        pl.BlockSpec((tm, GLA_V_W), tok_map),
    ]
    return pl.pallas_call(
        _inproj_kernel,
        grid=(n_total // tm,),
        in_specs=in_specs,
        out_specs=out_specs,
        out_shape=out_shape,
        compiler_params=_cparams(("arbitrary",)),
        name="inproj",
    )(x2, mod_rows, lw["n1"], lw["w_in"], lw["q_norm"], lw["w_uq"], lw["kv_norm"], lw["w_ukv"],
      lw["gq_nope"], lw["gk_nope"], tabs["cq"], tabs["sq"], tabs["tk"],
      lw["sgu_norm"], lw["sgu_w"], lw["sgu_b"], lw["wg"], lw["bg"])


_GLA_LEVELS = (32, 16, 8, 4, 2, 1)
_N_LEV = len(_GLA_LEVELS)
_G_SPLIT = 3
_R_QI = _N_LEV * GLA_CHUNK
_R_KS = _R_QI + GLA_CHUNK
_R_TOT = _R_KS + GLA_CHUNK
_M_ROWS = _R_TOT + 8


def _gla_constants():
    c = GLA_CHUNK
    t = np.arange(c)[:, None]
    j = np.arange(c)[None, :]
    lev, masks = [], []
    for m in _GLA_LEVELS:
        upper = (t // m) % 2 == 1
        bound = (t // (2 * m)) * 2 * m + m - 1
        lev.append(np.where(upper, (j > bound) & (j <= t), (j > t) & (j <= bound)).astype(np.float32))
        s = j
        masks.append(((t // (2 * m) == s // (2 * m)) & upper & ((s // m) % 2 == 0)).astype(np.float32))
    masks.append((t == j).astype(np.float32))
    mqi = (j <= t).astype(np.float32)
    mks = (j > t).astype(np.float32)
    tot = np.ones((8, c), np.float32)
    flip = lambda a: a[::-1, ::-1]
    m_f = np.concatenate(lev + [mqi, mks, tot], axis=0)
    m_b = np.concatenate([flip(a) for a in lev] + [flip(mqi), flip(mks), tot], axis=0)
    mask_f = np.stack([np.tile(a, (GLA_HEADS, 1)) for a in masks])
    mask_b = np.stack([np.tile(flip(a), (GLA_HEADS, 1)) for a in masks])
    m_all = np.tile(np.stack([m_f, m_b]), (1, 1, _G_SPLIT))
    return m_all, np.stack([mask_f, mask_b])


def _stack_heads(a, lane_head):
    return jnp.concatenate([jnp.where(lane_head == hh, a, jnp.zeros_like(a)) for hh in range(GLA_HEADS)], axis=0)


def _gla_chunk(q, k, g, v, st, mall, masks, lane_head):
    c = GLA_CHUNK
    g_hi = g.astype(BF16)
    r1 = g - g_hi.astype(F32)
    g_mid = r1.astype(BF16)
    g_lo = (r1 - g_mid.astype(F32)).astype(BF16)
    e = _dot(mall, jnp.concatenate([g_hi, g_mid, g_lo], axis=0))
    a = masks[_N_LEV] * _dot_nt(_stack_heads(q.astype(BF16), lane_head), k.astype(BF16))
    for lev in range(_N_LEV):
        ex = jnp.exp(e[lev * c:(lev + 1) * c])
        a = a + masks[lev] * _dot_nt(_stack_heads((q * ex).astype(BF16), lane_head), (k * ex).astype(BF16))
    qi = (q * jnp.exp(e[_R_QI:_R_QI + c])).astype(BF16)
    o_inter = _dot_nt(_stack_heads(qi, lane_head), st.astype(BF16))
    vb = v.astype(BF16)
    a16 = a.astype(BF16)
    outs = []
    for hh in range(GLA_HEADS):
        rows = slice(hh * c, (hh + 1) * c)
        outs.append(o_inter[rows] + _dot(a16[rows], vb[:, hh * GLA_DV:(hh + 1) * GLA_DV]))
    o = jnp.concatenate(outs, axis=-1)
    kd = (k * jnp.exp(e[_R_KS:_R_KS + c])).astype(BF16)
    upd = _dot_tn(vb, kd)
    st_new = st * jnp.exp(e[_R_TOT:_R_TOT + 1])
    for hh in range(GLA_HEADS):
        st_new = st_new + jnp.where(lane_head == hh, upd[hh * GLA_DV:(hh + 1) * GLA_DV], 0.0)
    return o, st_new


def _gla_kernel(fwd_ref, bwd_ref, s0_ref, mall_ref, mask_ref, of_ref, ob_ref, sfin_ref, st_ref):
    step = pl.program_id(0)
    n_batch = fwd_ref.shape[0]

    @pl.when(step == 0)
    def _():
        st_ref[...] = s0_ref[...]

    lane_head = lax.broadcasted_iota(jnp.int32, (1, GLA_QK_W), 1) // GLA_DK
    w = GLA_QK_W
    for b in range(n_batch):
        for direction, (in_ref, o_ref) in enumerate(((fwd_ref, of_ref), (bwd_ref, ob_ref))):
            mall = mall_ref[direction]
            masks = [mask_ref[direction, i] for i in range(_N_LEV + 1)]
            q = in_ref[b, :, 0:w]
            k = in_ref[b, :, w:2 * w]
            g = in_ref[b, :, (2 + direction) * w:(3 + direction) * w]
            v = in_ref[b, :, 4 * w:4 * w + GLA_V_W]
            o, st_new = _gla_chunk(q, k, g, v, st_ref[direction, b], mall, masks, lane_head)
            o_ref[b] = o
            st_ref[direction, b] = st_new

    @pl.when(step == pl.num_programs(0) - 1)
    def _():
        sfin_ref[...] = st_ref[...]


def _gla(gla_in, s0, consts):
    n_batch, n_tok, width = gla_in.shape
    nc = n_tok // GLA_CHUNK
    mall, masks = consts
    st_shape = (2, n_batch, GLA_DV, GLA_QK_W)
    return pl.pallas_call(
        _gla_kernel,
        grid=(nc,),
        in_specs=[
            pl.BlockSpec((n_batch, GLA_CHUNK, width), lambda c: (0, c, 0)),
            pl.BlockSpec((n_batch, GLA_CHUNK, width), lambda c: (0, nc - 1 - c, 0)),
            pl.BlockSpec(st_shape, lambda c: (0, 0, 0, 0)),
            pl.BlockSpec(mall.shape, lambda c: (0, 0, 0)),
            pl.BlockSpec(masks.shape, lambda c: (0, 0, 0, 0)),
        ],
        out_specs=[
            pl.BlockSpec((n_batch, GLA_CHUNK, GLA_V_W), lambda c: (0, c, 0)),
            pl.BlockSpec((n_batch, GLA_CHUNK, GLA_V_W), lambda c: (0, nc - 1 - c, 0)),
            pl.BlockSpec(st_shape, lambda c: (0, 0, 0, 0)),
        ],
        out_shape=[
            jax.ShapeDtypeStruct((n_batch, n_tok, GLA_V_W), F32),
            jax.ShapeDtypeStruct((n_batch, n_tok, GLA_V_W), F32),
            jax.ShapeDtypeStruct(st_shape, F32),
        ],
        scratch_shapes=[pltpu.VMEM(st_shape, F32)],
        compiler_params=_cparams(("arbitrary",)),
        name="gla",
    )(gla_in, gla_in, s0, mall, masks)


def _attn_kernel(*refs, n_seg):
    q_ref = refs[0]
    k_refs = refs[1:1 + n_seg]
    v_refs = refs[1 + n_seg:1 + 2 * n_seg]
    o_ref = refs[1 + 2 * n_seg]
    q = q_ref[0, 0]
    m = acc = den = None
    for k_ref, v_ref in zip(k_refs, v_refs):
        n_keys = k_ref.shape[2]
        tk = min(TK_ATTN, n_keys)
        for c in range(n_keys // tk):
            keys = slice(c * tk, (c + 1) * tk)
            s = _dot_nt(q, k_ref[0, 0, keys, :])
            m_c = jnp.max(s, axis=-1, keepdims=True)
            if m is None:
                m = m_c
                p = jnp.exp2(s - m)
                den = jnp.sum(p, axis=-1, keepdims=True)
                acc = _dot(p.astype(BF16), v_ref[0, 0, keys, :])
            else:
                m_new = jnp.maximum(m, m_c)
                alpha = jnp.exp2(m - m_new)
                p = jnp.exp2(s - m_new)
                den = alpha * den + jnp.sum(p, axis=-1, keepdims=True)
                acc = alpha * acc + _dot(p.astype(BF16), v_ref[0, 0, keys, :])
                m = m_new
    o_ref[0] = (acc / den).astype(BF16)


def _attention(q, ks, vs):
    n_batch, n_heads, nq, _ = q.shape
    tq = min(TQ_ATTN, nq)
    n_seg = len(ks)
    in_specs = [pl.BlockSpec((1, 1, tq, MLA_QK), lambda b, h, i: (b, h, i, 0))]
    for k in ks:
        in_specs.append(pl.BlockSpec((1, 1, k.shape[2], MLA_QK), lambda b, h, i: (b, h, 0, 0)))
    for v in vs:
        in_specs.append(pl.BlockSpec((1, 1, v.shape[2], MLA_V), lambda b, h, i: (b, h, 0, 0)))
    return pl.pallas_call(
        functools.partial(_attn_kernel, n_seg=n_seg),
        grid=(n_batch, n_heads, nq // tq),
        in_specs=in_specs,
        out_specs=pl.BlockSpec((1, tq, MLA_V), lambda b, h, i: (b, i, h)),
        out_shape=jax.ShapeDtypeStruct((n_batch, nq, n_heads * MLA_V), BF16),
        compiler_params=_cparams(("arbitrary", "arbitrary", "arbitrary")),
        name="attn",
    )(q, *ks, *vs)


def _outproj_kernel(a_ref, b_ref, of_ref, ob_ref, r_ref, x_ref, mod_ref, gon_ref, n2_ref, wout_ref,
                    wr_hi_ref, wr_lo_ref, br_ref, lst_ref, sl_ref,
                    x1_out, h2_out, meta_out, cnt_out):
    d = D_MODEL
    tm = x_ref.shape[0]
    gate1 = mod_ref[0, :, 2 * d:3 * d]
    shift2 = mod_ref[0, :, 3 * d:4 * d]
    scale2 = mod_ref[0, :, 4 * d:5 * d]

    og = of_ref[...] + ob_ref[...]
    gon = gon_ref[...]
    parts = []
    for hh in range(GLA_HEADS):
        seg = og[:, hh * GLA_DV:(hh + 1) * GLA_DV]
        parts.append(seg * lax.rsqrt(jnp.mean(seg * seg, axis=-1, keepdims=True) + EPS) * gon)
    cg = jnp.concatenate(parts, axis=-1) * _silu(r_ref[...])
    na = MLA_HEADS * MLA_V
    y = _dot(a_ref[...], wout_ref[0:na, :])
    y = y + _dot(b_ref[...], wout_ref[na:na + SGU_WIDTH, :])
    y = y + _dot(cg.astype(BF16), wout_ref[na + SGU_WIDTH:na + SGU_WIDTH + GLA_V_W, :])
    x1 = x_ref[...] + gate1 * y
    x1_out[...] = x1

    ms = jnp.mean(x1 * x1, axis=-1, keepdims=True)
    h2 = (x1 * lax.rsqrt(ms + EPS) * n2_ref[...]) * (1.0 + scale2) + shift2
    h_hi = h2.astype(BF16)
    h2_out[...] = h_hi
    h_lo = (h2 - h_hi.astype(F32)).astype(BF16)

    wr_hi = wr_hi_ref[...]
    lt = _dot_nt(wr_hi, h_hi) + _dot_nt(wr_hi, h_lo) + _dot_nt(wr_lo_ref[...], h_hi) + br_ref[:, 0:1]
    grp = [lt[MOE_EXPERTS + g:MOE_EXPERTS + g + 1, :] for g in range(MOE_GROUPS)]
    gmax = functools.reduce(jnp.maximum, grp)
    gsel = jnp.full(gmax.shape, MOE_GROUPS - 1, jnp.int32)
    for g in range(MOE_GROUPS - 2, -1, -1):
        gsel = jnp.where(grp[g] == gmax, g, gsel)
    gsum = functools.reduce(jnp.add, [jnp.exp(gl - gmax) for gl in grp])
    g_w = 1.0 / gsum
    e_in = []
    for j in range(MOE_PER_GROUP):
        ej = lt[j:j + 1, :]
        for g in range(1, MOE_GROUPS):
            ej = jnp.where(gsel == g, lt[g * MOE_PER_GROUP + j:g * MOE_PER_GROUP + j + 1, :], ej)
        e_in.append(ej)
    m1 = functools.reduce(jnp.maximum, e_in)
    i1 = jnp.full(m1.shape, MOE_PER_GROUP - 1, jnp.int32)
    for j in range(MOE_PER_GROUP - 2, -1, -1):
        i1 = jnp.where(e_in[j] == m1, j, i1)
    e_rest = [jnp.where(i1 == j, -jnp.inf, e_in[j]) for j in range(MOE_PER_GROUP)]
    m2 = functools.reduce(jnp.maximum, e_rest)
    i2 = jnp.full(m2.shape, MOE_PER_GROUP - 1, jnp.int32)
    for j in range(MOE_PER_GROUP - 2, -1, -1):
        i2 = jnp.where(e_rest[j] == m2, j, i2)
    p2 = jnp.exp(m2 - m1)
    w1 = g_w / (1.0 + p2)
    w2 = g_w * p2 / (1.0 + p2)
    eid1 = gsel * MOE_PER_GROUP + i1
    eid2 = gsel * MOE_PER_GROUP + i2

    e_iota = lax.broadcasted_iota(jnp.int32, (MOE_EXPERTS, tm), 0)
    hit1 = e_iota == eid1
    hit2 = e_iota == eid2
    onehot = jnp.where(hit1, 1.0, 0.0) + jnp.where(hit2, 1.0, 0.0)
    rank = _dot(onehot.astype(BF16), lst_ref[...])
    counts = jnp.sum(onehot, axis=1, keepdims=True)
    padded = jnp.ceil(counts * (1.0 / BF16_ROWS)) * BF16_ROWS
    padded_l = jnp.broadcast_to(padded, (MOE_EXPERTS, LANES))
    off = _dot(sl_ref[...], padded_l.astype(BF16))
    pos_e = off[:, 0:1] + rank
    pos1 = jnp.sum(jnp.where(hit1, pos_e, 0.0), axis=0, keepdims=True)
    pos2 = jnp.sum(jnp.where(hit2, pos_e, 0.0), axis=0, keepdims=True)
    zero = jnp.zeros_like(pos1)
    meta_out[0] = jnp.concatenate(
        [pos1, pos2, w1, w2, eid1.astype(F32), eid2.astype(F32), zero, zero], axis=0)
    cnt_out[0] = padded_l


def _outproj(a, b, o_f, o_b, r, x2, n_batch, mod_rows, mod_row0, lw, rconst):
    n_total = x2.shape[0]
    n_tok = n_total // n_batch
    tm = T_LOC
    tpb = n_tok // tm
    if mod_row0 == 0:
        mod_map = lambda i: (i // tpb, 0, 0)
    else:
        mod_map = lambda i: (mod_row0, 0, 0)
    const2 = lambda i: (0, 0)
    tok_map = lambda i: (i, 0)
    n_tiles = n_total // tm
    in_specs = [
        pl.BlockSpec((tm, MLA_HEADS * MLA_V), tok_map),
        pl.BlockSpec((tm, SGU_WIDTH), tok_map),
        pl.BlockSpec((tm, GLA_V_W), tok_map),
        pl.BlockSpec((tm, GLA_V_W), tok_map),
        pl.BlockSpec((tm, GLA_V_W), tok_map),
        pl.BlockSpec((tm, D_MODEL), tok_map),
        pl.BlockSpec((1, 1, 6 * D_MODEL), mod_map),
        pl.BlockSpec((1, GLA_DV), const2),
        pl.BlockSpec((1, D_MODEL), const2),
        pl.BlockSpec((D_MODEL, D_MODEL), const2, pipeline_mode=pl.Buffered(1)),
        pl.BlockSpec((32, D_MODEL), const2),
        pl.BlockSpec((32, D_MODEL), const2),
        pl.BlockSpec((32, LANES), const2),
        pl.BlockSpec((tm, tm), const2),
        pl.BlockSpec((MOE_EXPERTS, MOE_EXPERTS), const2),
    ]
    out_shape = [
        jax.ShapeDtypeStruct((n_total, D_MODEL), F32),
        jax.ShapeDtypeStruct((n_total, D_MODEL), BF16),
        jax.ShapeDtypeStruct((n_tiles, 8, tm), F32),
        jax.ShapeDtypeStruct((n_tiles, MOE_EXPERTS, LANES), F32),
    ]
    out_specs = [
        pl.BlockSpec((tm, D_MODEL), tok_map),
        pl.BlockSpec((tm, D_MODEL), tok_map),
        pl.BlockSpec((1, 8, tm), lambda i: (i, 0, 0)),
        pl.BlockSpec((1, MOE_EXPERTS, LANES), lambda i: (i, 0, 0)),
    ]
    return pl.pallas_call(
        _outproj_kernel,
        grid=(n_tiles,),
        in_specs=in_specs,
        out_specs=out_specs,
        out_shape=out_shape,
        compiler_params=_cparams(("arbitrary",)),
        name="outproj",
    )(a, b, o_f, o_b, r, x2, mod_rows, lw["gla_out_norm"], lw["n2"], lw["w_out"],
      lw["wr_hi"], lw["wr_lo"], lw["br"], rconst["lst"], rconst["sl"])


def _piece_copies(src_ref, dst_ref, sem, src_row, dst_row, n_rows):
    copies = []
    for size in _PIECE_SIZES:
        done = n_rows & ~(2 * size - 1)
        pred = (n_rows & size) != 0
        s0 = pl.multiple_of(src_row + done, BF16_ROWS)
        d0 = pl.multiple_of(dst_row + done, BF16_ROWS)
        cp = pltpu.make_async_copy(src_ref.at[pl.ds(s0, size)], dst_ref.at[pl.ds(d0, size)], sem)
        copies.append((pred, cp))
    return copies


def _dispatch_kernel(loc_tab, dst_tab, cnt_tab, ztab_dst, ztab_cnt, n_used, h2_ref, meta_ref, xs_ref,
                     xloc_ref, zeros_ref, sem):
    j = pl.program_id(0)
    tm = h2_ref.shape[0]

    @pl.when(j == 0)
    def _():
        zeros_ref[...] = jnp.zeros_like(zeros_ref)

        def fill(t, carry):
            cp = pltpu.make_async_copy(zeros_ref, xs_ref.at[pl.ds(pl.multiple_of(t * TM_FFN, TM_FFN), TM_FFN)], sem)
            cp.start()
            cp.wait()
            return carry
        lax.fori_loop(n_used[0], xs_ref.shape[0] // TM_FFN, fill, 0)
        zcopies = []
        for e in range(MOE_EXPERTS):
            zcopies += _piece_copies(zeros_ref, xs_ref, sem, 0, ztab_dst[e], ztab_cnt[e])
        for pred, cp in zcopies:
            @pl.when(pred)
            def _(cp=cp):
                cp.start()
        for pred, cp in zcopies:
            @pl.when(pred)
            def _(cp=cp):
                cp.wait()

    pos1 = meta_ref[0, 0:1, :]
    pos2 = meta_ref[0, 1:2, :]
    s_iota = lax.broadcasted_iota(jnp.int32, (SLOTS, tm), 0).astype(F32)
    perm = jnp.where((s_iota == pos1) | (s_iota == pos2), 1.0, 0.0).astype(BF16)
    xloc_ref[...] = _dot(perm, h2_ref[...]).astype(BF16)

    copies = []
    for e in range(MOE_EXPERTS):
        idx = j * MOE_EXPERTS + e
        copies += _piece_copies(xloc_ref, xs_ref, sem, loc_tab[idx], dst_tab[idx], cnt_tab[idx])
    for pred, cp in copies:
        @pl.when(pred)
        def _(cp=cp):
            cp.start()
    for pred, cp in copies:
        @pl.when(pred)
        def _(cp=cp):
            cp.wait()


def _dispatch(h2, meta, tabs, r_max):
    n_tiles = meta.shape[0]
    tm = T_LOC
    grid_spec = pltpu.PrefetchScalarGridSpec(
        num_scalar_prefetch=6,
        grid=(n_tiles,),
        in_specs=[pl.BlockSpec((tm, D_MODEL), lambda i, *_: (i, 0)),
                  pl.BlockSpec((1, 8, tm), lambda i, *_: (i, 0, 0))],
        out_specs=pl.BlockSpec(memory_space=pl.ANY),
        scratch_shapes=[pltpu.VMEM((SLOTS, D_MODEL), BF16), pltpu.VMEM((TM_FFN, D_MODEL), BF16),
                        pltpu.SemaphoreType.DMA(())],
    )
    return pl.pallas_call(
        _dispatch_kernel,
        grid_spec=grid_spec,
        out_shape=jax.ShapeDtypeStruct((r_max, D_MODEL), BF16),
        compiler_params=_cparams(("arbitrary",)),
        name="dispatch",
    )(tabs["loc"], tabs["dst"], tabs["cnt"], tabs["zdst"], tabs["zcnt"], tabs["n_used"], h2, meta)


def _ffn_kernel(tile_e, tile_blk, n_used, x_ref, w1_ref, w3_ref, w2_ref, y_ref):
    i = pl.program_id(0)

    @pl.when(i < n_used[0])
    def _():
        x = x_ref[...]
        acc = None
        for f in range(D_EXPERT // F_CHUNK):
            cols = slice(f * F_CHUNK, (f + 1) * F_CHUNK)
            h1 = _dot(x, w1_ref[0, 0, :, cols])
            h3 = _dot(x, w3_ref[0, 0, :, cols])
            act = (_silu(h1) * h3).astype(BF16)
            part = _dot(act, w2_ref[0, 0, cols, :])
            acc = part if acc is None else acc + part
        y_ref[...] = acc.astype(BF16)

    @pl.when(i >= n_used[0])
    def _():
        y_ref[...] = jnp.zeros_like(y_ref)


def _ffn(xs, layer, w1, w3, w2, tabs):
    r_max = xs.shape[0]
    n_mt = r_max // TM_FFN
    grid_spec = pltpu.PrefetchScalarGridSpec(
        num_scalar_prefetch=3,
        grid=(n_mt,),
        in_specs=[
            pl.BlockSpec((TM_FFN, D_MODEL), lambda i, te, tb, nu: (tb[i], 0)),
            pl.BlockSpec((1, 1, D_MODEL, D_EXPERT), lambda i, te, tb, nu: (layer, te[i], 0, 0)),
            pl.BlockSpec((1, 1, D_MODEL, D_EXPERT), lambda i, te, tb, nu: (layer, te[i], 0, 0)),
            pl.BlockSpec((1, 1, D_EXPERT, D_MODEL), lambda i, te, tb, nu: (layer, te[i], 0, 0)),
        ],
        out_specs=pl.BlockSpec((TM_FFN, D_MODEL), lambda i, te, tb, nu: (i, 0)),
    )
    return pl.pallas_call(
        _ffn_kernel,
        grid_spec=grid_spec,
        out_shape=jax.ShapeDtypeStruct((r_max, D_MODEL), BF16),
        compiler_params=_cparams(("arbitrary",)),
        name="ffn",
    )(tabs["tile_e"], tabs["tile_blk"], tabs["n_used"], xs, w1, w3, w2)


def _combine_kernel(loc_tab, dst_tab, cnt_tab, y_ref, metac_ref, x1_ref, mod_ref, out_ref, yloc_ref, sem):
    j = pl.program_id(0)
    d = D_MODEL
    tm = x1_ref.shape[0]

    @pl.when(j == 0)
    def _():
        yloc_ref[...] = jnp.zeros_like(yloc_ref)

    copies = []
    for e in range(MOE_EXPERTS):
        idx = j * MOE_EXPERTS + e
        copies += _piece_copies(y_ref, yloc_ref, sem, dst_tab[idx], loc_tab[idx], cnt_tab[idx])
    for pred, cp in copies:
        @pl.when(pred)
        def _(cp=cp):
            cp.start()
    for pred, cp in copies:
        @pl.when(pred)
        def _(cp=cp):
            cp.wait()

    mc = metac_ref[0]
    s_iota = lax.broadcasted_iota(jnp.int32, (tm, SLOTS), 1).astype(F32)
    wc = jnp.where(s_iota == mc[:, 0:1], mc[:, 2:3], 0.0) + jnp.where(s_iota == mc[:, 1:2], mc[:, 3:4], 0.0)
    moe = _dot(wc.astype(BF16), yloc_ref[...])
    gate2 = mod_ref[0, :, 5 * d:6 * d]
    out_ref[...] = x1_ref[...] + gate2 * moe


def _combine(y, metac, x1, n_batch, mod_rows, mod_row0, tabs):
    n_total = x1.shape[0]
    n_tok = n_total // n_batch
    tm = T_LOC
    tpb = n_tok // tm
    n_tiles = n_total // tm
    if mod_row0 == 0:
        mod_map = lambda i, *_: (i // tpb, 0, 0)
    else:
        mod_map = lambda i, *_: (mod_row0, 0, 0)
    grid_spec = pltpu.PrefetchScalarGridSpec(
        num_scalar_prefetch=3,
        grid=(n_tiles,),
        in_specs=[
            pl.BlockSpec(memory_space=pl.ANY),
            pl.BlockSpec((1, tm, 8), lambda i, *_: (i, 0, 0)),
            pl.BlockSpec((tm, D_MODEL), lambda i, *_: (i, 0)),
            pl.BlockSpec((1, 1, 6 * D_MODEL), mod_map),
        ],
        out_specs=pl.BlockSpec((tm, D_MODEL), lambda i, *_: (i, 0)),
        scratch_shapes=[pltpu.VMEM((SLOTS, D_MODEL), BF16), pltpu.SemaphoreType.DMA(())],
    )
    return pl.pallas_call(
        _combine_kernel,
        grid_spec=grid_spec,
        out_shape=jax.ShapeDtypeStruct((n_total, D_MODEL), F32),
        compiler_params=_cparams(("arbitrary",)),
        name="combine",
    )(tabs["loc"], tabs["dst"], tabs["cnt"], y, metac, x1, mod_rows)


def _rope_tables(n_tokens):
    rows = n_tokens // GRID_W
    row = jnp.repeat(jnp.arange(rows, dtype=jnp.int32), GRID_W)
    col = jnp.tile(jnp.arange(GRID_W, dtype=jnp.int32), rows)
    n_freq = MLA_ROPE // 4
    inv_freq = ROPE_THETA ** (-jnp.arange(n_freq, dtype=F32) / n_freq)
    ang_r = row[:, None].astype(F32) * inv_freq
    ang_c = col[:, None].astype(F32) * inv_freq
    cos = jnp.concatenate([jnp.cos(ang_r)] * 2 + [jnp.cos(ang_c)] * 2, axis=-1)
    sin = jnp.concatenate([-jnp.sin(ang_r), jnp.sin(ang_r), -jnp.sin(ang_c), jnp.sin(ang_c)], axis=-1)
    return cos, sin


_SWAP64 = np.concatenate([np.arange(16, 32), np.arange(0, 16), np.arange(48, 64), np.arange(32, 48)])


def _position_tables(cos, sin, q_gain, k_gain):
    gq = q_gain[MLA_NOPE:]
    gk = k_gain[MLA_NOPE:]
    cq = jnp.tile(cos * gq, (1, MLA_HEADS))
    sq = jnp.tile(sin * gq[_SWAP64], (1, MLA_HEADS))
    tk = jnp.concatenate([cos * gk, sin * gk[_SWAP64]], axis=-1)
    return {"cq": cq, "sq": sq, "tk": tk}


def _layer_weights(l, p):
    d = D_MODEL
    o = _IN_OFF
    w_in = p["w_in"][l]
    seg = lambda i: w_in[:, o[i]:o[i + 1]]
    kpe = seg(2)
    w_in_r = jnp.concatenate(
        [seg(0), seg(1), seg(3), seg(4), seg(5) * (GLA_DK ** -0.5), seg(6), seg(7), seg(8),
         kpe, kpe[:, _SWAP64], seg(9), seg(10), jnp.zeros((d, Z_WIDTH - Z_GG - 2 * GLA_GATE_RANK), F32)],
        axis=-1).astype(BF16)
    w_uq = p["mla_w_uq"][l].reshape(MLA_Q_RANK, MLA_HEADS, MLA_QK)
    q_nope = w_uq[:, :, :MLA_NOPE].reshape(MLA_Q_RANK, -1)
    q_pe = w_uq[:, :, MLA_NOPE:]
    w_uq_r = jnp.concatenate([q_nope, q_pe.reshape(MLA_Q_RANK, -1), q_pe[:, :, _SWAP64].reshape(MLA_Q_RANK, -1)],
                             axis=-1).astype(BF16)
    w_ukv = p["mla_w_ukv"][l].reshape(MLA_KV_RANK, MLA_HEADS, MLA_NOPE + MLA_V)
    w_ukv_r = jnp.concatenate([w_ukv[:, :, :MLA_NOPE].reshape(MLA_KV_RANK, -1),
                               w_ukv[:, :, MLA_NOPE:].reshape(MLA_KV_RANK, -1)], axis=-1).astype(BF16)
    wg = jnp.zeros((LANES, 2 * GLA_QK_W), F32)
    wg = wg.at[0:GLA_GATE_RANK, 0:GLA_QK_W].set(p["gla_wg_f"][l])
    wg = wg.at[GLA_GATE_RANK:2 * GLA_GATE_RANK, GLA_QK_W:].set(p["gla_wg_b"][l])
    bg = jnp.concatenate([p["gla_bg_f"][l], p["gla_bg_b"][l]])[None, :]
    sgu_b = jnp.repeat(p["sgu_b"][l].T, SGU_GROUP_DIM, axis=1)
    wr = jnp.zeros((32, d), F32)
    wr = wr.at[0:MOE_EXPERTS].set(p["moe_w_expert"][l].T)
    wr = wr.at[MOE_EXPERTS:MOE_EXPERTS + MOE_GROUPS].set(p["moe_w_group"][l].T)
    wr_hi = wr.astype(BF16)
    wr_lo = (wr - wr_hi.astype(F32)).astype(BF16)
    br = jnp.zeros((32,), F32).at[0:MOE_EXPERTS].set(p["moe_b_expert"][l])
    br = br.at[MOE_EXPERTS:MOE_EXPERTS + MOE_GROUPS].set(p["moe_b_group"][l])
    return {
        "n1": p["norm1_g"][l][None, :], "n2": p["norm2_g"][l][None, :],
        "w_in": w_in_r, "q_norm": p["mla_q_norm"][l][None, :], "kv_norm": p["mla_kv_norm"][l][None, :],
        "w_uq": w_uq_r, "w_ukv": w_ukv_r,
        "gq_nope": p["mla_q_gain"][l][None, :MLA_NOPE], "gk_nope": p["mla_k_gain"][l][None, :MLA_NOPE],
        "sgu_norm": p["sgu_norm"][l].reshape(1, SGU_WIDTH), "sgu_w": p["sgu_w"][l].astype(BF16), "sgu_b": sgu_b,
        "wg": wg.astype(BF16), "bg": bg,
        "gla_out_norm": p["gla_out_norm"][l][None, :], "w_out": p["w_out"][l].astype(BF16),
        "wr_hi": wr_hi, "wr_lo": wr_lo, "br": jnp.broadcast_to(br[:, None], (32, LANES)),
        "layer": l, "w1": p["moe_w1_bf16"], "w3": p["moe_w3_bf16"], "w2": p["moe_w2_bf16"],
    }


def _router_constants():
    t = np.arange(T_LOC)
    lst = (t[:, None] < t[None, :]).astype(np.float32)
    e = np.arange(MOE_EXPERTS)
    sl = (e[None, :] < e[:, None]).astype(np.float32)
    return {"lst": jnp.asarray(lst, BF16), "sl": jnp.asarray(sl, BF16)}


def _dispatch_tables(cnt, r_max):
    n_tiles = cnt.shape[0]
    cnt = cnt.astype(jnp.int32)
    loc_off = jnp.cumsum(cnt, axis=1) - cnt
    rows_e = jnp.sum(cnt, axis=0)
    tiles_e = (rows_e + TM_FFN - 1) // TM_FFN
    tile_start = jnp.cumsum(tiles_e) - tiles_e
    base_e = tile_start * TM_FFN
    dst = base_e[None, :] + jnp.cumsum(cnt, axis=0) - cnt
    n_used = jnp.sum(tiles_e)
    n_mt = r_max // TM_FFN
    ti = jnp.arange(n_mt, dtype=jnp.int32)
    tile_blk = jnp.minimum(ti, n_used - 1)
    tile_e = jnp.sum(tile_blk[:, None] >= (tile_start + tiles_e)[None, :], axis=1).astype(jnp.int32)
    tile_e = jnp.minimum(tile_e, MOE_EXPERTS - 1)
    zdst = base_e + rows_e
    zcnt = tiles_e * TM_FFN - rows_e
    i32 = lambda a: a.astype(jnp.int32)
    return {"loc": i32(loc_off.reshape(-1)), "dst": i32(dst.reshape(-1)),
            "cnt": i32(cnt.reshape(-1)), "zdst": i32(zdst), "zcnt": i32(zcnt),
            "tile_e": tile_e, "tile_blk": i32(tile_blk), "n_used": i32(n_used.reshape(1))}


def _moe(h2, meta, cnt, x1, n_batch, mod_rows, mod_row0, lw):
    n_tiles = meta.shape[0]
    worst_rows = n_tiles * SLOTS + MOE_EXPERTS * (TM_FFN - BF16_ROWS)
    r_max = -(-worst_rows // TM_FFN) * TM_FFN
    tabs = _dispatch_tables(cnt[:, :, 0], r_max)
    xs = _dispatch(h2, meta, tabs, r_max)
    y = _ffn(xs, lw["layer"], lw["w1"], lw["w3"], lw["w2"], tabs)
    metac = jnp.swapaxes(meta, 1, 2)
    return _combine(y, metac, x1, n_batch, mod_rows, mod_row0, tabs)


def kernel(x, c, ctx, c_ctx, norm1_g, norm2_g, ada_w, ada_b, w_in, mla_q_norm, mla_w_uq, mla_kv_norm, mla_w_ukv,
           mla_q_gain, mla_k_gain, sgu_norm, sgu_w, sgu_b, gla_wg_f, gla_bg_f, gla_wg_b, gla_bg_b, gla_out_norm,
           w_out, moe_w_group, moe_b_group, moe_w_expert, moe_b_expert, moe_w1, moe_w3, moe_w2):
    p = dict(norm1_g=norm1_g, norm2_g=norm2_g, w_in=w_in, mla_q_norm=mla_q_norm, mla_w_uq=mla_w_uq,
             mla_kv_norm=mla_kv_norm, mla_w_ukv=mla_w_ukv, mla_q_gain=mla_q_gain, mla_k_gain=mla_k_gain,
             sgu_norm=sgu_norm, sgu_w=sgu_w, sgu_b=sgu_b, gla_wg_f=gla_wg_f, gla_bg_f=gla_bg_f,
             gla_wg_b=gla_wg_b, gla_bg_b=gla_bg_b, gla_out_norm=gla_out_norm, w_out=w_out,
             moe_w_group=moe_w_group, moe_b_group=moe_b_group, moe_w_expert=moe_w_expert,
             moe_b_expert=moe_b_expert, moe_w1_bf16=moe_w1.astype(BF16), moe_w3_bf16=moe_w3.astype(BF16),
             moe_w2_bf16=moe_w2.astype(BF16))
    n_batch, seq, d = x.shape
    n_ctx = ctx.shape[1]
    depth = ada_w.shape[0]
    assert d == D_MODEL and n_batch <= 4 and seq % T_LOC == 0 and n_ctx % TM_IN == 0
    assert (n_batch * n_ctx) % T_LOC == 0

    c_rows = jnp.zeros((8, d), F32).at[0:n_batch].set(c).at[4].set(c_ctx)
    mod = _ada(c_rows, ada_w, ada_b)

    cos, sin = _rope_tables(seq)
    ones_c = jnp.ones((TM_IN, MLA_ROPE), F32)
    gla_consts_np = _gla_constants()
    gla_consts = (jnp.asarray(gla_consts_np[0], BF16), jnp.asarray(gla_consts_np[1], F32))
    rconst = _router_constants()
    zero_state = jnp.zeros((2, n_batch, GLA_DV, GLA_QK_W), F32)

    x2 = x.reshape(n_batch * seq, d)
    ctx2 = ctx.reshape(n_batch * n_ctx, d)
    for l in range(depth):
        ctx_out = l < depth - 1
        lw = _layer_weights(l, p)
        mod_rows = mod[l].reshape(8, 1, 6 * d)
        tabs_x = _position_tables(cos, sin, mla_q_gain[l], mla_k_gain[l])
        tabs_c = _position_tables(ones_c, 0.0 * ones_c, mla_q_gain[l], mla_k_gain[l])

        q_x, k_x, v_x, b_x, gla_x, r_x = _inproj(x2, n_batch, mod_rows, 0, lw, tabs_x)
        q_c, k_c, v_c, b_c, gla_c, r_c = _inproj(ctx2, n_batch, mod_rows, 4, lw, tabs_c)

        of_c, ob_c, s_ctx = _gla(gla_c.reshape(n_batch, n_ctx, -1), zero_state, gla_consts)
        of_x, ob_x, _ = _gla(gla_x.reshape(n_batch, seq, -1), s_ctx, gla_consts)
        a_x = _attention(q_x, [k_x, k_c], [v_x, v_c])

        x1, h2, meta, cnt = _outproj(a_x.reshape(n_batch * seq, -1), b_x, of_x.reshape(n_batch * seq, -1),
                                     ob_x.reshape(n_batch * seq, -1), r_x, x2, n_batch, mod_rows, 0, lw, rconst)
        x2 = _moe(h2, meta, cnt, x1, n_batch, mod_rows, 0, lw)

        if ctx_out:
            a_c = _attention(q_c, [k_c], [v_c])
            c1, h2c, meta_c, cnt_c = _outproj(a_c.reshape(n_batch * n_ctx, -1), b_c,
                                              of_c.reshape(n_batch * n_ctx, -1), ob_c.reshape(n_batch * n_ctx, -1),
                                              r_c, ctx2, n_batch, mod_rows, 4, lw, rconst)
            ctx2 = _moe(h2c, meta_c, cnt_c, c1, n_batch, mod_rows, 4, lw)
    return x2.reshape(n_batch, seq, d)
```

```python
import functools

import numpy as np
import jax
import jax.numpy as jnp
from jax import lax
from jax.experimental import pallas as pl
from jax.experimental.pallas import tpu as pltpu

F32 = jnp.float32
BF16 = jnp.bfloat16

D_MODEL = 2048
GRID_W = 64
EPS = 1e-6

MLA_HEADS = 8
MLA_Q_RANK = 512
MLA_KV_RANK = 512
MLA_NOPE = 128
MLA_ROPE = 64
MLA_V = 128
MLA_QK = MLA_NOPE + MLA_ROPE
MLA_SCALE = MLA_QK ** -0.5
ROPE_THETA = 10000.0
LOG2_E = float(np.log2(np.e))

SGU_WIDTH = 512
SGU_GROUPS = 4
SGU_GROUP_DIM = SGU_WIDTH // SGU_GROUPS
SGU_CHUNK = 128

GLA_HEADS = 4
GLA_DK = 64
GLA_DV = 128
GLA_GATE_RANK = 16
GLA_GATE_NORM = 16.0
GLA_CHUNK = 64
GLA_QK_W = GLA_HEADS * GLA_DK
GLA_V_W = GLA_HEADS * GLA_DV

MOE_GROUPS = 4
MOE_PER_GROUP = 4
MOE_EXPERTS = MOE_GROUPS * MOE_PER_GROUP
D_EXPERT = 1024

_IN_SPLITS = (512, 512, 64, 512, 512, 256, 256, 512, 512, 16, 16)
_IN_OFF = np.concatenate([[0], np.cumsum(_IN_SPLITS)]).tolist()

Z_CQ, Z_CKV, Z_ZU, Z_ZV = 0, 512, 1024, 1536
Z_GQ, Z_GK, Z_GV, Z_GR = 2048, 2304, 2560, 3072
Z_KPE, Z_GG = 3584, 3712
Z_WIDTH = 3840

LANES = 128
BF16_ROWS = 16
VMEM_LIMIT = 56 * 1024 * 1024

TM_IN = 256
TQ_ATTN = 1024
TK_ATTN = 2048
T_LOC = 512
SLOTS = 2 * T_LOC + MOE_EXPERTS * BF16_ROWS
TM_FFN = 512
F_CHUNK = 256
_PIECE_SIZES = (512, 256, 128, 64, 32, 16)


def _cparams(sem, vmem=VMEM_LIMIT):
    return pltpu.CompilerParams(dimension_semantics=sem, vmem_limit_bytes=vmem)


def _silu(x):
    return x / (1.0 + jnp.exp(-x))


def _gelu_tanh(x):
    return 0.5 * x * (1.0 + jnp.tanh(np.sqrt(2.0 / np.pi).astype(np.float32) * (x + 0.044715 * (x * x * x))))


def _log_sigmoid(x):
    return jnp.minimum(x, 0.0) - jnp.log(1.0 + jnp.exp(-jnp.abs(x)))


def _dot(a, b):
    return jnp.dot(a, b, preferred_element_type=F32)


def _dot_nt(a, b):
    return lax.dot_general(a, b, (((1,), (1,)), ((), ())), preferred_element_type=F32)


def _dot_tn(a, b):
    return lax.dot_general(a, b, (((0,), (0,)), ((), ())), preferred_element_type=F32)


def _ada_kernel(c_ref, w_ref, b_ref, o_ref):
    a = _silu(c_ref[...]).astype(BF16)
    o_ref[0] = _dot(a, w_ref[0].astype(BF16)) + b_ref[0]


def _ada(c_rows, ada_w, ada_b):
    depth, d, n = ada_w.shape
    tn = 1536
    return pl.pallas_call(
        _ada_kernel,
        grid=(depth, n // tn),
        in_specs=[
            pl.BlockSpec((8, d), lambda l, j: (0, 0)),
            pl.BlockSpec((1, d, tn), lambda l, j: (l, 0, j)),
            pl.BlockSpec((1, 1, tn), lambda l, j: (l, 0, j)),
        ],
        out_specs=pl.BlockSpec((1, 8, tn), lambda l, j: (l, 0, j)),
        out_shape=jax.ShapeDtypeStruct((depth, 8, n), F32),
        compiler_params=_cparams(("arbitrary", "arbitrary")),
        name="ada",
    )(c_rows, ada_w, ada_b.reshape(depth, 1, n))


def _inproj_kernel(x_ref, mod_ref, n1_ref, win_ref, qn_ref, wuq_ref, kvn_ref, wukv_ref,
                   gqn_ref, gkn_ref, cq_ref, sq_ref, tk_ref,
                   sgn_ref, sgw_ref, sgb_ref, wg_ref, bg_ref,
                   q_out, k_out, v_out, b_out, gla_out, r_out):
    d = D_MODEL
    tm = x_ref.shape[0]
    x = x_ref[...]
    shift = mod_ref[0, :, 0:d]
    scale = mod_ref[0, :, d:2 * d]
    ms = jnp.mean(x * x, axis=-1, keepdims=True)
    h = (x * lax.rsqrt(ms + EPS) * n1_ref[...]) * (1.0 + scale) + shift
    z = _dot(h.astype(BF16), win_ref[...])

    cq = z[:, Z_CQ:Z_CQ + MLA_Q_RANK]
    ckv = z[:, Z_CKV:Z_CKV + MLA_KV_RANK]
    cqn = cq * lax.rsqrt(jnp.mean(cq * cq, axis=-1, keepdims=True) + EPS) * qn_ref[...]
    ckvn = ckv * lax.rsqrt(jnp.mean(ckv * ckv, axis=-1, keepdims=True) + EPS) * kvn_ref[...]
    qa = _dot(cqn.astype(BF16), wuq_ref[...])
    kv = _dot(ckvn.astype(BF16), wukv_ref[...])
    npe = MLA_HEADS * MLA_NOPE
    rw = MLA_HEADS * MLA_ROPE
    q_pe = qa[:, npe:npe + rw]
    q_rot = q_pe * cq_ref[...] + qa[:, npe + rw:npe + 2 * rw] * sq_ref[...]
    zk = z[:, Z_KPE:Z_KPE + 2 * MLA_ROPE]
    tkz = zk * tk_ref[...]
    k_rot = tkz[:, 0:MLA_ROPE] + tkz[:, MLA_ROPE:2 * MLA_ROPE]
    k_pe = zk[:, 0:MLA_ROPE]
    ss_kpe = jnp.sum(k_pe * k_pe, axis=-1, keepdims=True)
    gqn = gqn_ref[...]
    gkn = gkn_ref[...]
    for hh in range(MLA_HEADS):
        qn_h = qa[:, hh * MLA_NOPE:(hh + 1) * MLA_NOPE]
        qp_h = q_pe[:, hh * MLA_ROPE:(hh + 1) * MLA_ROPE]
        ssq = jnp.sum(qn_h * qn_h, axis=-1, keepdims=True) + jnp.sum(qp_h * qp_h, axis=-1, keepdims=True)
        rq = lax.rsqrt(ssq * (1.0 / MLA_QK) + EPS) * (MLA_SCALE * LOG2_E)
        q_h = jnp.concatenate([qn_h * gqn * rq, q_rot[:, hh * MLA_ROPE:(hh + 1) * MLA_ROPE] * rq], axis=-1)
        q_out[0, hh] = q_h.astype(BF16)
        kn_h = kv[:, hh * MLA_NOPE:(hh + 1) * MLA_NOPE]
        ssk = jnp.sum(kn_h * kn_h, axis=-1, keepdims=True) + ss_kpe
        rk = lax.rsqrt(ssk * (1.0 / MLA_QK) + EPS)
        k_h = jnp.concatenate([kn_h * gkn * rk, k_rot * rk], axis=-1)
        k_out[0, hh] = k_h.astype(BF16)
        v_out[0, hh] = kv[:, npe + hh * MLA_V:npe + (hh + 1) * MLA_V].astype(BF16)

    u = _gelu_tanh(z[:, Z_ZU:Z_ZU + SGU_WIDTH])
    vv = _gelu_tanh(z[:, Z_ZV:Z_ZV + SGU_WIDTH])
    sgn = sgn_ref[...]
    for c in range(tm // SGU_CHUNK):
        rows = slice(c * SGU_CHUNK, (c + 1) * SGU_CHUNK)
        outs = []
        for g in range(SGU_GROUPS):
            cols = slice(g * SGU_GROUP_DIM, (g + 1) * SGU_GROUP_DIM)
            seg = vv[rows, cols]
            vn = seg * lax.rsqrt(jnp.mean(seg * seg, axis=-1, keepdims=True) + EPS) * sgn[:, cols]
            outs.append(_dot(sgw_ref[g], vn.astype(BF16)))
        sp = jnp.concatenate(outs, axis=-1) + sgb_ref[...]
        b_out[rows, :] = (u[rows, :] * sp).astype(BF16)

    gl = _dot(z[:, Z_GG:Z_GG + LANES].astype(BF16), wg_ref[...]) + bg_ref[...]
    gdec = _log_sigmoid(gl) * (1.0 / GLA_GATE_NORM)
    gla_out[:, 0:2 * GLA_QK_W] = z[:, Z_GQ:Z_GQ + 2 * GLA_QK_W]
    gla_out[:, 2 * GLA_QK_W:4 * GLA_QK_W] = gdec
    gla_out[:, 4 * GLA_QK_W:4 * GLA_QK_W + GLA_V_W] = z[:, Z_GV:Z_GV + GLA_V_W]
    r_out[...] = z[:, Z_GR:Z_GR + GLA_V_W]


def _inproj(x2, n_batch, mod_rows, mod_row0, lw, tabs):
    n_total = x2.shape[0]
    n_tok = n_total // n_batch
    tm = TM_IN
    tpb = n_tok // tm
    tab_blocks = tabs["cq"].shape[0] // tm
    if mod_row0 == 0:
        mod_map = lambda i: (i // tpb, 0, 0)
    else:
        mod_map = lambda i: (mod_row0, 0, 0)
    const2 = lambda i: (0, 0)
    tab_map = lambda i: (i % tab_blocks, 0)
    tok_map = lambda i: (i, 0)
    head_map = lambda i: (i // tpb, 0, i % tpb, 0)
    in_specs = [
        pl.BlockSpec((tm, D_MODEL), tok_map),
        pl.BlockSpec((1, 1, 6 * D_MODEL), mod_map),
        pl.BlockSpec((1, D_MODEL), const2),
        pl.BlockSpec((D_MODEL, Z_WIDTH), const2, pipeline_mode=pl.Buffered(1)),
        pl.BlockSpec((1, MLA_Q_RANK), const2),
        pl.BlockSpec((MLA_Q_RANK, 2048), const2, pipeline_mode=pl.Buffered(1)),
        pl.BlockSpec((1, MLA_KV_RANK), const2),
        pl.BlockSpec((MLA_KV_RANK, 2048), const2, pipeline_mode=pl.Buffered(1)),
        pl.BlockSpec((1, MLA_NOPE), const2),
        pl.BlockSpec((1, MLA_NOPE), const2),
        pl.BlockSpec((tm, MLA_HEADS * MLA_ROPE), tab_map),
        pl.BlockSpec((tm, MLA_HEADS * MLA_ROPE), tab_map),
        pl.BlockSpec((tm, 2 * MLA_ROPE), tab_map),
        pl.BlockSpec((1, SGU_WIDTH), const2),
        pl.BlockSpec((SGU_GROUPS, SGU_CHUNK, SGU_CHUNK), lambda i: (0, 0, 0)),
        pl.BlockSpec((SGU_CHUNK, SGU_WIDTH), const2),
        pl.BlockSpec((LANES, 2 * GLA_QK_W), const2),
        pl.BlockSpec((1, 2 * GLA_QK_W), const2),
    ]
    out_shape = [
        jax.ShapeDtypeStruct((n_batch, MLA_HEADS, n_tok, MLA_QK), BF16),
        jax.ShapeDtypeStruct((n_batch, MLA_HEADS, n_tok, MLA_QK), BF16),
        jax.ShapeDtypeStruct((n_batch, MLA_HEADS, n_tok, MLA_V), BF16),
        jax.ShapeDtypeStruct((n_total, SGU_WIDTH), BF16),
        jax.ShapeDtypeStruct((n_total, 4 * GLA_QK_W + GLA_V_W), F32),
        jax.ShapeDtypeStruct((n_total, GLA_V_W), F32),
    ]
    out_specs = [
        pl.BlockSpec((1, MLA_HEADS, tm, MLA_QK), head_map),
        pl.BlockSpec((1, MLA_HEADS, tm, MLA_QK), head_map),
        pl.BlockSpec((1, MLA_HEADS, tm, MLA_V), head_map),
        pl.BlockSpec((tm, SGU_WIDTH), tok_map),
        pl.BlockSpec((tm, 4 * GLA_QK_W + GLA_V_W), tok_map),
        pl.BlockSpec((tm, GLA_V_W), tok_map),
    ]
    return pl.pallas_call(
        _inproj_kernel,
        grid=(n_total // tm,),
        in_specs=in_specs,
        out_specs=out_specs,
        out_shape=out_shape,
        compiler_params=_cparams(("arbitrary",)),
        name="inproj",
    )(x2, mod_rows, lw["n1"], lw["w_in"], lw["q_norm"], lw["w_uq"], lw["kv_norm"], lw["w_ukv"],
      lw["gq_nope"], lw["gk_nope"], tabs["cq"], tabs["sq"], tabs["tk"],
      lw["sgu_norm"], lw["sgu_w"], lw["sgu_b"], lw["wg"], lw["bg"])


_GLA_LEVELS = (32, 16, 8, 4, 2, 1)
_N_LEV = len(_GLA_LEVELS)
_G_SPLIT = 3
_R_QI = _N_LEV * GLA_CHUNK
_R_KS = _R_QI + GLA_CHUNK
_R_TOT = _R_KS + GLA_CHUNK
_M_ROWS = _R_TOT + 8


def _gla_constants():
    c = GLA_CHUNK
    t = np.arange(c)[:, None]
    j = np.arange(c)[None, :]
    lev, masks = [], []
    for m in _GLA_LEVELS:
        upper = (t // m) % 2 == 1
        bound = (t // (2 * m)) * 2 * m + m - 1
        lev.append(np.where(upper, (j > bound) & (j <= t), (j > t) & (j <= bound)).astype(np.float32))
        s = j
        masks.append(((t // (2 * m) == s // (2 * m)) & upper & ((s // m) % 2 == 0)).astype(np.float32))
    masks.append((t == j).astype(np.float32))
    mqi = (j <= t).astype(np.float32)
    mks = (j > t).astype(np.float32)
    tot = np.ones((8, c), np.float32)
    flip = lambda a: a[::-1, ::-1]
    m_f = np.concatenate(lev + [mqi, mks, tot], axis=0)
    m_b = np.concatenate([flip(a) for a in lev] + [flip(mqi), flip(mks), tot], axis=0)
    mask_f = np.stack([np.tile(a, (GLA_HEADS, 1)) for a in masks])
    mask_b = np.stack([np.tile(flip(a), (GLA_HEADS, 1)) for a in masks])
    m_all = np.tile(np.stack([m_f, m_b]), (1, 1, _G_SPLIT))
    return m_all, np.stack([mask_f, mask_b])


def _stack_heads(a, lane_head):
    return jnp.concatenate([jnp.where(lane_head == hh, a, jnp.zeros_like(a)) for hh in range(GLA_HEADS)], axis=0)


def _gla_chunk(q, k, g, v, st, mall, masks, lane_head):
    c = GLA_CHUNK
    g_hi = g.astype(BF16)
    r1 = g - g_hi.astype(F32)
    g_mid = r1.astype(BF16)
    g_lo = (r1 - g_mid.astype(F32)).astype(BF16)
    e = _dot(mall, jnp.concatenate([g_hi, g_mid, g_lo], axis=0))
    a = masks[_N_LEV] * _dot_nt(_stack_heads(q.astype(BF16), lane_head), k.astype(BF16))
    for lev in range(_N_LEV):
        ex = jnp.exp(e[lev * c:(lev + 1) * c])
        a = a + masks[lev] * _dot_nt(_stack_heads((q * ex).astype(BF16), lane_head), (k * ex).astype(BF16))
    qi = (q * jnp.exp(e[_R_QI:_R_QI + c])).astype(BF16)
    o_inter = _dot_nt(_stack_heads(qi, lane_head), st.astype(BF16))
    vb = v.astype(BF16)
    a16 = a.astype(BF16)
    outs = []
    for hh in range(GLA_HEADS):
        rows = slice(hh * c, (hh + 1) * c)
        outs.append(o_inter[rows] + _dot(a16[rows], vb[:, hh * GLA_DV:(hh + 1) * GLA_DV]))
    o = jnp.concatenate(outs, axis=-1)
    kd = (k * jnp.exp(e[_R_KS:_R_KS + c])).astype(BF16)
    upd = _dot_tn(vb, kd)
    st_new = st * jnp.exp(e[_R_TOT:_R_TOT + 1])
    for hh in range(GLA_HEADS):
        st_new = st_new + jnp.where(lane_head == hh, upd[hh * GLA_DV:(hh + 1) * GLA_DV], 0.0)
    return o, st_new


def _gla_kernel(fwd_ref, bwd_ref, s0_ref, mall_ref, mask_ref, of_ref, ob_ref, sfin_ref, st_ref):
    step = pl.program_id(0)
    n_batch = fwd_ref.shape[0]

    @pl.when(step == 0)
    def _():
        st_ref[...] = s0_ref[...]

    lane_head = lax.broadcasted_iota(jnp.int32, (1, GLA_QK_W), 1) // GLA_DK
    w = GLA_QK_W
    for b in range(n_batch):
        for direction, (in_ref, o_ref) in enumerate(((fwd_ref, of_ref), (bwd_ref, ob_ref))):
            mall = mall_ref[direction]
            masks = [mask_ref[direction, i] for i in range(_N_LEV + 1)]
            q = in_ref[b, :, 0:w]
            k = in_ref[b, :, w:2 * w]
            g = in_ref[b, :, (2 + direction) * w:(3 + direction) * w]
            v = in_ref[b, :, 4 * w:4 * w + GLA_V_W]
            o, st_new = _gla_chunk(q, k, g, v, st_ref[direction, b], mall, masks, lane_head)
            o_ref[b] = o
            st_ref[direction, b] = st_new

    @pl.when(step == pl.num_programs(0) - 1)
    def _():
        sfin_ref[...] = st_ref[...]


def _gla(gla_in, s0, consts):
    n_batch, n_tok, width = gla_in.shape
    nc = n_tok // GLA_CHUNK
    mall, masks = consts
    st_shape = (2, n_batch, GLA_DV, GLA_QK_W)
    return pl.pallas_call(
        _gla_kernel,
        grid=(nc,),
        in_specs=[
            pl.BlockSpec((n_batch, GLA_CHUNK, width), lambda c: (0, c, 0)),
            pl.BlockSpec((n_batch, GLA_CHUNK, width), lambda c: (0, nc - 1 - c, 0)),
            pl.BlockSpec(st_shape, lambda c: (0, 0, 0, 0)),
            pl.BlockSpec(mall.shape, lambda c: (0, 0, 0)),
            pl.BlockSpec(masks.shape, lambda c: (0, 0, 0, 0)),
        ],
        out_specs=[
            pl.BlockSpec((n_batch, GLA_CHUNK, GLA_V_W), lambda c: (0, c, 0)),
            pl.BlockSpec((n_batch, GLA_CHUNK, GLA_V_W), lambda c: (0, nc - 1 - c, 0)),
            pl.BlockSpec(st_shape, lambda c: (0, 0, 0, 0)),
        ],
        out_shape=[
            jax.ShapeDtypeStruct((n_batch, n_tok, GLA_V_W), F32),
            jax.ShapeDtypeStruct((n_batch, n_tok, GLA_V_W), F32),
            jax.ShapeDtypeStruct(st_shape, F32),
        ],
        scratch_shapes=[pltpu.VMEM(st_shape, F32)],
        compiler_params=_cparams(("arbitrary",)),
        name="gla",
    )(gla_in, gla_in, s0, mall, masks)


def _attn_kernel(*refs, n_seg):
    q_ref = refs[0]
    k_refs = refs[1:1 + n_seg]
    v_refs = refs[1 + n_seg:1 + 2 * n_seg]
    o_ref = refs[1 + 2 * n_seg]
    q = q_ref[0, 0]
    n_main = k_refs[0].shape[2]
    tk = min(TK_ATTN, n_main)
    groups = [[(0, slice(c * tk, (c + 1) * tk))] for c in range(n_main // tk)]
    groups[0] += [(i, slice(0, k_refs[i].shape[2])) for i in range(1, n_seg)]
    m = acc = den = None
    for group in groups:
        scores = [_dot_nt(q, k_refs[i][0, 0, keys, :]) for i, keys in group]
        m_g = functools.reduce(jnp.maximum, [jnp.max(s, axis=-1, keepdims=True) for s in scores])
        m_new = m_g if m is None else jnp.maximum(m, m_g)
        ps = [jnp.exp2(s - m_new) for s in scores]
        den_g = functools.reduce(jnp.add, [jnp.sum(p, axis=-1, keepdims=True) for p in ps])
        acc_g = functools.reduce(jnp.add, [_dot(p.astype(BF16), v_refs[i][0, 0, keys, :])
                                           for p, (i, keys) in zip(ps, group)])
        if m is None:
            den, acc = den_g, acc_g
        else:
            alpha = jnp.exp2(m - m_new)
            den = alpha * den + den_g
            acc = alpha * acc + acc_g
        m = m_new
    o_ref[0] = (acc / den).astype(BF16)


def _attention(q, ks, vs):
    n_batch, n_heads, nq, _ = q.shape
    tq = min(TQ_ATTN, nq)
    n_seg = len(ks)
    in_specs = [pl.BlockSpec((1, 1, tq, MLA_QK), lambda b, h, i: (b, h, i, 0))]
    for k in ks:
        in_specs.append(pl.BlockSpec((1, 1, k.shape[2], MLA_QK), lambda b, h, i: (b, h, 0, 0)))
    for v in vs:
        in_specs.append(pl.BlockSpec((1, 1, v.shape[2], MLA_V), lambda b, h, i: (b, h, 0, 0)))
    return pl.pallas_call(
        functools.partial(_attn_kernel, n_seg=n_seg),
        grid=(n_batch, n_heads, nq // tq),
        in_specs=in_specs,
        out_specs=pl.BlockSpec((1, tq, MLA_V), lambda b, h, i: (b, i, h)),
        out_shape=jax.ShapeDtypeStruct((n_batch, nq, n_heads * MLA_V), BF16),
        compiler_params=_cparams(("arbitrary", "arbitrary", "arbitrary")),
        name="attn",
    )(q, *ks, *vs)


def _outproj_kernel(a_ref, b_ref, of_ref, ob_ref, r_ref, x_ref, mod_ref, gon_ref, n2_ref, wout_ref,
                    wr_hi_ref, wr_lo_ref, br_ref, lst_ref, sl_ref,
                    x1_out, h2_out, meta_out, cnt_out):
    d = D_MODEL
    tm = x_ref.shape[0]
    gate1 = mod_ref[0, :, 2 * d:3 * d]
    shift2 = mod_ref[0, :, 3 * d:4 * d]
    scale2 = mod_ref[0, :, 4 * d:5 * d]

    og = of_ref[...] + ob_ref[...]
    gon = gon_ref[...]
    parts = []
    for hh in range(GLA_HEADS):
        seg = og[:, hh * GLA_DV:(hh + 1) * GLA_DV]
        parts.append(seg * lax.rsqrt(jnp.mean(seg * seg, axis=-1, keepdims=True) + EPS) * gon)
    cg = jnp.concatenate(parts, axis=-1) * _silu(r_ref[...])
    na = MLA_HEADS * MLA_V
    y = _dot(a_ref[...], wout_ref[0:na, :])
    y = y + _dot(b_ref[...], wout_ref[na:na + SGU_WIDTH, :])
    y = y + _dot(cg.astype(BF16), wout_ref[na + SGU_WIDTH:na + SGU_WIDTH + GLA_V_W, :])
    x1 = x_ref[...] + gate1 * y
    x1_out[...] = x1

    ms = jnp.mean(x1 * x1, axis=-1, keepdims=True)
    h2 = (x1 * lax.rsqrt(ms + EPS) * n2_ref[...]) * (1.0 + scale2) + shift2
    h_hi = h2.astype(BF16)
    h2_out[...] = h_hi
    h_lo = (h2 - h_hi.astype(F32)).astype(BF16)

    wr_hi = wr_hi_ref[...]
    lt = _dot_nt(wr_hi, h_hi) + _dot_nt(wr_hi, h_lo) + _dot_nt(wr_lo_ref[...], h_hi) + br_ref[:, 0:1]
    grp = [lt[MOE_EXPERTS + g:MOE_EXPERTS + g + 1, :] for g in range(MOE_GROUPS)]
    gmax = functools.reduce(jnp.maximum, grp)
    gsel = jnp.full(gmax.shape, MOE_GROUPS - 1, jnp.int32)
    for g in range(MOE_GROUPS - 2, -1, -1):
        gsel = jnp.where(grp[g] == gmax, g, gsel)
    gsum = functools.reduce(jnp.add, [jnp.exp(gl - gmax) for gl in grp])
    g_w = 1.0 / gsum
    e_in = []
    for j in range(MOE_PER_GROUP):
        ej = lt[j:j + 1, :]
        for g in range(1, MOE_GROUPS):
            ej = jnp.where(gsel == g, lt[g * MOE_PER_GROUP + j:g * MOE_PER_GROUP + j + 1, :], ej)
        e_in.append(ej)
    m1 = functools.reduce(jnp.maximum, e_in)
    i1 = jnp.full(m1.shape, MOE_PER_GROUP - 1, jnp.int32)
    for j in range(MOE_PER_GROUP - 2, -1, -1):
        i1 = jnp.where(e_in[j] == m1, j, i1)
    e_rest = [jnp.where(i1 == j, -jnp.inf, e_in[j]) for j in range(MOE_PER_GROUP)]
    m2 = functools.reduce(jnp.maximum, e_rest)
    i2 = jnp.full(m2.shape, MOE_PER_GROUP - 1, jnp.int32)
    for j in range(MOE_PER_GROUP - 2, -1, -1):
        i2 = jnp.where(e_rest[j] == m2, j, i2)
    p2 = jnp.exp(m2 - m1)
    w1 = g_w / (1.0 + p2)
    w2 = g_w * p2 / (1.0 + p2)
    eid1 = gsel * MOE_PER_GROUP + i1
    eid2 = gsel * MOE_PER_GROUP + i2

    e_iota = lax.broadcasted_iota(jnp.int32, (MOE_EXPERTS, tm), 0)
    hit1 = e_iota == eid1
    hit2 = e_iota == eid2
    onehot = jnp.where(hit1, 1.0, 0.0) + jnp.where(hit2, 1.0, 0.0)
    rank = _dot(onehot.astype(BF16), lst_ref[...])
    counts = jnp.sum(onehot, axis=1, keepdims=True)
    padded = jnp.ceil(counts * (1.0 / BF16_ROWS)) * BF16_ROWS
    padded_l = jnp.broadcast_to(padded, (MOE_EXPERTS, LANES))
    off = _dot(sl_ref[...], padded_l.astype(BF16))
    pos_e = off[:, 0:1] + rank
    pos1 = jnp.sum(jnp.where(hit1, pos_e, 0.0), axis=0, keepdims=True)
    pos2 = jnp.sum(jnp.where(hit2, pos_e, 0.0), axis=0, keepdims=True)
    zero = jnp.zeros_like(pos1)
    meta_out[0] = jnp.concatenate(
        [pos1, pos2, w1, w2, eid1.astype(F32), eid2.astype(F32), zero, zero], axis=0)
    cnt_out[0] = padded_l


def _outproj(a, b, o_f, o_b, r, x2, n_batch, mod_rows, mod_row0, lw, rconst):
    n_total = x2.shape[0]
    n_tok = n_total // n_batch
    tm = T_LOC
    tpb = n_tok // tm
    if mod_row0 == 0:
        mod_map = lambda i: (i // tpb, 0, 0)
    else:
        mod_map = lambda i: (mod_row0, 0, 0)
    const2 = lambda i: (0, 0)
    tok_map = lambda i: (i, 0)
    n_tiles = n_total // tm
    in_specs = [
        pl.BlockSpec((tm, MLA_HEADS * MLA_V), tok_map),
        pl.BlockSpec((tm, SGU_WIDTH), tok_map),
        pl.BlockSpec((tm, GLA_V_W), tok_map),
        pl.BlockSpec((tm, GLA_V_W), tok_map),
        pl.BlockSpec((tm, GLA_V_W), tok_map),
        pl.BlockSpec((tm, D_MODEL), tok_map),
        pl.BlockSpec((1, 1, 6 * D_MODEL), mod_map),
        pl.BlockSpec((1, GLA_DV), const2),
        pl.BlockSpec((1, D_MODEL), const2),
        pl.BlockSpec((D_MODEL, D_MODEL), const2, pipeline_mode=pl.Buffered(1)),
        pl.BlockSpec((32, D_MODEL), const2),
        pl.BlockSpec((32, D_MODEL), const2),
        pl.BlockSpec((32, LANES), const2),
        pl.BlockSpec((tm, tm), const2),
        pl.BlockSpec((MOE_EXPERTS, MOE_EXPERTS), const2),
    ]
    out_shape = [
        jax.ShapeDtypeStruct((n_total, D_MODEL), F32),
        jax.ShapeDtypeStruct((n_total, D_MODEL), BF16),
        jax.ShapeDtypeStruct((n_tiles, 8, tm), F32),
        jax.ShapeDtypeStruct((n_tiles, MOE_EXPERTS, LANES), F32),
    ]
    out_specs = [
        pl.BlockSpec((tm, D_MODEL), tok_map),
        pl.BlockSpec((tm, D_MODEL), tok_map),
        pl.BlockSpec((1, 8, tm), lambda i: (i, 0, 0)),
        pl.BlockSpec((1, MOE_EXPERTS, LANES), lambda i: (i, 0, 0)),
    ]
    return pl.pallas_call(
        _outproj_kernel,
        grid=(n_tiles,),
        in_specs=in_specs,
        out_specs=out_specs,
        out_shape=out_shape,
        compiler_params=_cparams(("arbitrary",)),
        name="outproj",
    )(a, b, o_f, o_b, r, x2, mod_rows, lw["gla_out_norm"], lw["n2"], lw["w_out"],
      lw["wr_hi"], lw["wr_lo"], lw["br"], rconst["lst"], rconst["sl"])


def _piece_copies(src_ref, dst_ref, sem, src_row, dst_row, n_rows):
    copies = []
    for size in _PIECE_SIZES:
        done = n_rows & ~(2 * size - 1)
        pred = (n_rows & size) != 0
        s0 = pl.multiple_of(src_row + done, BF16_ROWS)
        d0 = pl.multiple_of(dst_row + done, BF16_ROWS)
        cp = pltpu.make_async_copy(src_ref.at[pl.ds(s0, size)], dst_ref.at[pl.ds(d0, size)], sem)
        copies.append((pred, cp))
    return copies


def _list_base(tile, s):
    return tile * len(_PIECE_SIZES) + s


def _start_listed_pieces(src_ref, dst_ref, sem, src_tab, dst_tab, n_tab, tile):
    for s, size in enumerate(_PIECE_SIZES):
        base = _list_base(tile, s)

        def body(i, carry, size=size, base=base):
            idx = base * MOE_EXPERTS + i
            s0 = pl.multiple_of(src_tab[idx], BF16_ROWS)
            d0 = pl.multiple_of(dst_tab[idx], BF16_ROWS)
            pltpu.make_async_copy(src_ref.at[pl.ds(s0, size)], dst_ref.at[pl.ds(d0, size)], sem).start()
            return carry

        lax.fori_loop(0, n_tab[base], body, 0)


def _wait_listed_pieces(src_ref, dst_ref, sem, n_tab, tile):
    for s, size in enumerate(_PIECE_SIZES):
        def body(i, carry, size=size):
            pltpu.make_async_copy(src_ref.at[pl.ds(0, size)], dst_ref.at[pl.ds(0, size)], sem).wait()
            return carry

        lax.fori_loop(0, n_tab[_list_base(tile, s)], body, 0)


def _dispatch_kernel(ploc_tab, pdst_tab, pn_tab, ztab_dst, ztab_cnt, n_used, h2_ref, meta_ref, xs_ref,
                     xloc_ref, zeros_ref, sems):
    j = pl.program_id(0)
    n_tiles = pl.num_programs(0)
    tm = h2_ref.shape[0]
    slot = j % 2

    @pl.when(j == 0)
    def _():
        zeros_ref[...] = jnp.zeros_like(zeros_ref)

        def fill(t, carry):
            cp = pltpu.make_async_copy(zeros_ref, xs_ref.at[pl.ds(pl.multiple_of(t * TM_FFN, TM_FFN), TM_FFN)],
                                       sems.at[0])
            cp.start()
            cp.wait()
            return carry
        lax.fori_loop(n_used[0], xs_ref.shape[0] // TM_FFN, fill, 0)
        zcopies = []
        for e in range(MOE_EXPERTS):
            zcopies += _piece_copies(zeros_ref, xs_ref, sems.at[0], 0, ztab_dst[e], ztab_cnt[e])
        for pred, cp in zcopies:
            @pl.when(pred)
            def _(cp=cp):
                cp.start()
        for pred, cp in zcopies:
            @pl.when(pred)
            def _(cp=cp):
                cp.wait()

    pos1 = meta_ref[0, 0:1, :]
    pos2 = meta_ref[0, 1:2, :]
    s_iota = lax.broadcasted_iota(jnp.int32, (SLOTS, tm), 0).astype(F32)
    perm = jnp.where((s_iota == pos1) | (s_iota == pos2), 1.0, 0.0).astype(BF16)
    xloc_ref[slot] = _dot(perm, h2_ref[...]).astype(BF16)

    @pl.when(j > 0)
    def _():
        _wait_listed_pieces(xloc_ref.at[1 - slot], xs_ref, sems.at[1 - slot], pn_tab, j - 1)

    _start_listed_pieces(xloc_ref.at[slot], xs_ref, sems.at[slot], ploc_tab, pdst_tab, pn_tab, j)

    @pl.when(j == n_tiles - 1)
    def _():
        _wait_listed_pieces(xloc_ref.at[slot], xs_ref, sems.at[slot], pn_tab, j)


def _dispatch(h2, meta, tabs, r_max):
    n_tiles = meta.shape[0]
    tm = T_LOC
    grid_spec = pltpu.PrefetchScalarGridSpec(
        num_scalar_prefetch=6,
        grid=(n_tiles,),
        in_specs=[pl.BlockSpec((tm, D_MODEL), lambda i, *_: (i, 0)),
                  pl.BlockSpec((1, 8, tm), lambda i, *_: (i, 0, 0))],
        out_specs=pl.BlockSpec(memory_space=pl.ANY),
        scratch_shapes=[pltpu.VMEM((2, SLOTS, D_MODEL), BF16), pltpu.VMEM((TM_FFN, D_MODEL), BF16),
                        pltpu.SemaphoreType.DMA((2,))],
    )
    return pl.pallas_call(
        _dispatch_kernel,
        grid_spec=grid_spec,
        out_shape=jax.ShapeDtypeStruct((r_max, D_MODEL), BF16),
        compiler_params=_cparams(("arbitrary",)),
        name="dispatch",
    )(tabs["ploc"], tabs["pdst"], tabs["pn"], tabs["zdst"], tabs["zcnt"], tabs["n_used"], h2, meta)


def _ffn_kernel(tile_e, tile_blk, n_used, x_ref, w1_ref, w3_ref, w2_ref, y_ref):
    i = pl.program_id(0)

    @pl.when(i < n_used[0])
    def _():
        x = x_ref[...]
        acc = None
        for f in range(D_EXPERT // F_CHUNK):
            cols = slice(f * F_CHUNK, (f + 1) * F_CHUNK)
            h1 = _dot(x, w1_ref[0, 0, :, cols])
            h3 = _dot(x, w3_ref[0, 0, :, cols])
            act = (_silu(h1) * h3).astype(BF16)
            part = _dot(act, w2_ref[0, 0, cols, :])
            acc = part if acc is None else acc + part
        y_ref[...] = acc.astype(BF16)

    @pl.when(i >= n_used[0])
    def _():
        y_ref[...] = jnp.zeros_like(y_ref)


def _ffn(xs, layer, w1, w3, w2, tabs):
    r_max = xs.shape[0]
    n_mt = r_max // TM_FFN
    grid_spec = pltpu.PrefetchScalarGridSpec(
        num_scalar_prefetch=3,
        grid=(n_mt,),
        in_specs=[
            pl.BlockSpec((TM_FFN, D_MODEL), lambda i, te, tb, nu: (tb[i], 0)),
            pl.BlockSpec((1, 1, D_MODEL, D_EXPERT), lambda i, te, tb, nu: (layer, te[i], 0, 0)),
            pl.BlockSpec((1, 1, D_MODEL, D_EXPERT), lambda i, te, tb, nu: (layer, te[i], 0, 0)),
            pl.BlockSpec((1, 1, D_EXPERT, D_MODEL), lambda i, te, tb, nu: (layer, te[i], 0, 0)),
        ],
        out_specs=pl.BlockSpec((TM_FFN, D_MODEL), lambda i, te, tb, nu: (i, 0)),
    )
    return pl.pallas_call(
        _ffn_kernel,
        grid_spec=grid_spec,
        out_shape=jax.ShapeDtypeStruct((r_max, D_MODEL), BF16),
        compiler_params=_cparams(("arbitrary",)),
        name="ffn",
    )(tabs["tile_e"], tabs["tile_blk"], tabs["n_used"], xs, w1, w3, w2)


def _combine_kernel(ploc_tab, pdst_tab, pn_tab, y_ref, metac_ref, x1_ref, mod_ref, out_ref, yloc_ref, sems):
    j = pl.program_id(0)
    n_tiles = pl.num_programs(0)
    d = D_MODEL
    tm = x1_ref.shape[0]
    slot = j % 2

    @pl.when(j == 0)
    def _():
        yloc_ref[...] = jnp.zeros_like(yloc_ref)
        _start_listed_pieces(y_ref, yloc_ref.at[0], sems.at[0], pdst_tab, ploc_tab, pn_tab, 0)

    @pl.when(j + 1 < n_tiles)
    def _():
        _start_listed_pieces(y_ref, yloc_ref.at[1 - slot], sems.at[1 - slot], pdst_tab, ploc_tab, pn_tab, j + 1)

    _wait_listed_pieces(y_ref, yloc_ref.at[slot], sems.at[slot], pn_tab, j)

    mc = metac_ref[0]
    s_iota = lax.broadcasted_iota(jnp.int32, (tm, SLOTS), 1).astype(F32)
    wc = jnp.where(s_iota == mc[:, 0:1], mc[:, 2:3], 0.0) + jnp.where(s_iota == mc[:, 1:2], mc[:, 3:4], 0.0)
    moe = _dot(wc.astype(BF16), yloc_ref[slot])
    gate2 = mod_ref[0, :, 5 * d:6 * d]
    out_ref[...] = x1_ref[...] + gate2 * moe


def _combine(y, metac, x1, n_batch, mod_rows, mod_row0, tabs):
    n_total = x1.shape[0]
    n_tok = n_total // n_batch
    tm = T_LOC
    tpb = n_tok // tm
    n_tiles = n_total // tm
    if mod_row0 == 0:
        mod_map = lambda i, *_: (i // tpb, 0, 0)
    else:
        mod_map = lambda i, *_: (mod_row0, 0, 0)
    grid_spec = pltpu.PrefetchScalarGridSpec(
        num_scalar_prefetch=3,
        grid=(n_tiles,),
        in_specs=[
            pl.BlockSpec(memory_space=pl.ANY),
            pl.BlockSpec((1, tm, 8), lambda i, *_: (i, 0, 0)),
            pl.BlockSpec((tm, D_MODEL), lambda i, *_: (i, 0)),
            pl.BlockSpec((1, 1, 6 * D_MODEL), mod_map),
        ],
        out_specs=pl.BlockSpec((tm, D_MODEL), lambda i, *_: (i, 0)),
        scratch_shapes=[pltpu.VMEM((2, SLOTS, D_MODEL), BF16), pltpu.SemaphoreType.DMA((2,))],
    )
    return pl.pallas_call(
        _combine_kernel,
        grid_spec=grid_spec,
        out_shape=jax.ShapeDtypeStruct((n_total, D_MODEL), F32),
        compiler_params=_cparams(("arbitrary",)),
        name="combine",
    )(tabs["ploc"], tabs["pdst"], tabs["pn"], y, metac, x1, mod_rows)


def _rope_tables(n_tokens):
    rows = n_tokens // GRID_W
    row = jnp.repeat(jnp.arange(rows, dtype=jnp.int32), GRID_W)
    col = jnp.tile(jnp.arange(GRID_W, dtype=jnp.int32), rows)
    n_freq = MLA_ROPE // 4
    inv_freq = ROPE_THETA ** (-jnp.arange(n_freq, dtype=F32) / n_freq)
    ang_r = row[:, None].astype(F32) * inv_freq
    ang_c = col[:, None].astype(F32) * inv_freq
    cos = jnp.concatenate([jnp.cos(ang_r)] * 2 + [jnp.cos(ang_c)] * 2, axis=-1)
    sin = jnp.concatenate([-jnp.sin(ang_r), jnp.sin(ang_r), -jnp.sin(ang_c), jnp.sin(ang_c)], axis=-1)
    return cos, sin


_SWAP64 = np.concatenate([np.arange(16, 32), np.arange(0, 16), np.arange(48, 64), np.arange(32, 48)])


def _position_tables(cos, sin, q_gain, k_gain):
    gq = q_gain[MLA_NOPE:]
    gk = k_gain[MLA_NOPE:]
    cq = jnp.tile(cos * gq, (1, MLA_HEADS))
    sq = jnp.tile(sin * gq[_SWAP64], (1, MLA_HEADS))
    tk = jnp.concatenate([cos * gk, sin * gk[_SWAP64]], axis=-1)
    return {"cq": cq, "sq": sq, "tk": tk}


def _layer_weights(l, p):
    d = D_MODEL
    o = _IN_OFF
    w_in = p["w_in"][l]
    seg = lambda i: w_in[:, o[i]:o[i + 1]]
    kpe = seg(2)
    w_in_r = jnp.concatenate(
        [seg(0), seg(1), seg(3), seg(4), seg(5) * (GLA_DK ** -0.5), seg(6), seg(7), seg(8),
         kpe, kpe[:, _SWAP64], seg(9), seg(10), jnp.zeros((d, Z_WIDTH - Z_GG - 2 * GLA_GATE_RANK), F32)],
        axis=-1).astype(BF16)
    w_uq = p["mla_w_uq"][l].reshape(MLA_Q_RANK, MLA_HEADS, MLA_QK)
    q_nope = w_uq[:, :, :MLA_NOPE].reshape(MLA_Q_RANK, -1)
    q_pe = w_uq[:, :, MLA_NOPE:]
    w_uq_r = jnp.concatenate([q_nope, q_pe.reshape(MLA_Q_RANK, -1), q_pe[:, :, _SWAP64].reshape(MLA_Q_RANK, -1)],
                             axis=-1).astype(BF16)
    w_ukv = p["mla_w_ukv"][l].reshape(MLA_KV_RANK, MLA_HEADS, MLA_NOPE + MLA_V)
    w_ukv_r = jnp.concatenate([w_ukv[:, :, :MLA_NOPE].reshape(MLA_KV_RANK, -1),
                               w_ukv[:, :, MLA_NOPE:].reshape(MLA_KV_RANK, -1)], axis=-1).astype(BF16)
    wg = jnp.zeros((LANES, 2 * GLA_QK_W), F32)
    wg = wg.at[0:GLA_GATE_RANK, 0:GLA_QK_W].set(p["gla_wg_f"][l])
    wg = wg.at[GLA_GATE_RANK:2 * GLA_GATE_RANK, GLA_QK_W:].set(p["gla_wg_b"][l])
    bg = jnp.concatenate([p["gla_bg_f"][l], p["gla_bg_b"][l]])[None, :]
    sgu_b = jnp.repeat(p["sgu_b"][l].T, SGU_GROUP_DIM, axis=1)
    wr = jnp.zeros((32, d), F32)
    wr = wr.at[0:MOE_EXPERTS].set(p["moe_w_expert"][l].T)
    wr = wr.at[MOE_EXPERTS:MOE_EXPERTS + MOE_GROUPS].set(p["moe_w_group"][l].T)
    wr_hi = wr.astype(BF16)
    wr_lo = (wr - wr_hi.astype(F32)).astype(BF16)
    br = jnp.zeros((32,), F32).at[0:MOE_EXPERTS].set(p["moe_b_expert"][l])
    br = br.at[MOE_EXPERTS:MOE_EXPERTS + MOE_GROUPS].set(p["moe_b_group"][l])
    return {
        "n1": p["norm1_g"][l][None, :], "n2": p["norm2_g"][l][None, :],
        "w_in": w_in_r, "q_norm": p["mla_q_norm"][l][None, :], "kv_norm": p["mla_kv_norm"][l][None, :],
        "w_uq": w_uq_r, "w_ukv": w_ukv_r,
        "gq_nope": p["mla_q_gain"][l][None, :MLA_NOPE], "gk_nope": p["mla_k_gain"][l][None, :MLA_NOPE],
        "sgu_norm": p["sgu_norm"][l].reshape(1, SGU_WIDTH), "sgu_w": p["sgu_w"][l].astype(BF16), "sgu_b": sgu_b,
        "wg": wg.astype(BF16), "bg": bg,
        "gla_out_norm": p["gla_out_norm"][l][None, :], "w_out": p["w_out"][l].astype(BF16),
        "wr_hi": wr_hi, "wr_lo": wr_lo, "br": jnp.broadcast_to(br[:, None], (32, LANES)),
        "layer": l, "w1": p["moe_w1_bf16"], "w3": p["moe_w3_bf16"], "w2": p["moe_w2_bf16"],
    }


def _router_constants():
    t = np.arange(T_LOC)
    lst = (t[:, None] < t[None, :]).astype(np.float32)
    e = np.arange(MOE_EXPERTS)
    sl = (e[None, :] < e[:, None]).astype(np.float32)
    return {"lst": jnp.asarray(lst, BF16), "sl": jnp.asarray(sl, BF16)}


def _dispatch_tables(cnt, r_max):
    n_tiles = cnt.shape[0]
    cnt = cnt.astype(jnp.int32)
    loc_off = jnp.cumsum(cnt, axis=1) - cnt
    rows_e = jnp.sum(cnt, axis=0)
    tiles_e = (rows_e + TM_FFN - 1) // TM_FFN
    tile_start = jnp.cumsum(tiles_e) - tiles_e
    base_e = tile_start * TM_FFN
    dst = base_e[None, :] + jnp.cumsum(cnt, axis=0) - cnt
    n_used = jnp.sum(tiles_e)
    n_mt = r_max // TM_FFN
    ti = jnp.arange(n_mt, dtype=jnp.int32)
    tile_blk = jnp.minimum(ti, n_used - 1)
    tile_e = jnp.sum(tile_blk[:, None] >= (tile_start + tiles_e)[None, :], axis=1).astype(jnp.int32)
    tile_e = jnp.minimum(tile_e, MOE_EXPERTS - 1)
    zdst = base_e + rows_e
    zcnt = tiles_e * TM_FFN - rows_e
    i32 = lambda a: a.astype(jnp.int32)
    sizes = jnp.asarray(_PIECE_SIZES, jnp.int32)
    has = (cnt[:, :, None] & sizes) != 0
    done = cnt[:, :, None] & ~(2 * sizes - 1)
    pos = jnp.cumsum(has, axis=1) - 1
    pick = has[..., None] & (pos[..., None] == jnp.arange(MOE_EXPERTS))
    listed = lambda a: jnp.sum(jnp.where(pick, (a[:, :, None] + done)[..., None], 0), axis=1)
    return {"ploc": i32(listed(loc_off).reshape(-1)), "pdst": i32(listed(dst).reshape(-1)),
            "pn": i32(jnp.sum(has, axis=1).reshape(-1)), "zdst": i32(zdst), "zcnt": i32(zcnt),
            "tile_e": tile_e, "tile_blk": i32(tile_blk), "n_used": i32(n_used.reshape(1))}


def _moe(h2, meta, cnt, x1, n_batch, mod_rows, mod_row0, lw):
    n_tiles = meta.shape[0]
    worst_rows = n_tiles * SLOTS + MOE_EXPERTS * (TM_FFN - BF16_ROWS)
    r_max = -(-worst_rows // TM_FFN) * TM_FFN
    tabs = _dispatch_tables(cnt[:, :, 0], r_max)
    xs = _dispatch(h2, meta, tabs, r_max)
    y = _ffn(xs, lw["layer"], lw["w1"], lw["w3"], lw["w2"], tabs)
    metac = jnp.swapaxes(meta, 1, 2)
    return _combine(y, metac, x1, n_batch, mod_rows, mod_row0, tabs)


def kernel(x, c, ctx, c_ctx, norm1_g, norm2_g, ada_w, ada_b, w_in, mla_q_norm, mla_w_uq, mla_kv_norm, mla_w_ukv,
           mla_q_gain, mla_k_gain, sgu_norm, sgu_w, sgu_b, gla_wg_f, gla_bg_f, gla_wg_b, gla_bg_b, gla_out_norm,
           w_out, moe_w_group, moe_b_group, moe_w_expert, moe_b_expert, moe_w1, moe_w3, moe_w2):
    p = dict(norm1_g=norm1_g, norm2_g=norm2_g, w_in=w_in, mla_q_norm=mla_q_norm, mla_w_uq=mla_w_uq,
             mla_kv_norm=mla_kv_norm, mla_w_ukv=mla_w_ukv, mla_q_gain=mla_q_gain, mla_k_gain=mla_k_gain,
             sgu_norm=sgu_norm, sgu_w=sgu_w, sgu_b=sgu_b, gla_wg_f=gla_wg_f, gla_bg_f=gla_bg_f,
             gla_wg_b=gla_wg_b, gla_bg_b=gla_bg_b, gla_out_norm=gla_out_norm, w_out=w_out,
             moe_w_group=moe_w_group, moe_b_group=moe_b_group, moe_w_expert=moe_w_expert,
             moe_b_expert=moe_b_expert, moe_w1_bf16=moe_w1.astype(BF16), moe_w3_bf16=moe_w3.astype(BF16),
             moe_w2_bf16=moe_w2.astype(BF16))
    n_batch, seq, d = x.shape
    n_ctx = ctx.shape[1]
    depth = ada_w.shape[0]
    assert d == D_MODEL and n_batch <= 4 and seq % T_LOC == 0 and n_ctx % TM_IN == 0
    assert (n_batch * n_ctx) % T_LOC == 0

    c_rows = jnp.zeros((8, d), F32).at[0:n_batch].set(c).at[4].set(c_ctx)
    mod = _ada(c_rows, ada_w, ada_b)

    cos, sin = _rope_tables(seq)
    ones_c = jnp.ones((TM_IN, MLA_ROPE), F32)
    gla_consts_np = _gla_constants()
    gla_consts = (jnp.asarray(gla_consts_np[0], BF16), jnp.asarray(gla_consts_np[1], F32))
    rconst = _router_constants()
    zero_state = jnp.zeros((2, n_batch, GLA_DV, GLA_QK_W), F32)

    x2 = x.reshape(n_batch * seq, d)
    ctx2 = ctx.reshape(n_batch * n_ctx, d)
    for l in range(depth):
        ctx_out = l < depth - 1
        lw = _layer_weights(l, p)
        mod_rows = mod[l].reshape(8, 1, 6 * d)
        tabs_x = _position_tables(cos, sin, mla_q_gain[l], mla_k_gain[l])
        tabs_c = _position_tables(ones_c, 0.0 * ones_c, mla_q_gain[l], mla_k_gain[l])

        q_x, k_x, v_x, b_x, gla_x, r_x = _inproj(x2, n_batch, mod_rows, 0, lw, tabs_x)
        q_c, k_c, v_c, b_c, gla_c, r_c = _inproj(ctx2, n_batch, mod_rows, 4, lw, tabs_c)

        of_c, ob_c, s_ctx = _gla(gla_c.reshape(n_batch, n_ctx, -1), zero_state, gla_consts)
        of_x, ob_x, _ = _gla(gla_x.reshape(n_batch, seq, -1), s_ctx, gla_consts)
        a_x = _attention(q_x, [k_x, k_c], [v_x, v_c])

        x1, h2, meta, cnt = _outproj(a_x.reshape(n_batch * seq, -1), b_x, of_x.reshape(n_batch * seq, -1),
                                     ob_x.reshape(n_batch * seq, -1), r_x, x2, n_batch, mod_rows, 0, lw, rconst)
        x2 = _moe(h2, meta, cnt, x1, n_batch, mod_rows, 0, lw)

        if ctx_out:
            a_c = _attention(q_c, [k_c], [v_c])
            c1, h2c, meta_c, cnt_c = _outproj(a_c.reshape(n_batch * n_ctx, -1), b_c,
                                              of_c.reshape(n_batch * n_ctx, -1), ob_c.reshape(n_batch * n_ctx, -1),
                                              r_c, ctx2, n_batch, mod_rows, 4, lw, rconst)
            ctx2 = _moe(h2c, meta_c, cnt_c, c1, n_batch, mod_rows, 4, lw)
    return x2.reshape(n_batch, seq, d)
```

```python
import functools

import numpy as np
import jax
import jax.numpy as jnp
from jax import lax
from jax.experimental import pallas as pl
from jax.experimental.pallas import tpu as pltpu

F32 = jnp.float32
BF16 = jnp.bfloat16

D_MODEL = 2048
GRID_W = 64
EPS = 1e-6

MLA_HEADS = 8
MLA_Q_RANK = 512
MLA_KV_RANK = 512
MLA_NOPE = 128
MLA_ROPE = 64
MLA_V = 128
MLA_V_AUG = 2 * MLA_V
MLA_QK = MLA_NOPE + MLA_ROPE
MLA_SCALE = MLA_QK ** -0.5
ROPE_THETA = 10000.0
LOG2_E = float(np.log2(np.e))

SGU_WIDTH = 512
SGU_GROUPS = 4
SGU_GROUP_DIM = SGU_WIDTH // SGU_GROUPS
SGU_CHUNK = 128

GLA_HEADS = 4
GLA_DK = 64
GLA_DV = 128
GLA_GATE_RANK = 16
GLA_GATE_NORM = 16.0
GLA_CHUNK = 64
GLA_QK_W = GLA_HEADS * GLA_DK
GLA_V_W = GLA_HEADS * GLA_DV

MOE_GROUPS = 4
MOE_PER_GROUP = 4
MOE_EXPERTS = MOE_GROUPS * MOE_PER_GROUP
D_EXPERT = 1024

_IN_SPLITS = (512, 512, 64, 512, 512, 256, 256, 512, 512, 16, 16)
_IN_OFF = np.concatenate([[0], np.cumsum(_IN_SPLITS)]).tolist()

Z_CQ, Z_CKV, Z_ZU, Z_ZV = 0, 512, 1024, 1536
Z_GQ, Z_GK, Z_GV, Z_GR = 2048, 2304, 2560, 3072
Z_KPE, Z_GG = 3584, 3712
Z_WIDTH = 3840

LANES = 128
BF16_ROWS = 16
VMEM_LIMIT = 56 * 1024 * 1024
VMEM_LIMIT_INPROJ = 61 * 1024 * 1024

TM_IN = 512
TM_SUB = 256
TQ_ATTN = 1024
TK_ATTN = 2048
T_LOC = 512
SLOTS = 2 * T_LOC + MOE_EXPERTS * BF16_ROWS
TM_FFN = 512
F_CHUNK = 256
_PIECE_SIZES = (512, 256, 128, 64, 32, 16)


def _cparams(sem, vmem=VMEM_LIMIT):
    return pltpu.CompilerParams(dimension_semantics=sem, vmem_limit_bytes=vmem)


def _silu(x):
    return x / (1.0 + jnp.exp(-x))


def _gelu_tanh(x):
    return 0.5 * x * (1.0 + jnp.tanh(np.sqrt(2.0 / np.pi).astype(np.float32) * (x + 0.044715 * (x * x * x))))


def _log_sigmoid(x):
    return jnp.minimum(x, 0.0) - jnp.log(1.0 + jnp.exp(-jnp.abs(x)))


def _dot(a, b):
    return jnp.dot(a, b, preferred_element_type=F32)


def _dot_nt(a, b):
    return lax.dot_general(a, b, (((1,), (1,)), ((), ())), preferred_element_type=F32)


def _dot_tn(a, b):
    return lax.dot_general(a, b, (((0,), (0,)), ((), ())), preferred_element_type=F32)


def _ada_kernel(c_ref, w_ref, b_ref, o_ref):
    a = _silu(c_ref[...]).astype(BF16)
    o_ref[0] = _dot(a, w_ref[0].astype(BF16)) + b_ref[0]


def _ada(c_rows, ada_w, ada_b):
    depth, d, n = ada_w.shape
    tn = 1536
    return pl.pallas_call(
        _ada_kernel,
        grid=(depth, n // tn),
        in_specs=[
            pl.BlockSpec((8, d), lambda l, j: (0, 0)),
            pl.BlockSpec((1, d, tn), lambda l, j: (l, 0, j)),
            pl.BlockSpec((1, 1, tn), lambda l, j: (l, 0, j)),
        ],
        out_specs=pl.BlockSpec((1, 8, tn), lambda l, j: (l, 0, j)),
        out_shape=jax.ShapeDtypeStruct((depth, 8, n), F32),
        compiler_params=_cparams(("arbitrary", "arbitrary")),
        name="ada",
    )(c_rows, ada_w, ada_b.reshape(depth, 1, n))


def _inproj_kernel(x_ref, mod_ref, n1_ref, win_ref, qn_ref, wuq_ref, kvn_ref, wukv_ref,
                   gqn_ref, gkn_ref, cq_ref, sq_ref, tk_ref,
                   sgn_ref, sgw_ref, sgb_ref, wg_ref, bg_ref,
                   q_out, k_out, v_out, b_out, gla_out, r_out):
    for t in range(x_ref.shape[0] // TM_SUB):
        _inproj_rows(slice(t * TM_SUB, (t + 1) * TM_SUB), x_ref, mod_ref, n1_ref, win_ref, qn_ref, wuq_ref,
                     kvn_ref, wukv_ref, gqn_ref, gkn_ref, cq_ref, sq_ref, tk_ref, sgn_ref, sgw_ref, sgb_ref,
                     wg_ref, bg_ref, q_out, k_out, v_out, b_out, gla_out, r_out)


def _inproj_rows(rs, x_ref, mod_ref, n1_ref, win_ref, qn_ref, wuq_ref, kvn_ref, wukv_ref,
                 gqn_ref, gkn_ref, cq_ref, sq_ref, tk_ref, sgn_ref, sgw_ref, sgb_ref, wg_ref, bg_ref,
                 q_out, k_out, v_out, b_out, gla_out, r_out):
    d = D_MODEL
    tm = rs.stop - rs.start
    x = x_ref[rs, :]
    shift = mod_ref[0, :, 0:d]
    scale = mod_ref[0, :, d:2 * d]
    ms = jnp.mean(x * x, axis=-1, keepdims=True)
    h = (x * lax.rsqrt(ms + EPS) * n1_ref[...]) * (1.0 + scale) + shift
    hb = h.astype(BF16)
    z_mla = _dot(hb, win_ref[:, 0:Z_ZU])
    z_end = _dot(hb, win_ref[:, Z_KPE:Z_WIDTH])
    z_mid = _dot(hb, win_ref[:, Z_ZU:Z_KPE])

    cq = z_mla[:, Z_CQ:Z_CQ + MLA_Q_RANK]
    ckv = z_mla[:, Z_CKV:Z_CKV + MLA_KV_RANK]
    cqn = cq * lax.rsqrt(jnp.mean(cq * cq, axis=-1, keepdims=True) + EPS) * qn_ref[...]
    ckvn = ckv * lax.rsqrt(jnp.mean(ckv * ckv, axis=-1, keepdims=True) + EPS) * kvn_ref[...]
    qa = _dot(cqn.astype(BF16), wuq_ref[...])
    kv = _dot(ckvn.astype(BF16), wukv_ref[...])
    npe = MLA_HEADS * MLA_NOPE
    rw = MLA_HEADS * MLA_ROPE
    q_pe = qa[:, npe:npe + rw]
    q_rot = q_pe * cq_ref[rs, :] + qa[:, npe + rw:npe + 2 * rw] * sq_ref[rs, :]
    zk = z_end[:, 0:2 * MLA_ROPE]
    tkz = zk * tk_ref[rs, :]
    k_rot = tkz[:, 0:MLA_ROPE] + tkz[:, MLA_ROPE:2 * MLA_ROPE]
    k_pe = zk[:, 0:MLA_ROPE]
    ss_kpe = jnp.sum(k_pe * k_pe, axis=-1, keepdims=True)
    gqn = gqn_ref[...]
    gkn = gkn_ref[...]
    for hh in range(MLA_HEADS):
        qn_h = qa[:, hh * MLA_NOPE:(hh + 1) * MLA_NOPE]
        qp_h = q_pe[:, hh * MLA_ROPE:(hh + 1) * MLA_ROPE]
        ssq = jnp.sum(qn_h * qn_h, axis=-1, keepdims=True) + jnp.sum(qp_h * qp_h, axis=-1, keepdims=True)
        rq = lax.rsqrt(ssq * (1.0 / MLA_QK) + EPS) * (MLA_SCALE * LOG2_E)
        q_h = jnp.concatenate([qn_h * gqn * rq, q_rot[:, hh * MLA_ROPE:(hh + 1) * MLA_ROPE] * rq], axis=-1)
        q_out[0, hh, rs, :] = q_h.astype(BF16)
        kn_h = kv[:, hh * MLA_NOPE:(hh + 1) * MLA_NOPE]
        ssk = jnp.sum(kn_h * kn_h, axis=-1, keepdims=True) + ss_kpe
        rk = lax.rsqrt(ssk * (1.0 / MLA_QK) + EPS)
        k_h = jnp.concatenate([kn_h * gkn * rk, k_rot * rk], axis=-1)
        k_out[0, hh, rs, :] = k_h.astype(BF16)
        v_h = kv[:, npe + hh * MLA_V:npe + (hh + 1) * MLA_V]
        v_out[0, hh, rs, :] = jnp.concatenate([v_h, jnp.ones_like(v_h)], axis=-1).astype(BF16)

    u = _gelu_tanh(z_mid[:, 0:SGU_WIDTH])
    vv = _gelu_tanh(z_mid[:, Z_ZV - Z_ZU:Z_ZV - Z_ZU + SGU_WIDTH])
    sgn = sgn_ref[...]
    for c in range(tm // SGU_CHUNK):
        rows = slice(c * SGU_CHUNK, (c + 1) * SGU_CHUNK)
        outs = []
        for g in range(SGU_GROUPS):
            cols = slice(g * SGU_GROUP_DIM, (g + 1) * SGU_GROUP_DIM)
            seg = vv[rows, cols]
            vn = seg * lax.rsqrt(jnp.mean(seg * seg, axis=-1, keepdims=True) + EPS) * sgn[:, cols]
            outs.append(_dot(sgw_ref[g], vn.astype(BF16)))
        sp = jnp.concatenate(outs, axis=-1) + sgb_ref[...]
        b_out[rs.start + c * SGU_CHUNK:rs.start + (c + 1) * SGU_CHUNK, :] = (u[rows, :] * sp).astype(BF16)

    gl = _dot(z_end[:, Z_GG - Z_KPE:Z_GG - Z_KPE + LANES].astype(BF16), wg_ref[...]) + bg_ref[...]
    gdec = _log_sigmoid(gl) * (1.0 / GLA_GATE_NORM)
    gla_out[rs, 0:2 * GLA_QK_W] = z_mid[:, Z_GQ - Z_ZU:Z_GQ - Z_ZU + 2 * GLA_QK_W]
    gla_out[rs, 2 * GLA_QK_W:4 * GLA_QK_W] = gdec
    gla_out[rs, 4 * GLA_QK_W:4 * GLA_QK_W + GLA_V_W] = z_mid[:, Z_GV - Z_ZU:Z_GV - Z_ZU + GLA_V_W]
    r_out[rs, :] = z_mid[:, Z_GR - Z_ZU:Z_GR - Z_ZU + GLA_V_W]


def _inproj(x2, n_batch, mod_rows, mod_row0, lw, tabs):
    n_total = x2.shape[0]
    n_tok = n_total // n_batch
    tm = min(TM_IN, n_tok)
    tpb = n_tok // tm
    tab_blocks = tabs["cq"].shape[0] // tm
    if mod_row0 == 0:
        mod_map = lambda i: (i // tpb, 0, 0)
    else:
        mod_map = lambda i: (mod_row0, 0, 0)
    const2 = lambda i: (0, 0)
    tab_map = lambda i: (i % tab_blocks, 0)
    tok_map = lambda i: (i, 0)
    head_map = lambda i: (i // tpb, 0, i % tpb, 0)
    in_specs = [
        pl.BlockSpec((tm, D_MODEL), tok_map),
        pl.BlockSpec((1, 1, 6 * D_MODEL), mod_map),
        pl.BlockSpec((1, D_MODEL), const2),
        pl.BlockSpec((D_MODEL, Z_WIDTH), const2, pipeline_mode=pl.Buffered(1)),
        pl.BlockSpec((1, MLA_Q_RANK), const2),
        pl.BlockSpec((MLA_Q_RANK, 2048), const2, pipeline_mode=pl.Buffered(1)),
        pl.BlockSpec((1, MLA_KV_RANK), const2),
        pl.BlockSpec((MLA_KV_RANK, 2048), const2, pipeline_mode=pl.Buffered(1)),
        pl.BlockSpec((1, MLA_NOPE), const2),
        pl.BlockSpec((1, MLA_NOPE), const2),
        pl.BlockSpec((tm, MLA_HEADS * MLA_ROPE), tab_map),
        pl.BlockSpec((tm, MLA_HEADS * MLA_ROPE), tab_map),
        pl.BlockSpec((tm, 2 * MLA_ROPE), tab_map),
        pl.BlockSpec((1, SGU_WIDTH), const2),
        pl.BlockSpec((SGU_GROUPS, SGU_CHUNK, SGU_CHUNK), lambda i: (0, 0, 0)),
        pl.BlockSpec((SGU_CHUNK, SGU_WIDTH), const2),
        pl.BlockSpec((LANES, 2 * GLA_QK_W), const2),
        pl.BlockSpec((1, 2 * GLA_QK_W), const2),
    ]
    out_shape = [
        jax.ShapeDtypeStruct((n_batch, MLA_HEADS, n_tok, MLA_QK), BF16),
        jax.ShapeDtypeStruct((n_batch, MLA_HEADS, n_tok, MLA_QK), BF16),
        jax.ShapeDtypeStruct((n_batch, MLA_HEADS, n_tok, MLA_V_AUG), BF16),
        jax.ShapeDtypeStruct((n_total, SGU_WIDTH), BF16),
        jax.ShapeDtypeStruct((n_total, 4 * GLA_QK_W + GLA_V_W), F32),
        jax.ShapeDtypeStruct((n_total, GLA_V_W), F32),
    ]
    out_specs = [
        pl.BlockSpec((1, MLA_HEADS, tm, MLA_QK), head_map),
        pl.BlockSpec((1, MLA_HEADS, tm, MLA_QK), head_map),
        pl.BlockSpec((1, MLA_HEADS, tm, MLA_V_AUG), head_map),
        pl.BlockSpec((tm, SGU_WIDTH), tok_map),
        pl.BlockSpec((tm, 4 * GLA_QK_W + GLA_V_W), tok_map),
        pl.BlockSpec((tm, GLA_V_W), tok_map),
    ]
    return pl.pallas_call(
        _inproj_kernel,
        grid=(n_total // tm,),
        in_specs=in_specs,
        out_specs=out_specs,
        out_shape=out_shape,
        compiler_params=_cparams(("arbitrary",), VMEM_LIMIT_INPROJ),
        name="inproj",
    )(x2, mod_rows, lw["n1"], lw["w_in"], lw["q_norm"], lw["w_uq"], lw["kv_norm"], lw["w_ukv"],
      lw["gq_nope"], lw["gk_nope"], tabs["cq"], tabs["sq"], tabs["tk"],
      lw["sgu_norm"], lw["sgu_w"], lw["sgu_b"], lw["wg"], lw["bg"])


_GLA_LEVELS = (32, 16, 8, 4, 2, 1)
_N_LEV = len(_GLA_LEVELS)
_G_SPLIT = 3
_R_QI = _N_LEV * GLA_CHUNK
_R_KS = _R_QI + GLA_CHUNK
_R_TOT = _R_KS + GLA_CHUNK
_M_ROWS = _R_TOT + 8


def _gla_constants():
    c = GLA_CHUNK
    t = np.arange(c)[:, None]
    j = np.arange(c)[None, :]
    lev, masks = [], []
    for m in _GLA_LEVELS:
        upper = (t // m) % 2 == 1
        bound = (t // (2 * m)) * 2 * m + m - 1
        lev.append(np.where(upper, (j > bound) & (j <= t), (j > t) & (j <= bound)).astype(np.float32))
        s = j
        masks.append(((t // (2 * m) == s // (2 * m)) & upper & ((s // m) % 2 == 0)).astype(np.float32))
    masks.append((t == j).astype(np.float32))
    mqi = (j <= t).astype(np.float32)
    mks = (j > t).astype(np.float32)
    tot = np.ones((8, c), np.float32)
    flip = lambda a: a[::-1, ::-1]
    m_f = np.concatenate(lev + [mqi, mks, tot], axis=0)
    m_b = np.concatenate([flip(a) for a in lev] + [flip(mqi), flip(mks), tot], axis=0)
    mask_f = np.stack([np.tile(a, (GLA_HEADS, 1)) for a in masks])
    mask_b = np.stack([np.tile(flip(a), (GLA_HEADS, 1)) for a in masks])
    m_all = np.tile(np.stack([m_f, m_b]), (1, 1, _G_SPLIT))
    return m_all, np.stack([mask_f, mask_b])


def _stack_heads(a, lane_head):
    return jnp.concatenate([jnp.where(lane_head == hh, a, jnp.zeros_like(a)) for hh in range(GLA_HEADS)], axis=0)


def _gla_chunk(q, k, g, v, st, mall, masks, lane_head):
    c = GLA_CHUNK
    g_hi = g.astype(BF16)
    r1 = g - g_hi.astype(F32)
    g_mid = r1.astype(BF16)
    g_lo = (r1 - g_mid.astype(F32)).astype(BF16)
    e = _dot(mall, jnp.concatenate([g_hi, g_mid, g_lo], axis=0))
    a = masks[_N_LEV] * _dot_nt(_stack_heads(q.astype(BF16), lane_head), k.astype(BF16))
    for lev in range(_N_LEV):
        ex = jnp.exp(e[lev * c:(lev + 1) * c])
        a = a + masks[lev] * _dot_nt(_stack_heads((q * ex).astype(BF16), lane_head), (k * ex).astype(BF16))
    qi = (q * jnp.exp(e[_R_QI:_R_QI + c])).astype(BF16)
    o_inter = _dot_nt(_stack_heads(qi, lane_head), st.astype(BF16))
    vb = v.astype(BF16)
    a16 = a.astype(BF16)
    outs = []
    for hh in range(GLA_HEADS):
        rows = slice(hh * c, (hh + 1) * c)
        outs.append(o_inter[rows] + _dot(a16[rows], vb[:, hh * GLA_DV:(hh + 1) * GLA_DV]))
    o = jnp.concatenate(outs, axis=-1)
    kd = (k * jnp.exp(e[_R_KS:_R_KS + c])).astype(BF16)
    upd = _dot_tn(vb, kd)
    st_new = st * jnp.exp(e[_R_TOT:_R_TOT + 1])
    for hh in range(GLA_HEADS):
        st_new = st_new + jnp.where(lane_head == hh, upd[hh * GLA_DV:(hh + 1) * GLA_DV], 0.0)
    return o, st_new


def _gla_kernel(fwd_ref, bwd_ref, s0_ref, mall_ref, mask_ref, of_ref, ob_ref, sfin_ref, st_ref):
    step = pl.program_id(0)
    n_batch = fwd_ref.shape[0]

    @pl.when(step == 0)
    def _():
        st_ref[...] = s0_ref[...]

    lane_head = lax.broadcasted_iota(jnp.int32, (1, GLA_QK_W), 1) // GLA_DK
    w = GLA_QK_W
    for b in range(n_batch):
        for direction, (in_ref, o_ref) in enumerate(((fwd_ref, of_ref), (bwd_ref, ob_ref))):
            mall = mall_ref[direction]
            masks = [mask_ref[direction, i] for i in range(_N_LEV + 1)]
            q = in_ref[b, :, 0:w]
            k = in_ref[b, :, w:2 * w]
            g = in_ref[b, :, (2 + direction) * w:(3 + direction) * w]
            v = in_ref[b, :, 4 * w:4 * w + GLA_V_W]
            o, st_new = _gla_chunk(q, k, g, v, st_ref[direction, b], mall, masks, lane_head)
            o_ref[b] = o
            st_ref[direction, b] = st_new

    @pl.when(step == pl.num_programs(0) - 1)
    def _():
        sfin_ref[...] = st_ref[...]


def _gla(gla_in, s0, consts):
    n_batch, n_tok, width = gla_in.shape
    nc = n_tok // GLA_CHUNK
    mall, masks = consts
    st_shape = (2, n_batch, GLA_DV, GLA_QK_W)
    return pl.pallas_call(
        _gla_kernel,
        grid=(nc,),
        in_specs=[
            pl.BlockSpec((n_batch, GLA_CHUNK, width), lambda c: (0, c, 0)),
            pl.BlockSpec((n_batch, GLA_CHUNK, width), lambda c: (0, nc - 1 - c, 0)),
            pl.BlockSpec(st_shape, lambda c: (0, 0, 0, 0)),
            pl.BlockSpec(mall.shape, lambda c: (0, 0, 0)),
            pl.BlockSpec(masks.shape, lambda c: (0, 0, 0, 0)),
        ],
        out_specs=[
            pl.BlockSpec((n_batch, GLA_CHUNK, GLA_V_W), lambda c: (0, c, 0)),
            pl.BlockSpec((n_batch, GLA_CHUNK, GLA_V_W), lambda c: (0, nc - 1 - c, 0)),
            pl.BlockSpec(st_shape, lambda c: (0, 0, 0, 0)),
        ],
        out_shape=[
            jax.ShapeDtypeStruct((n_batch, n_tok, GLA_V_W), F32),
            jax.ShapeDtypeStruct((n_batch, n_tok, GLA_V_W), F32),
            jax.ShapeDtypeStruct(st_shape, F32),
        ],
        scratch_shapes=[pltpu.VMEM(st_shape, F32)],
        compiler_params=_cparams(("arbitrary",)),
        name="gla",
    )(gla_in, gla_in, s0, mall, masks)


def _attn_kernel(*refs, n_seg):
    q_ref = refs[0]
    k_refs = refs[1:1 + n_seg]
    v_refs = refs[1 + n_seg:1 + 2 * n_seg]
    o_ref = refs[1 + 2 * n_seg]
    q = q_ref[0, 0]
    n_main = k_refs[0].shape[2]
    tk = min(TK_ATTN, n_main)
    groups = [[(0, slice(c * tk, (c + 1) * tk))] for c in range(n_main // tk)]
    groups[0] += [(i, slice(0, k_refs[i].shape[2])) for i in range(1, n_seg)]
    m = acc = None
    for group in groups:
        scores = [_dot_nt(q, k_refs[i][0, 0, keys, :]) for i, keys in group]
        m_g = functools.reduce(jnp.maximum, [jnp.max(s, axis=-1, keepdims=True) for s in scores])
        m_new = m_g if m is None else jnp.maximum(m, m_g)
        acc_g = functools.reduce(jnp.add, [_dot(jnp.exp2(s - m_new).astype(BF16), v_refs[i][0, 0, keys, :])
                                           for s, (i, keys) in zip(scores, group)])
        acc = acc_g if m is None else jnp.exp2(m - m_new) * acc + acc_g
        m = m_new
    o_ref[0] = (acc[:, 0:MLA_V] / acc[:, MLA_V:MLA_V_AUG]).astype(BF16)


def _attention(q, ks, vs):
    n_batch, n_heads, nq, _ = q.shape
    tq = min(TQ_ATTN, nq)
    n_seg = len(ks)
    in_specs = [pl.BlockSpec((1, 1, tq, MLA_QK), lambda b, h, i: (b, h, i, 0))]
    for k in ks:
        in_specs.append(pl.BlockSpec((1, 1, k.shape[2], MLA_QK), lambda b, h, i: (b, h, 0, 0)))
    for v in vs:
        in_specs.append(pl.BlockSpec((1, 1, v.shape[2], MLA_V_AUG), lambda b, h, i: (b, h, 0, 0)))
    return pl.pallas_call(
        functools.partial(_attn_kernel, n_seg=n_seg),
        grid=(n_batch, n_heads, nq // tq),
        in_specs=in_specs,
        out_specs=pl.BlockSpec((1, tq, MLA_V), lambda b, h, i: (b, i, h)),
        out_shape=jax.ShapeDtypeStruct((n_batch, nq, n_heads * MLA_V), BF16),
        compiler_params=_cparams(("arbitrary", "arbitrary", "arbitrary")),
        name="attn",
    )(q, *ks, *vs)


def _outproj_kernel(a_ref, b_ref, of_ref, ob_ref, r_ref, x_ref, mod_ref, gon_ref, n2_ref, wout_ref,
                    wr_hi_ref, wr_lo_ref, br_ref, lst_ref, sl_ref,
                    x1_out, h2_out, meta_out, cnt_out):
    d = D_MODEL
    tm = x_ref.shape[0]
    gate1 = mod_ref[0, :, 2 * d:3 * d]
    shift2 = mod_ref[0, :, 3 * d:4 * d]
    scale2 = mod_ref[0, :, 4 * d:5 * d]

    og = of_ref[...] + ob_ref[...]
    gon = gon_ref[...]
    parts = []
    for hh in range(GLA_HEADS):
        seg = og[:, hh * GLA_DV:(hh + 1) * GLA_DV]
        parts.append(seg * lax.rsqrt(jnp.mean(seg * seg, axis=-1, keepdims=True) + EPS) * gon)
    cg = jnp.concatenate(parts, axis=-1) * _silu(r_ref[...])
    na = MLA_HEADS * MLA_V
    y = _dot(a_ref[...], wout_ref[0:na, :])
    y = y + _dot(b_ref[...], wout_ref[na:na + SGU_WIDTH, :])
    y = y + _dot(cg.astype(BF16), wout_ref[na + SGU_WIDTH:na + SGU_WIDTH + GLA_V_W, :])
    x1 = x_ref[...] + gate1 * y
    x1_out[...] = x1

    ms = jnp.mean(x1 * x1, axis=-1, keepdims=True)
    h2 = (x1 * lax.rsqrt(ms + EPS) * n2_ref[...]) * (1.0 + scale2) + shift2
    h_hi = h2.astype(BF16)
    h2_out[...] = h_hi
    h_lo = (h2 - h_hi.astype(F32)).astype(BF16)

    wr_hi = wr_hi_ref[...]
    lt2 = _dot_nt(jnp.concatenate([wr_hi, wr_lo_ref[...]], axis=0), h_hi)
    lt = lt2[0:32] + lt2[32:64] + _dot_nt(wr_hi, h_lo) + br_ref[:, 0:1]
    grp = [lt[MOE_EXPERTS + g:MOE_EXPERTS + g + 1, :] for g in range(MOE_GROUPS)]
    gmax = functools.reduce(jnp.maximum, grp)
    gsel = jnp.full(gmax.shape, MOE_GROUPS - 1, jnp.int32)
    for g in range(MOE_GROUPS - 2, -1, -1):
        gsel = jnp.where(grp[g] == gmax, g, gsel)
    gsum = functools.reduce(jnp.add, [jnp.exp(gl - gmax) for gl in grp])
    g_w = 1.0 / gsum
    e_in = []
    for j in range(MOE_PER_GROUP):
        ej = lt[j:j + 1, :]
        for g in range(1, MOE_GROUPS):
            ej = jnp.where(gsel == g, lt[g * MOE_PER_GROUP + j:g * MOE_PER_GROUP + j + 1, :], ej)
        e_in.append(ej)
    m1 = functools.reduce(jnp.maximum, e_in)
    i1 = jnp.full(m1.shape, MOE_PER_GROUP - 1, jnp.int32)
    for j in range(MOE_PER_GROUP - 2, -1, -1):
        i1 = jnp.where(e_in[j] == m1, j, i1)
    e_rest = [jnp.where(i1 == j, -jnp.inf, e_in[j]) for j in range(MOE_PER_GROUP)]
    m2 = functools.reduce(jnp.maximum, e_rest)
    i2 = jnp.full(m2.shape, MOE_PER_GROUP - 1, jnp.int32)
    for j in range(MOE_PER_GROUP - 2, -1, -1):
        i2 = jnp.where(e_rest[j] == m2, j, i2)
    p2 = jnp.exp(m2 - m1)
    w1 = g_w / (1.0 + p2)
    w2 = g_w * p2 / (1.0 + p2)
    eid1 = gsel * MOE_PER_GROUP + i1
    eid2 = gsel * MOE_PER_GROUP + i2

    e_iota = lax.broadcasted_iota(jnp.int32, (MOE_EXPERTS, tm), 0)
    hit1 = e_iota == eid1
    hit2 = e_iota == eid2
    onehot = jnp.where(hit1, 1.0, 0.0) + jnp.where(hit2, 1.0, 0.0)
    rank = _dot(onehot.astype(BF16), lst_ref[...])
    counts = jnp.sum(onehot, axis=1, keepdims=True)
    padded = jnp.ceil(counts * (1.0 / BF16_ROWS)) * BF16_ROWS
    padded_l = jnp.broadcast_to(padded, (MOE_EXPERTS, LANES))
    off = _dot(sl_ref[...], padded_l.astype(BF16))
    pos_e = off[:, 0:1] + rank
    pos1 = jnp.sum(jnp.where(hit1, pos_e, 0.0), axis=0, keepdims=True)
    pos2 = jnp.sum(jnp.where(hit2, pos_e, 0.0), axis=0, keepdims=True)
    zero = jnp.zeros_like(pos1)
    meta_out[0] = jnp.concatenate(
        [pos1, pos2, w1, w2, eid1.astype(F32), eid2.astype(F32), zero, zero], axis=0)
    cnt_out[0] = padded_l


def _outproj(a, b, o_f, o_b, r, x2, n_batch, mod_rows, mod_row0, lw, rconst):
    n_total = x2.shape[0]
    n_tok = n_total // n_batch
    tm = T_LOC
    tpb = n_tok // tm
    if mod_row0 == 0:
        mod_map = lambda i: (i // tpb, 0, 0)
    else:
        mod_map = lambda i: (mod_row0, 0, 0)
    const2 = lambda i: (0, 0)
    tok_map = lambda i: (i, 0)
    n_tiles = n_total // tm
    in_specs = [
        pl.BlockSpec((tm, MLA_HEADS * MLA_V), tok_map),
        pl.BlockSpec((tm, SGU_WIDTH), tok_map),
        pl.BlockSpec((tm, GLA_V_W), tok_map),
        pl.BlockSpec((tm, GLA_V_W), tok_map),
        pl.BlockSpec((tm, GLA_V_W), tok_map),
        pl.BlockSpec((tm, D_MODEL), tok_map),
        pl.BlockSpec((1, 1, 6 * D_MODEL), mod_map),
        pl.BlockSpec((1, GLA_DV), const2),
        pl.BlockSpec((1, D_MODEL), const2),
        pl.BlockSpec((D_MODEL, D_MODEL), const2, pipeline_mode=pl.Buffered(1)),
        pl.BlockSpec((32, D_MODEL), const2),
        pl.BlockSpec((32, D_MODEL), const2),
        pl.BlockSpec((32, LANES), const2),
        pl.BlockSpec((tm, tm), const2),
        pl.BlockSpec((MOE_EXPERTS, MOE_EXPERTS), const2),
    ]
    out_shape = [
        jax.ShapeDtypeStruct((n_total, D_MODEL), F32),
        jax.ShapeDtypeStruct((n_total, D_MODEL), BF16),
        jax.ShapeDtypeStruct((n_tiles, 8, tm), F32),
        jax.ShapeDtypeStruct((n_tiles, MOE_EXPERTS, LANES), F32),
    ]
    out_specs = [
        pl.BlockSpec((tm, D_MODEL), tok_map),
        pl.BlockSpec((tm, D_MODEL), tok_map),
        pl.BlockSpec((1, 8, tm), lambda i: (i, 0, 0)),
        pl.BlockSpec((1, MOE_EXPERTS, LANES), lambda i: (i, 0, 0)),
    ]
    return pl.pallas_call(
        _outproj_kernel,
        grid=(n_tiles,),
        in_specs=in_specs,
        out_specs=out_specs,
        out_shape=out_shape,
        compiler_params=_cparams(("arbitrary",)),
        name="outproj",
    )(a, b, o_f, o_b, r, x2, mod_rows, lw["gla_out_norm"], lw["n2"], lw["w_out"],
      lw["wr_hi"], lw["wr_lo"], lw["br"], rconst["lst"], rconst["sl"])


def _piece_copies(src_ref, dst_ref, sem, src_row, dst_row, n_rows):
    copies = []
    for size in _PIECE_SIZES:
        done = n_rows & ~(2 * size - 1)
        pred = (n_rows & size) != 0
        s0 = pl.multiple_of(src_row + done, BF16_ROWS)
        d0 = pl.multiple_of(dst_row + done, BF16_ROWS)
        cp = pltpu.make_async_copy(src_ref.at[pl.ds(s0, size)], dst_ref.at[pl.ds(d0, size)], sem)
        copies.append((pred, cp))
    return copies


def _list_base(tile, s):
    return tile * len(_PIECE_SIZES) + s


def _start_listed_pieces(src_ref, dst_ref, sem, src_tab, dst_tab, n_tab, tile):
    for s, size in enumerate(_PIECE_SIZES):
        base = _list_base(tile, s)

        def body(i, carry, size=size, base=base):
            idx = base * MOE_EXPERTS + i
            s0 = pl.multiple_of(src_tab[idx], BF16_ROWS)
            d0 = pl.multiple_of(dst_tab[idx], BF16_ROWS)
            pltpu.make_async_copy(src_ref.at[pl.ds(s0, size)], dst_ref.at[pl.ds(d0, size)], sem).start()
            return carry

        lax.fori_loop(0, n_tab[base], body, 0)


def _wait_listed_pieces(src_ref, dst_ref, sem, n_tab, tile):
    for s, size in enumerate(_PIECE_SIZES):
        def body(i, carry, size=size):
            pltpu.make_async_copy(src_ref.at[pl.ds(0, size)], dst_ref.at[pl.ds(0, size)], sem).wait()
            return carry

        lax.fori_loop(0, n_tab[_list_base(tile, s)], body, 0)


def _stream_starts(stream_tiles):
    return [sum(stream_tiles[:s]) for s in range(len(stream_tiles))]


def _stream_block(i, start, n):
    return jnp.clip(i - start, 0, n - 1)


def _dispatch_kernel(ploc_tab, pdst_tab, pn_tab, ztab_dst, ztab_cnt, n_used, *refs, stream_tiles):
    n_streams = len(stream_tiles)
    h2_refs = refs[:n_streams]
    meta_ref, xs_ref, xloc_ref, zeros_ref, sems = refs[n_streams:]
    j = pl.program_id(0)
    n_tiles = pl.num_programs(0)
    tm = h2_refs[0].shape[0]
    slot = j % 2

    @pl.when(j == 0)
    def _():
        zeros_ref[...] = jnp.zeros_like(zeros_ref)

        def fill(t, carry):
            cp = pltpu.make_async_copy(zeros_ref, xs_ref.at[pl.ds(pl.multiple_of(t * TM_FFN, TM_FFN), TM_FFN)],
                                       sems.at[0])
            cp.start()
            cp.wait()
            return carry
        lax.fori_loop(n_used[0], xs_ref.shape[0] // TM_FFN, fill, 0)
        zcopies = []
        for e in range(MOE_EXPERTS):
            zcopies += _piece_copies(zeros_ref, xs_ref, sems.at[0], 0, ztab_dst[e], ztab_cnt[e])
        for pred, cp in zcopies:
            @pl.when(pred)
            def _(cp=cp):
                cp.start()
        for pred, cp in zcopies:
            @pl.when(pred)
            def _(cp=cp):
                cp.wait()

    pos1 = meta_ref[0, 0:1, :]
    pos2 = meta_ref[0, 1:2, :]
    s_iota = lax.broadcasted_iota(jnp.int32, (SLOTS, tm), 0).astype(F32)
    perm = jnp.where((s_iota == pos1) | (s_iota == pos2), 1.0, 0.0).astype(BF16)
    h2 = h2_refs[0][...]
    for s, start in list(enumerate(_stream_starts(stream_tiles)))[1:]:
        h2 = jnp.where(j >= start, h2_refs[s][...], h2)
    xloc_ref[slot] = _dot(perm, h2).astype(BF16)

    @pl.when(j > 0)
    def _():
        _wait_listed_pieces(xloc_ref.at[1 - slot], xs_ref, sems.at[1 - slot], pn_tab, j - 1)

    _start_listed_pieces(xloc_ref.at[slot], xs_ref, sems.at[slot], ploc_tab, pdst_tab, pn_tab, j)

    @pl.when(j == n_tiles - 1)
    def _():
        _wait_listed_pieces(xloc_ref.at[slot], xs_ref, sems.at[slot], pn_tab, j)


def _dispatch(h2s, meta, tabs, r_max):
    n_tiles = meta.shape[0]
    tm = T_LOC
    stream_tiles = tuple(h.shape[0] // tm for h in h2s)
    assert sum(stream_tiles) == n_tiles
    h2_specs = [pl.BlockSpec((tm, D_MODEL), lambda i, *_, start=start, n=n: (_stream_block(i, start, n), 0))
                for start, n in zip(_stream_starts(stream_tiles), stream_tiles)]
    grid_spec = pltpu.PrefetchScalarGridSpec(
        num_scalar_prefetch=6,
        grid=(n_tiles,),
        in_specs=h2_specs + [pl.BlockSpec((1, 8, tm), lambda i, *_: (i, 0, 0))],
        out_specs=pl.BlockSpec(memory_space=pl.ANY),
        scratch_shapes=[pltpu.VMEM((2, SLOTS, D_MODEL), BF16), pltpu.VMEM((TM_FFN, D_MODEL), BF16),
                        pltpu.SemaphoreType.DMA((2,))],
    )
    return pl.pallas_call(
        functools.partial(_dispatch_kernel, stream_tiles=stream_tiles),
        grid_spec=grid_spec,
        out_shape=jax.ShapeDtypeStruct((r_max, D_MODEL), BF16),
        compiler_params=_cparams(("arbitrary",)),
        name="dispatch",
    )(tabs["ploc"], tabs["pdst"], tabs["pn"], tabs["zdst"], tabs["zcnt"], tabs["n_used"], *h2s, meta)


def _ffn_kernel(tile_e, tile_blk, n_used, x_ref, w1_ref, w3_ref, w2_ref, y_ref):
    i = pl.program_id(0)

    @pl.when(i < n_used[0])
    def _():
        x = x_ref[...]
        acc = None
        for f in range(D_EXPERT // F_CHUNK):
            cols = slice(f * F_CHUNK, (f + 1) * F_CHUNK)
            h1 = _dot(x, w1_ref[0, 0, :, cols])
            h3 = _dot(x, w3_ref[0, 0, :, cols])
            act = (_silu(h1) * h3).astype(BF16)
            part = _dot(act, w2_ref[0, 0, cols, :])
            acc = part if acc is None else acc + part
        y_ref[...] = acc.astype(BF16)

    @pl.when(i >= n_used[0])
    def _():
        y_ref[...] = jnp.zeros_like(y_ref)


def _ffn(xs, layer, w1, w3, w2, tabs):
    r_max = xs.shape[0]
    n_mt = r_max // TM_FFN
    grid_spec = pltpu.PrefetchScalarGridSpec(
        num_scalar_prefetch=3,
        grid=(n_mt,),
        in_specs=[
            pl.BlockSpec((TM_FFN, D_MODEL), lambda i, te, tb, nu: (tb[i], 0)),
            pl.BlockSpec((1, 1, D_MODEL, D_EXPERT), lambda i, te, tb, nu: (layer, te[i], 0, 0)),
            pl.BlockSpec((1, 1, D_MODEL, D_EXPERT), lambda i, te, tb, nu: (layer, te[i], 0, 0)),
            pl.BlockSpec((1, 1, D_EXPERT, D_MODEL), lambda i, te, tb, nu: (layer, te[i], 0, 0)),
        ],
        out_specs=pl.BlockSpec((TM_FFN, D_MODEL), lambda i, te, tb, nu: (i, 0)),
    )
    return pl.pallas_call(
        _ffn_kernel,
        grid_spec=grid_spec,
        out_shape=jax.ShapeDtypeStruct((r_max, D_MODEL), BF16),
        compiler_params=_cparams(("arbitrary",)),
        name="ffn",
    )(tabs["tile_e"], tabs["tile_blk"], tabs["n_used"], xs, w1, w3, w2)


def _combine_kernel(ploc_tab, pdst_tab, pn_tab, y_ref, metac_ref, mod_ref, *refs, stream_tiles):
    n_streams = len(stream_tiles)
    x1_refs = refs[:n_streams]
    out_refs = refs[n_streams:2 * n_streams]
    yloc_ref, sems = refs[2 * n_streams:]
    j = pl.program_id(0)
    n_tiles = pl.num_programs(0)
    d = D_MODEL
    tm = x1_refs[0].shape[0]
    slot = j % 2

    @pl.when(j == 0)
    def _():
        yloc_ref[...] = jnp.zeros_like(yloc_ref)
        _start_listed_pieces(y_ref, yloc_ref.at[0], sems.at[0], pdst_tab, ploc_tab, pn_tab, 0)

    @pl.when(j + 1 < n_tiles)
    def _():
        _start_listed_pieces(y_ref, yloc_ref.at[1 - slot], sems.at[1 - slot], pdst_tab, ploc_tab, pn_tab, j + 1)

    _wait_listed_pieces(y_ref, yloc_ref.at[slot], sems.at[slot], pn_tab, j)

    mc = metac_ref[0]
    s_iota = lax.broadcasted_iota(jnp.int32, (tm, SLOTS), 1).astype(F32)
    wc = jnp.where(s_iota == mc[:, 0:1], mc[:, 2:3], 0.0) + jnp.where(s_iota == mc[:, 1:2], mc[:, 3:4], 0.0)
    moe = _dot(wc.astype(BF16), yloc_ref[slot])
    gate2 = mod_ref[0, :, 5 * d:6 * d]
    for start, n, x1_ref, out_ref in zip(_stream_starts(stream_tiles), stream_tiles, x1_refs, out_refs):
        @pl.when((j >= start) & (j < start + n))
        def _(x1_ref=x1_ref, out_ref=out_ref):
            out_ref[...] = x1_ref[...] + gate2 * moe


def _combine(y, metac, x1s, n_batch, mod_rows, mod_row0s, tabs):
    tm = T_LOC
    stream_tiles = tuple(x1.shape[0] // tm for x1 in x1s)
    starts = _stream_starts(stream_tiles)
    n_tiles = sum(stream_tiles)

    def mod_map(i, *_):
        row = jnp.int32(0)
        for start, n, x1, row0 in zip(starts, stream_tiles, x1s, mod_row0s):
            tpb = x1.shape[0] // n_batch // tm
            row_s = _stream_block(i, start, n) // tpb if row0 == 0 else row0
            row = jnp.where(i >= start, row_s, row)
        return (row, 0, 0)

    tile_specs = [pl.BlockSpec((tm, D_MODEL), lambda i, *_, start=start, n=n: (_stream_block(i, start, n), 0))
                  for start, n in zip(starts, stream_tiles)]
    grid_spec = pltpu.PrefetchScalarGridSpec(
        num_scalar_prefetch=3,
        grid=(n_tiles,),
        in_specs=[
            pl.BlockSpec(memory_space=pl.ANY),
            pl.BlockSpec((1, tm, 8), lambda i, *_: (i, 0, 0)),
            pl.BlockSpec((1, 1, 6 * D_MODEL), mod_map),
        ] + tile_specs,
        out_specs=tile_specs,
        scratch_shapes=[pltpu.VMEM((2, SLOTS, D_MODEL), BF16), pltpu.SemaphoreType.DMA((2,))],
    )
    return pl.pallas_call(
        functools.partial(_combine_kernel, stream_tiles=stream_tiles),
        grid_spec=grid_spec,
        out_shape=[jax.ShapeDtypeStruct(x1.shape, F32) for x1 in x1s],
        compiler_params=_cparams(("arbitrary",)),
        name="combine",
    )(tabs["ploc"], tabs["pdst"], tabs["pn"], y, metac, mod_rows, *x1s)


def _rope_tables(n_tokens):
    rows = n_tokens // GRID_W
    row = jnp.repeat(jnp.arange(rows, dtype=jnp.int32), GRID_W)
    col = jnp.tile(jnp.arange(GRID_W, dtype=jnp.int32), rows)
    n_freq = MLA_ROPE // 4
    inv_freq = ROPE_THETA ** (-jnp.arange(n_freq, dtype=F32) / n_freq)
    ang_r = row[:, None].astype(F32) * inv_freq
    ang_c = col[:, None].astype(F32) * inv_freq
    cos = jnp.concatenate([jnp.cos(ang_r)] * 2 + [jnp.cos(ang_c)] * 2, axis=-1)
    sin = jnp.concatenate([-jnp.sin(ang_r), jnp.sin(ang_r), -jnp.sin(ang_c), jnp.sin(ang_c)], axis=-1)
    return cos, sin


_SWAP64 = np.concatenate([np.arange(16, 32), np.arange(0, 16), np.arange(48, 64), np.arange(32, 48)])


def _position_tables(cos, sin, q_gain, k_gain):
    gq = q_gain[MLA_NOPE:]
    gk = k_gain[MLA_NOPE:]
    cq = jnp.tile(cos * gq, (1, MLA_HEADS))
    sq = jnp.tile(sin * gq[_SWAP64], (1, MLA_HEADS))
    tk = jnp.concatenate([cos * gk, sin * gk[_SWAP64]], axis=-1)
    return {"cq": cq, "sq": sq, "tk": tk}


def _layer_weights(l, p):
    d = D_MODEL
    o = _IN_OFF
    w_in = p["w_in"][l]
    seg = lambda i: w_in[:, o[i]:o[i + 1]]
    kpe = seg(2)
    w_in_r = jnp.concatenate(
        [seg(0), seg(1), seg(3), seg(4), seg(5) * (GLA_DK ** -0.5), seg(6), seg(7), seg(8),
         kpe, kpe[:, _SWAP64], seg(9), seg(10), jnp.zeros((d, Z_WIDTH - Z_GG - 2 * GLA_GATE_RANK), F32)],
        axis=-1).astype(BF16)
    w_uq = p["mla_w_uq"][l].reshape(MLA_Q_RANK, MLA_HEADS, MLA_QK)
    q_nope = w_uq[:, :, :MLA_NOPE].reshape(MLA_Q_RANK, -1)
    q_pe = w_uq[:, :, MLA_NOPE:]
    w_uq_r = jnp.concatenate([q_nope, q_pe.reshape(MLA_Q_RANK, -1), q_pe[:, :, _SWAP64].reshape(MLA_Q_RANK, -1)],
                             axis=-1).astype(BF16)
    w_ukv = p["mla_w_ukv"][l].reshape(MLA_KV_RANK, MLA_HEADS, MLA_NOPE + MLA_V)
    w_ukv_r = jnp.concatenate([w_ukv[:, :, :MLA_NOPE].reshape(MLA_KV_RANK, -1),
                               w_ukv[:, :, MLA_NOPE:].reshape(MLA_KV_RANK, -1)], axis=-1).astype(BF16)
    wg = jnp.zeros((LANES, 2 * GLA_QK_W), F32)
    wg = wg.at[0:GLA_GATE_RANK, 0:GLA_QK_W].set(p["gla_wg_f"][l])
    wg = wg.at[GLA_GATE_RANK:2 * GLA_GATE_RANK, GLA_QK_W:].set(p["gla_wg_b"][l])
    bg = jnp.concatenate([p["gla_bg_f"][l], p["gla_bg_b"][l]])[None, :]
    sgu_b = jnp.repeat(p["sgu_b"][l].T, SGU_GROUP_DIM, axis=1)
    wr = jnp.zeros((32, d), F32)
    wr = wr.at[0:MOE_EXPERTS].set(p["moe_w_expert"][l].T)
    wr = wr.at[MOE_EXPERTS:MOE_EXPERTS + MOE_GROUPS].set(p["moe_w_group"][l].T)
    wr_hi = wr.astype(BF16)
    wr_lo = (wr - wr_hi.astype(F32)).astype(BF16)
    br = jnp.zeros((32,), F32).at[0:MOE_EXPERTS].set(p["moe_b_expert"][l])
    br = br.at[MOE_EXPERTS:MOE_EXPERTS + MOE_GROUPS].set(p["moe_b_group"][l])
    return {
        "n1": p["norm1_g"][l][None, :], "n2": p["norm2_g"][l][None, :],
        "w_in": w_in_r, "q_norm": p["mla_q_norm"][l][None, :], "kv_norm": p["mla_kv_norm"][l][None, :],
        "w_uq": w_uq_r, "w_ukv": w_ukv_r,
        "gq_nope": p["mla_q_gain"][l][None, :MLA_NOPE], "gk_nope": p["mla_k_gain"][l][None, :MLA_NOPE],
        "sgu_norm": p["sgu_norm"][l].reshape(1, SGU_WIDTH), "sgu_w": p["sgu_w"][l].astype(BF16), "sgu_b": sgu_b,
        "wg": wg.astype(BF16), "bg": bg,
        "gla_out_norm": p["gla_out_norm"][l][None, :], "w_out": p["w_out"][l].astype(BF16),
        "wr_hi": wr_hi, "wr_lo": wr_lo, "br": jnp.broadcast_to(br[:, None], (32, LANES)),
        "layer": l, "w1": p["moe_w1_bf16"], "w3": p["moe_w3_bf16"], "w2": p["moe_w2_bf16"],
    }


def _router_constants():
    t = np.arange(T_LOC)
    lst = (t[:, None] < t[None, :]).astype(np.float32)
    e = np.arange(MOE_EXPERTS)
    sl = (e[None, :] < e[:, None]).astype(np.float32)
    return {"lst": jnp.asarray(lst, BF16), "sl": jnp.asarray(sl, BF16)}


def _dispatch_tables(cnt, r_max):
    n_tiles = cnt.shape[0]
    cnt = cnt.astype(jnp.int32)
    loc_off = jnp.cumsum(cnt, axis=1) - cnt
    rows_e = jnp.sum(cnt, axis=0)
    tiles_e = (rows_e + TM_FFN - 1) // TM_FFN
    tile_start = jnp.cumsum(tiles_e) - tiles_e
    base_e = tile_start * TM_FFN
    dst = base_e[None, :] + jnp.cumsum(cnt, axis=0) - cnt
    n_used = jnp.sum(tiles_e)
    n_mt = r_max // TM_FFN
    ti = jnp.arange(n_mt, dtype=jnp.int32)
    tile_blk = jnp.minimum(ti, n_used - 1)
    tile_e = jnp.sum(tile_blk[:, None] >= (tile_start + tiles_e)[None, :], axis=1).astype(jnp.int32)
    tile_e = jnp.minimum(tile_e, MOE_EXPERTS - 1)
    zdst = base_e + rows_e
    zcnt = tiles_e * TM_FFN - rows_e
    i32 = lambda a: a.astype(jnp.int32)
    sizes = jnp.asarray(_PIECE_SIZES, jnp.int32)
    has = (cnt[:, :, None] & sizes) != 0
    done = cnt[:, :, None] & ~(2 * sizes - 1)
    pos = jnp.cumsum(has, axis=1) - 1
    pick = has[..., None] & (pos[..., None] == jnp.arange(MOE_EXPERTS))
    listed = lambda a: jnp.sum(jnp.where(pick, (a[:, :, None] + done)[..., None], 0), axis=1)
    return {"ploc": i32(listed(loc_off).reshape(-1)), "pdst": i32(listed(dst).reshape(-1)),
            "pn": i32(jnp.sum(has, axis=1).reshape(-1)), "zdst": i32(zdst), "zcnt": i32(zcnt),
            "tile_e": tile_e, "tile_blk": i32(tile_blk), "n_used": i32(n_used.reshape(1))}


def _moe(h2s, metas, cnts, x1s, n_batch, mod_rows, mod_row0s, lw):
    meta = jnp.concatenate(metas, axis=0)
    cnt = jnp.concatenate(cnts, axis=0)
    n_tiles = meta.shape[0]
    worst_rows = n_tiles * SLOTS + MOE_EXPERTS * (TM_FFN - BF16_ROWS)
    r_max = -(-worst_rows // TM_FFN) * TM_FFN
    tabs = _dispatch_tables(cnt[:, :, 0], r_max)
    xs = _dispatch(h2s, meta, tabs, r_max)
    y = _ffn(xs, lw["layer"], lw["w1"], lw["w3"], lw["w2"], tabs)
    metac = jnp.swapaxes(meta, 1, 2)
    return _combine(y, metac, x1s, n_batch, mod_rows, mod_row0s, tabs)


def kernel(x, c, ctx, c_ctx, norm1_g, norm2_g, ada_w, ada_b, w_in, mla_q_norm, mla_w_uq, mla_kv_norm, mla_w_ukv,
           mla_q_gain, mla_k_gain, sgu_norm, sgu_w, sgu_b, gla_wg_f, gla_bg_f, gla_wg_b, gla_bg_b, gla_out_norm,
           w_out, moe_w_group, moe_b_group, moe_w_expert, moe_b_expert, moe_w1, moe_w3, moe_w2):
    p = dict(norm1_g=norm1_g, norm2_g=norm2_g, w_in=w_in, mla_q_norm=mla_q_norm, mla_w_uq=mla_w_uq,
             mla_kv_norm=mla_kv_norm, mla_w_ukv=mla_w_ukv, mla_q_gain=mla_q_gain, mla_k_gain=mla_k_gain,
             sgu_norm=sgu_norm, sgu_w=sgu_w, sgu_b=sgu_b, gla_wg_f=gla_wg_f, gla_bg_f=gla_bg_f,
             gla_wg_b=gla_wg_b, gla_bg_b=gla_bg_b, gla_out_norm=gla_out_norm, w_out=w_out,
             moe_w_group=moe_w_group, moe_b_group=moe_b_group, moe_w_expert=moe_w_expert,
             moe_b_expert=moe_b_expert, moe_w1_bf16=moe_w1.astype(BF16), moe_w3_bf16=moe_w3.astype(BF16),
             moe_w2_bf16=moe_w2.astype(BF16))
    n_batch, seq, d = x.shape
    n_ctx = ctx.shape[1]
    depth = ada_w.shape[0]
    assert d == D_MODEL and n_batch <= 4 and seq % T_LOC == 0 and seq % TM_IN == 0 and n_ctx % TM_SUB == 0
    assert (n_batch * n_ctx) % T_LOC == 0

    c_rows = jnp.zeros((8, d), F32).at[0:n_batch].set(c).at[4].set(c_ctx)
    mod = _ada(c_rows, ada_w, ada_b)

    cos, sin = _rope_tables(seq)
    ones_c = jnp.ones((min(TM_IN, n_ctx), MLA_ROPE), F32)
    gla_consts_np = _gla_constants()
    gla_consts = (jnp.asarray(gla_consts_np[0], BF16), jnp.asarray(gla_consts_np[1], F32))
    rconst = _router_constants()
    zero_state = jnp.zeros((2, n_batch, GLA_DV, GLA_QK_W), F32)

    x2 = x.reshape(n_batch * seq, d)
    ctx2 = ctx.reshape(n_batch * n_ctx, d)
    for l in range(depth):
        ctx_out = l < depth - 1
        lw = _layer_weights(l, p)
        mod_rows = mod[l].reshape(8, 1, 6 * d)
        tabs_x = _position_tables(cos, sin, mla_q_gain[l], mla_k_gain[l])
        tabs_c = _position_tables(ones_c, 0.0 * ones_c, mla_q_gain[l], mla_k_gain[l])

        q_x, k_x, v_x, b_x, gla_x, r_x = _inproj(x2, n_batch, mod_rows, 0, lw, tabs_x)
        q_c, k_c, v_c, b_c, gla_c, r_c = _inproj(ctx2, n_batch, mod_rows, 4, lw, tabs_c)

        of_c, ob_c, s_ctx = _gla(gla_c.reshape(n_batch, n_ctx, -1), zero_state, gla_consts)
        of_x, ob_x, _ = _gla(gla_x.reshape(n_batch, seq, -1), s_ctx, gla_consts)
        a_x = _attention(q_x, [k_x, k_c], [v_x, v_c])

        x1, h2, meta, cnt = _outproj(a_x.reshape(n_batch * seq, -1), b_x, of_x.reshape(n_batch * seq, -1),
                                     ob_x.reshape(n_batch * seq, -1), r_x, x2, n_batch, mod_rows, 0, lw, rconst)
        if ctx_out:
            a_c = _attention(q_c, [k_c], [v_c])
            c1, h2c, meta_c, cnt_c = _outproj(a_c.reshape(n_batch * n_ctx, -1), b_c,
                                              of_c.reshape(n_batch * n_ctx, -1), ob_c.reshape(n_batch * n_ctx, -1),
                                              r_c, ctx2, n_batch, mod_rows, 4, lw, rconst)
            x2, ctx2 = _moe([h2, h2c], [meta, meta_c], [cnt, cnt_c], [x1, c1], n_batch, mod_rows, [0, 4], lw)
        else:
            x2, = _moe([h2], [meta], [cnt], [x1], n_batch, mod_rows, [0], lw)
    return x2.reshape(n_batch, seq, d)
```

```python
import functools

import numpy as np
import jax
import jax.numpy as jnp
from jax import lax
from jax.experimental import pallas as pl
from jax.experimental.pallas import tpu as pltpu

F32 = jnp.float32
BF16 = jnp.bfloat16

D_MODEL = 2048
GRID_W = 64
EPS = 1e-6

MLA_HEADS = 8
MLA_Q_RANK = 512
MLA_KV_RANK = 512
MLA_NOPE = 128
MLA_ROPE = 64
MLA_V = 128
MLA_V_AUG = 2 * MLA_V
MLA_QK = MLA_NOPE + MLA_ROPE
MLA_SCALE = MLA_QK ** -0.5
ROPE_THETA = 10000.0
LOG2_E = float(np.log2(np.e))

SGU_WIDTH = 512
SGU_GROUPS = 4
SGU_GROUP_DIM = SGU_WIDTH // SGU_GROUPS
SGU_CHUNK = 128

GLA_HEADS = 4
GLA_DK = 64
GLA_DV = 128
GLA_GATE_RANK = 16
GLA_GATE_NORM = 16.0
GLA_CHUNK = 64
GLA_QK_W = GLA_HEADS * GLA_DK
GLA_V_W = GLA_HEADS * GLA_DV

MOE_GROUPS = 4
MOE_PER_GROUP = 4
MOE_EXPERTS = MOE_GROUPS * MOE_PER_GROUP
D_EXPERT = 1024

_IN_SPLITS = (512, 512, 64, 512, 512, 256, 256, 512, 512, 16, 16)
_IN_OFF = np.concatenate([[0], np.cumsum(_IN_SPLITS)]).tolist()

Z_CQ, Z_CKV, Z_ZU, Z_ZV = 0, 512, 1024, 1536
Z_GQ, Z_GK, Z_GV, Z_GR = 2048, 2304, 2560, 3072
Z_KPE, Z_GG = 3584, 3712
Z_WIDTH = 3840

LANES = 128
BF16_ROWS = 16
VMEM_LIMIT = 56 * 1024 * 1024
VMEM_LIMIT_INPROJ = 61 * 1024 * 1024

TM_IN = 512
TM_SUB = 256
TQ_ATTN = 1024
TK_ATTN = 2048
T_LOC = 512
SLOTS = 2 * T_LOC + MOE_EXPERTS * BF16_ROWS
TM_FFN = 512
F_CHUNK = 512
F_SUB = 256
N_FCHUNK = D_EXPERT // F_CHUNK
_PIECE_SIZES = (512, 256, 128, 64, 32, 16)


def _cparams(sem, vmem=VMEM_LIMIT):
    return pltpu.CompilerParams(dimension_semantics=sem, vmem_limit_bytes=vmem)


def _silu(x):
    return x / (1.0 + jnp.exp(-x))


def _gelu_tanh(x):
    return 0.5 * x * (1.0 + jnp.tanh(np.sqrt(2.0 / np.pi).astype(np.float32) * (x + 0.044715 * (x * x * x))))


def _log_sigmoid(x):
    return jnp.minimum(x, 0.0) - jnp.log(1.0 + jnp.exp(-jnp.abs(x)))


def _dot(a, b):
    return jnp.dot(a, b, preferred_element_type=F32)


def _dot_nt(a, b):
    return lax.dot_general(a, b, (((1,), (1,)), ((), ())), preferred_element_type=F32)


def _dot_tn(a, b):
    return lax.dot_general(a, b, (((0,), (0,)), ((), ())), preferred_element_type=F32)


def _ada_kernel(c_ref, w_ref, b_ref, o_ref):
    a = _silu(c_ref[...]).astype(BF16)
    o_ref[0] = _dot(a, w_ref[0].astype(BF16)) + b_ref[0]


def _ada(c_rows, ada_w, ada_b):
    depth, d, n = ada_w.shape
    tn = 1536
    return pl.pallas_call(
        _ada_kernel,
        grid=(depth, n // tn),
        in_specs=[
            pl.BlockSpec((8, d), lambda l, j: (0, 0)),
            pl.BlockSpec((1, d, tn), lambda l, j: (l, 0, j)),
            pl.BlockSpec((1, 1, tn), lambda l, j: (l, 0, j)),
        ],
        out_specs=pl.BlockSpec((1, 8, tn), lambda l, j: (l, 0, j)),
        out_shape=jax.ShapeDtypeStruct((depth, 8, n), F32),
        compiler_params=_cparams(("arbitrary", "arbitrary")),
        name="ada",
    )(c_rows, ada_w, ada_b.reshape(depth, 1, n))


def _inproj_kernel(x_ref, mod_ref, n1_ref, win_ref, qn_ref, wuq_ref, kvn_ref, wukv_ref,
                   gqn_ref, gkn_ref, cq_ref, sq_ref, tk_ref,
                   sgn_ref, sgw_ref, sgb_ref, wg_ref, bg_ref,
                   q_out, k_out, v_out, b_out, gla_out, r_out):
    for t in range(x_ref.shape[0] // TM_SUB):
        _inproj_rows(slice(t * TM_SUB, (t + 1) * TM_SUB), x_ref, mod_ref, n1_ref, win_ref, qn_ref, wuq_ref,
                     kvn_ref, wukv_ref, gqn_ref, gkn_ref, cq_ref, sq_ref, tk_ref, sgn_ref, sgw_ref, sgb_ref,
                     wg_ref, bg_ref, q_out, k_out, v_out, b_out, gla_out, r_out)


def _inproj_rows(rs, x_ref, mod_ref, n1_ref, win_ref, qn_ref, wuq_ref, kvn_ref, wukv_ref,
                 gqn_ref, gkn_ref, cq_ref, sq_ref, tk_ref, sgn_ref, sgw_ref, sgb_ref, wg_ref, bg_ref,
                 q_out, k_out, v_out, b_out, gla_out, r_out):
    d = D_MODEL
    tm = rs.stop - rs.start
    x = x_ref[rs, :]
    shift = mod_ref[0, :, 0:d]
    scale = mod_ref[0, :, d:2 * d]
    ms = jnp.mean(x * x, axis=-1, keepdims=True)
    h = (x * lax.rsqrt(ms + EPS) * n1_ref[...]) * (1.0 + scale) + shift
    hb = h.astype(BF16)
    z_mla = _dot(hb, win_ref[:, 0:Z_ZU])
    z_end = _dot(hb, win_ref[:, Z_KPE:Z_WIDTH])
    z_mid = _dot(hb, win_ref[:, Z_ZU:Z_KPE])

    cq = z_mla[:, Z_CQ:Z_CQ + MLA_Q_RANK]
    ckv = z_mla[:, Z_CKV:Z_CKV + MLA_KV_RANK]
    cqn = cq * lax.rsqrt(jnp.mean(cq * cq, axis=-1, keepdims=True) + EPS) * qn_ref[...]
    ckvn = ckv * lax.rsqrt(jnp.mean(ckv * ckv, axis=-1, keepdims=True) + EPS) * kvn_ref[...]
    qa = _dot(cqn.astype(BF16), wuq_ref[...])
    kv = _dot(ckvn.astype(BF16), wukv_ref[...])
    npe = MLA_HEADS * MLA_NOPE
    rw = MLA_HEADS * MLA_ROPE
    q_pe = qa[:, npe:npe + rw]
    q_rot = q_pe * cq_ref[rs, :] + qa[:, npe + rw:npe + 2 * rw] * sq_ref[rs, :]
    zk = z_end[:, 0:2 * MLA_ROPE]
    tkz = zk * tk_ref[rs, :]
    k_rot = tkz[:, 0:MLA_ROPE] + tkz[:, MLA_ROPE:2 * MLA_ROPE]
    k_pe = zk[:, 0:MLA_ROPE]
    ss_kpe = jnp.sum(k_pe * k_pe, axis=-1, keepdims=True)
    gqn = gqn_ref[...]
    gkn = gkn_ref[...]
    for hh in range(MLA_HEADS):
        qn_h = qa[:, hh * MLA_NOPE:(hh + 1) * MLA_NOPE]
        qp_h = q_pe[:, hh * MLA_ROPE:(hh + 1) * MLA_ROPE]
        ssq = jnp.sum(qn_h * qn_h, axis=-1, keepdims=True) + jnp.sum(qp_h * qp_h, axis=-1, keepdims=True)
        rq = lax.rsqrt(ssq * (1.0 / MLA_QK) + EPS) * (MLA_SCALE * LOG2_E)
        q_h = jnp.concatenate([qn_h * gqn * rq, q_rot[:, hh * MLA_ROPE:(hh + 1) * MLA_ROPE] * rq], axis=-1)
        q_out[0, hh, rs, :] = q_h.astype(BF16)
        kn_h = kv[:, hh * MLA_NOPE:(hh + 1) * MLA_NOPE]
        ssk = jnp.sum(kn_h * kn_h, axis=-1, keepdims=True) + ss_kpe
        rk = lax.rsqrt(ssk * (1.0 / MLA_QK) + EPS)
        k_h = jnp.concatenate([kn_h * gkn * rk, k_rot * rk], axis=-1)
        k_out[0, hh, rs, :] = k_h.astype(BF16)
        v_h = kv[:, npe + hh * MLA_V:npe + (hh + 1) * MLA_V]
        v_out[0, hh, rs, :] = jnp.concatenate([v_h, jnp.ones_like(v_h)], axis=-1).astype(BF16)

    u = _gelu_tanh(z_mid[:, 0:SGU_WIDTH])
    vv = _gelu_tanh(z_mid[:, Z_ZV - Z_ZU:Z_ZV - Z_ZU + SGU_WIDTH])
    sgn = sgn_ref[...]
    for c in range(tm // SGU_CHUNK):
        rows = slice(c * SGU_CHUNK, (c + 1) * SGU_CHUNK)
        outs = []
        for g in range(SGU_GROUPS):
            cols = slice(g * SGU_GROUP_DIM, (g + 1) * SGU_GROUP_DIM)
            seg = vv[rows, cols]
            vn = seg * lax.rsqrt(jnp.mean(seg * seg, axis=-1, keepdims=True) + EPS) * sgn[:, cols]
            outs.append(_dot(sgw_ref[g], vn.astype(BF16)))
        sp = jnp.concatenate(outs, axis=-1) + sgb_ref[...]
        b_out[rs.start + c * SGU_CHUNK:rs.start + (c + 1) * SGU_CHUNK, :] = (u[rows, :] * sp).astype(BF16)

    gl = _dot(z_end[:, Z_GG - Z_KPE:Z_GG - Z_KPE + LANES].astype(BF16), wg_ref[...]) + bg_ref[...]
    gdec = _log_sigmoid(gl) * (1.0 / GLA_GATE_NORM)
    gla_out[rs, 0:2 * GLA_QK_W] = z_mid[:, Z_GQ - Z_ZU:Z_GQ - Z_ZU + 2 * GLA_QK_W]
    gla_out[rs, 2 * GLA_QK_W:4 * GLA_QK_W] = gdec
    gla_out[rs, 4 * GLA_QK_W:4 * GLA_QK_W + GLA_V_W] = z_mid[:, Z_GV - Z_ZU:Z_GV - Z_ZU + GLA_V_W]
    r_out[rs, :] = z_mid[:, Z_GR - Z_ZU:Z_GR - Z_ZU + GLA_V_W]


def _inproj(x2, n_batch, mod_rows, mod_row0, lw, tabs):
    n_total = x2.shape[0]
    n_tok = n_total // n_batch
    tm = min(TM_IN, n_tok)
    tpb = n_tok // tm
    tab_blocks = tabs["cq"].shape[0] // tm
    if mod_row0 == 0:
        mod_map = lambda i: (i // tpb, 0, 0)
    else:
        mod_map = lambda i: (mod_row0, 0, 0)
    const2 = lambda i: (0, 0)
    tab_map = lambda i: (i % tab_blocks, 0)
    tok_map = lambda i: (i, 0)
    head_map = lambda i: (i // tpb, 0, i % tpb, 0)
    in_specs = [
        pl.BlockSpec((tm, D_MODEL), tok_map),
        pl.BlockSpec((1, 1, 6 * D_MODEL), mod_map),
        pl.BlockSpec((1, D_MODEL), const2),
        pl.BlockSpec((D_MODEL, Z_WIDTH), const2, pipeline_mode=pl.Buffered(1)),
        pl.BlockSpec((1, MLA_Q_RANK), const2),
        pl.BlockSpec((MLA_Q_RANK, 2048), const2, pipeline_mode=pl.Buffered(1)),
        pl.BlockSpec((1, MLA_KV_RANK), const2),
        pl.BlockSpec((MLA_KV_RANK, 2048), const2, pipeline_mode=pl.Buffered(1)),
        pl.BlockSpec((1, MLA_NOPE), const2),
        pl.BlockSpec((1, MLA_NOPE), const2),
        pl.BlockSpec((tm, MLA_HEADS * MLA_ROPE), tab_map),
        pl.BlockSpec((tm, MLA_HEADS * MLA_ROPE), tab_map),
        pl.BlockSpec((tm, 2 * MLA_ROPE), tab_map),
        pl.BlockSpec((1, SGU_WIDTH), const2),
        pl.BlockSpec((SGU_GROUPS, SGU_CHUNK, SGU_CHUNK), lambda i: (0, 0, 0)),
        pl.BlockSpec((SGU_CHUNK, SGU_WIDTH), const2),
        pl.BlockSpec((LANES, 2 * GLA_QK_W), const2),
        pl.BlockSpec((1, 2 * GLA_QK_W), const2),
    ]
    out_shape = [
        jax.ShapeDtypeStruct((n_batch, MLA_HEADS, n_tok, MLA_QK), BF16),
        jax.ShapeDtypeStruct((n_batch, MLA_HEADS, n_tok, MLA_QK), BF16),
        jax.ShapeDtypeStruct((n_batch, MLA_HEADS, n_tok, MLA_V_AUG), BF16),
        jax.ShapeDtypeStruct((n_total, SGU_WIDTH), BF16),
        jax.ShapeDtypeStruct((n_total, 4 * GLA_QK_W + GLA_V_W), F32),
        jax.ShapeDtypeStruct((n_total, GLA_V_W), F32),
    ]
    out_specs = [
        pl.BlockSpec((1, MLA_HEADS, tm, MLA_QK), head_map),
        pl.BlockSpec((1, MLA_HEADS, tm, MLA_QK), head_map),
        pl.BlockSpec((1, MLA_HEADS, tm, MLA_V_AUG), head_map),
        pl.BlockSpec((tm, SGU_WIDTH), tok_map),
        pl.BlockSpec((tm, 4 * GLA_QK_W + GLA_V_W), tok_map),
        pl.BlockSpec((tm, GLA_V_W), tok_map),
    ]
    return pl.pallas_call(
        _inproj_kernel,
        grid=(n_total // tm,),
        in_specs=in_specs,
        out_specs=out_specs,
        out_shape=out_shape,
        compiler_params=_cparams(("arbitrary",), VMEM_LIMIT_INPROJ),
        name="inproj",
    )(x2, mod_rows, lw["n1"], lw["w_in"], lw["q_norm"], lw["w_uq"], lw["kv_norm"], lw["w_ukv"],
      lw["gq_nope"], lw["gk_nope"], tabs["cq"], tabs["sq"], tabs["tk"],
      lw["sgu_norm"], lw["sgu_w"], lw["sgu_b"], lw["wg"], lw["bg"])


_GLA_LEVELS = (32, 16, 8, 4, 2, 1)
_N_LEV = len(_GLA_LEVELS)
_G_SPLIT = 3
_R_QI = _N_LEV * GLA_CHUNK
_R_KS = _R_QI + GLA_CHUNK
_R_TOT = _R_KS + GLA_CHUNK
_M_ROWS = _R_TOT + 8


def _gla_constants():
    c = GLA_CHUNK
    t = np.arange(c)[:, None]
    j = np.arange(c)[None, :]
    lev, masks = [], []
    for m in _GLA_LEVELS:
        upper = (t // m) % 2 == 1
        bound = (t // (2 * m)) * 2 * m + m - 1
        lev.append(np.where(upper, (j > bound) & (j <= t), (j > t) & (j <= bound)).astype(np.float32))
        s = j
        masks.append(((t // (2 * m) == s // (2 * m)) & upper & ((s // m) % 2 == 0)).astype(np.float32))
    masks.append((t == j).astype(np.float32))
    mqi = (j <= t).astype(np.float32)
    mks = (j > t).astype(np.float32)
    tot = np.ones((8, c), np.float32)
    flip = lambda a: a[::-1, ::-1]
    m_f = np.concatenate(lev + [mqi, mks, tot], axis=0)
    m_b = np.concatenate([flip(a) for a in lev] + [flip(mqi), flip(mks), tot], axis=0)
    mask_f = np.stack([np.tile(a, (GLA_HEADS, 1)) for a in masks])
    mask_b = np.stack([np.tile(flip(a), (GLA_HEADS, 1)) for a in masks])
    m_all = np.tile(np.stack([m_f, m_b]), (1, 1, _G_SPLIT))
    return m_all, np.stack([mask_f, mask_b])


def _stack_heads(a, lane_head):
    return jnp.concatenate([jnp.where(lane_head == hh, a, jnp.zeros_like(a)) for hh in range(GLA_HEADS)], axis=0)


def _gla_chunk(q, k, g, v, st, mall, masks, lane_head):
    c = GLA_CHUNK
    g_hi = g.astype(BF16)
    r1 = g - g_hi.astype(F32)
    g_mid = r1.astype(BF16)
    g_lo = (r1 - g_mid.astype(F32)).astype(BF16)
    e = _dot(mall, jnp.concatenate([g_hi, g_mid, g_lo], axis=0))
    a = masks[_N_LEV] * _dot_nt(_stack_heads(q.astype(BF16), lane_head), k.astype(BF16))
    for lev in range(_N_LEV):
        ex = jnp.exp(e[lev * c:(lev + 1) * c])
        a = a + masks[lev] * _dot_nt(_stack_heads((q * ex).astype(BF16), lane_head), (k * ex).astype(BF16))
    qi = (q * jnp.exp(e[_R_QI:_R_QI + c])).astype(BF16)
    o_inter = _dot_nt(_stack_heads(qi, lane_head), st.astype(BF16))
    vb = v.astype(BF16)
    a16 = a.astype(BF16)
    outs = []
    for hh in range(GLA_HEADS):
        rows = slice(hh * c, (hh + 1) * c)
        outs.append(o_inter[rows] + _dot(a16[rows], vb[:, hh * GLA_DV:(hh + 1) * GLA_DV]))
    o = jnp.concatenate(outs, axis=-1)
    kd = (k * jnp.exp(e[_R_KS:_R_KS + c])).astype(BF16)
    upd = _dot_tn(vb, kd)
    st_new = st * jnp.exp(e[_R_TOT:_R_TOT + 1])
    for hh in range(GLA_HEADS):
        st_new = st_new + jnp.where(lane_head == hh, upd[hh * GLA_DV:(hh + 1) * GLA_DV], 0.0)
    return o, st_new


def _gla_kernel(fwd_ref, bwd_ref, s0_ref, mall_ref, mask_ref, of_ref, ob_ref, sfin_ref, st_ref):
    step = pl.program_id(0)
    n_batch = fwd_ref.shape[0]

    @pl.when(step == 0)
    def _():
        st_ref[...] = s0_ref[...]

    lane_head = lax.broadcasted_iota(jnp.int32, (1, GLA_QK_W), 1) // GLA_DK
    w = GLA_QK_W
    for b in range(n_batch):
        for direction, (in_ref, o_ref) in enumerate(((fwd_ref, of_ref), (bwd_ref, ob_ref))):
            mall = mall_ref[direction]
            masks = [mask_ref[direction, i] for i in range(_N_LEV + 1)]
            q = in_ref[b, :, 0:w]
            k = in_ref[b, :, w:2 * w]
            g = in_ref[b, :, (2 + direction) * w:(3 + direction) * w]
            v = in_ref[b, :, 4 * w:4 * w + GLA_V_W]
            o, st_new = _gla_chunk(q, k, g, v, st_ref[direction, b], mall, masks, lane_head)
            o_ref[b] = o
            st_ref[direction, b] = st_new

    @pl.when(step == pl.num_programs(0) - 1)
    def _():
        sfin_ref[...] = st_ref[...]


def _gla(gla_in, s0, consts):
    n_batch, n_tok, width = gla_in.shape
    nc = n_tok // GLA_CHUNK
    mall, masks = consts
    st_shape = (2, n_batch, GLA_DV, GLA_QK_W)
    return pl.pallas_call(
        _gla_kernel,
        grid=(nc,),
        in_specs=[
            pl.BlockSpec((n_batch, GLA_CHUNK, width), lambda c: (0, c, 0)),
            pl.BlockSpec((n_batch, GLA_CHUNK, width), lambda c: (0, nc - 1 - c, 0)),
            pl.BlockSpec(st_shape, lambda c: (0, 0, 0, 0)),
            pl.BlockSpec(mall.shape, lambda c: (0, 0, 0)),
            pl.BlockSpec(masks.shape, lambda c: (0, 0, 0, 0)),
        ],
        out_specs=[
            pl.BlockSpec((n_batch, GLA_CHUNK, GLA_V_W), lambda c: (0, c, 0)),
            pl.BlockSpec((n_batch, GLA_CHUNK, GLA_V_W), lambda c: (0, nc - 1 - c, 0)),
            pl.BlockSpec(st_shape, lambda c: (0, 0, 0, 0)),
        ],
        out_shape=[
            jax.ShapeDtypeStruct((n_batch, n_tok, GLA_V_W), F32),
            jax.ShapeDtypeStruct((n_batch, n_tok, GLA_V_W), F32),
            jax.ShapeDtypeStruct(st_shape, F32),
        ],
        scratch_shapes=[pltpu.VMEM(st_shape, F32)],
        compiler_params=_cparams(("arbitrary",)),
        name="gla",
    )(gla_in, gla_in, s0, mall, masks)


def _attn_kernel(*refs, n_seg):
    q_ref = refs[0]
    k_refs = refs[1:1 + n_seg]
    v_refs = refs[1 + n_seg:1 + 2 * n_seg]
    o_ref = refs[1 + 2 * n_seg]
    q = q_ref[0, 0]
    n_main = k_refs[0].shape[2]
    tk = min(TK_ATTN, n_main)
    groups = [[(0, slice(c * tk, (c + 1) * tk))] for c in range(n_main // tk)]
    groups[0] += [(i, slice(0, k_refs[i].shape[2])) for i in range(1, n_seg)]
    m = acc = None
    for group in groups:
        scores = [_dot_nt(q, k_refs[i][0, 0, keys, :]) for i, keys in group]
        m_g = functools.reduce(jnp.maximum, [jnp.max(s, axis=-1, keepdims=True) for s in scores])
        m_new = m_g if m is None else jnp.maximum(m, m_g)
        acc_g = functools.reduce(jnp.add, [_dot(jnp.exp2(s - m_new).astype(BF16), v_refs[i][0, 0, keys, :])
                                           for s, (i, keys) in zip(scores, group)])
        acc = acc_g if m is None else jnp.exp2(m - m_new) * acc + acc_g
        m = m_new
    o_ref[0] = (acc[:, 0:MLA_V] / acc[:, MLA_V:MLA_V_AUG]).astype(BF16)


def _attention(q, ks, vs):
    n_batch, n_heads, nq, _ = q.shape
    tq = min(TQ_ATTN, nq)
    n_seg = len(ks)
    in_specs = [pl.BlockSpec((1, 1, tq, MLA_QK), lambda b, h, i: (b, h, i, 0))]
    for k in ks:
        in_specs.append(pl.BlockSpec((1, 1, k.shape[2], MLA_QK), lambda b, h, i: (b, h, 0, 0)))
    for v in vs:
        in_specs.append(pl.BlockSpec((1, 1, v.shape[2], MLA_V_AUG), lambda b, h, i: (b, h, 0, 0)))
    return pl.pallas_call(
        functools.partial(_attn_kernel, n_seg=n_seg),
        grid=(n_batch, n_heads, nq // tq),
        in_specs=in_specs,
        out_specs=pl.BlockSpec((1, tq, MLA_V), lambda b, h, i: (b, i, h)),
        out_shape=jax.ShapeDtypeStruct((n_batch, nq, n_heads * MLA_V), BF16),
        compiler_params=_cparams(("arbitrary", "arbitrary", "arbitrary")),
        name="attn",
    )(q, *ks, *vs)


def _outproj_kernel(a_ref, b_ref, of_ref, ob_ref, r_ref, x_ref, mod_ref, gon_ref, n2_ref, wout_ref,
                    wr_hi_ref, wr_lo_ref, br_ref, lst_ref, sl_ref,
                    x1_out, h2_out, meta_out, cnt_out):
    d = D_MODEL
    tm = x_ref.shape[0]
    gate1 = mod_ref[0, :, 2 * d:3 * d]
    shift2 = mod_ref[0, :, 3 * d:4 * d]
    scale2 = mod_ref[0, :, 4 * d:5 * d]

    og = of_ref[...] + ob_ref[...]
    gon = gon_ref[...]
    parts = []
    for hh in range(GLA_HEADS):
        seg = og[:, hh * GLA_DV:(hh + 1) * GLA_DV]
        parts.append(seg * lax.rsqrt(jnp.mean(seg * seg, axis=-1, keepdims=True) + EPS) * gon)
    cg = jnp.concatenate(parts, axis=-1) * _silu(r_ref[...])
    na = MLA_HEADS * MLA_V
    y = _dot(a_ref[...], wout_ref[0:na, :])
    y = y + _dot(b_ref[...], wout_ref[na:na + SGU_WIDTH, :])
    y = y + _dot(cg.astype(BF16), wout_ref[na + SGU_WIDTH:na + SGU_WIDTH + GLA_V_W, :])
    x1 = x_ref[...] + gate1 * y
    x1_out[...] = x1

    ms = jnp.mean(x1 * x1, axis=-1, keepdims=True)
    h2 = (x1 * lax.rsqrt(ms + EPS) * n2_ref[...]) * (1.0 + scale2) + shift2
    h_hi = h2.astype(BF16)
    h2_out[...] = h_hi
    h_lo = (h2 - h_hi.astype(F32)).astype(BF16)

    wr_hi = wr_hi_ref[...]
    lt2 = _dot_nt(jnp.concatenate([wr_hi, wr_lo_ref[...]], axis=0), h_hi)
    lt = lt2[0:32] + lt2[32:64] + _dot_nt(wr_hi, h_lo) + br_ref[:, 0:1]
    grp = [lt[MOE_EXPERTS + g:MOE_EXPERTS + g + 1, :] for g in range(MOE_GROUPS)]
    gmax = functools.reduce(jnp.maximum, grp)
    gsel = jnp.full(gmax.shape, MOE_GROUPS - 1, jnp.int32)
    for g in range(MOE_GROUPS - 2, -1, -1):
        gsel = jnp.where(grp[g] == gmax, g, gsel)
    gsum = functools.reduce(jnp.add, [jnp.exp(gl - gmax) for gl in grp])
    g_w = 1.0 / gsum
    e_in = []
    for j in range(MOE_PER_GROUP):
        ej = lt[j:j + 1, :]
        for g in range(1, MOE_GROUPS):
            ej = jnp.where(gsel == g, lt[g * MOE_PER_GROUP + j:g * MOE_PER_GROUP + j + 1, :], ej)
        e_in.append(ej)
    m1 = functools.reduce(jnp.maximum, e_in)
    i1 = jnp.full(m1.shape, MOE_PER_GROUP - 1, jnp.int32)
    for j in range(MOE_PER_GROUP - 2, -1, -1):
        i1 = jnp.where(e_in[j] == m1, j, i1)
    e_rest = [jnp.where(i1 == j, -jnp.inf, e_in[j]) for j in range(MOE_PER_GROUP)]
    m2 = functools.reduce(jnp.maximum, e_rest)
    i2 = jnp.full(m2.shape, MOE_PER_GROUP - 1, jnp.int32)
    for j in range(MOE_PER_GROUP - 2, -1, -1):
        i2 = jnp.where(e_rest[j] == m2, j, i2)
    p2 = jnp.exp(m2 - m1)
    w1 = g_w / (1.0 + p2)
    w2 = g_w * p2 / (1.0 + p2)
    eid1 = gsel * MOE_PER_GROUP + i1
    eid2 = gsel * MOE_PER_GROUP + i2

    e_iota = lax.broadcasted_iota(jnp.int32, (MOE_EXPERTS, tm), 0)
    hit1 = e_iota == eid1
    hit2 = e_iota == eid2
    onehot = jnp.where(hit1, 1.0, 0.0) + jnp.where(hit2, 1.0, 0.0)
    rank = _dot(onehot.astype(BF16), lst_ref[...])
    counts = jnp.sum(onehot, axis=1, keepdims=True)
    padded = jnp.ceil(counts * (1.0 / BF16_ROWS)) * BF16_ROWS
    padded_l = jnp.broadcast_to(padded, (MOE_EXPERTS, LANES))
    off = _dot(sl_ref[...], padded_l.astype(BF16))
    pos_e = off[:, 0:1] + rank
    pos1 = jnp.sum(jnp.where(hit1, pos_e, 0.0), axis=0, keepdims=True)
    pos2 = jnp.sum(jnp.where(hit2, pos_e, 0.0), axis=0, keepdims=True)
    zero = jnp.zeros_like(pos1)
    meta_out[0] = jnp.concatenate(
        [pos1, pos2, w1, w2, eid1.astype(F32), eid2.astype(F32), zero, zero], axis=0)
    cnt_out[0] = padded_l


def _outproj(a, b, o_f, o_b, r, x2, n_batch, mod_rows, mod_row0, lw, rconst):
    n_total = x2.shape[0]
    n_tok = n_total // n_batch
    tm = T_LOC
    tpb = n_tok // tm
    if mod_row0 == 0:
        mod_map = lambda i: (i // tpb, 0, 0)
    else:
        mod_map = lambda i: (mod_row0, 0, 0)
    const2 = lambda i: (0, 0)
    tok_map = lambda i: (i, 0)
    n_tiles = n_total // tm
    in_specs = [
        pl.BlockSpec((tm, MLA_HEADS * MLA_V), tok_map),
        pl.BlockSpec((tm, SGU_WIDTH), tok_map),
        pl.BlockSpec((tm, GLA_V_W), tok_map),
        pl.BlockSpec((tm, GLA_V_W), tok_map),
        pl.BlockSpec((tm, GLA_V_W), tok_map),
        pl.BlockSpec((tm, D_MODEL), tok_map),
        pl.BlockSpec((1, 1, 6 * D_MODEL), mod_map),
        pl.BlockSpec((1, GLA_DV), const2),
        pl.BlockSpec((1, D_MODEL), const2),
        pl.BlockSpec((D_MODEL, D_MODEL), const2, pipeline_mode=pl.Buffered(1)),
        pl.BlockSpec((32, D_MODEL), const2),
        pl.BlockSpec((32, D_MODEL), const2),
        pl.BlockSpec((32, LANES), const2),
        pl.BlockSpec((tm, tm), const2),
        pl.BlockSpec((MOE_EXPERTS, MOE_EXPERTS), const2),
    ]
    out_shape = [
        jax.ShapeDtypeStruct((n_total, D_MODEL), F32),
        jax.ShapeDtypeStruct((n_total, D_MODEL), BF16),
        jax.ShapeDtypeStruct((n_tiles, 8, tm), F32),
        jax.ShapeDtypeStruct((n_tiles, MOE_EXPERTS, LANES), F32),
    ]
    out_specs = [
        pl.BlockSpec((tm, D_MODEL), tok_map),
        pl.BlockSpec((tm, D_MODEL), tok_map),
        pl.BlockSpec((1, 8, tm), lambda i: (i, 0, 0)),
        pl.BlockSpec((1, MOE_EXPERTS, LANES), lambda i: (i, 0, 0)),
    ]
    return pl.pallas_call(
        _outproj_kernel,
        grid=(n_tiles,),
        in_specs=in_specs,
        out_specs=out_specs,
        out_shape=out_shape,
        compiler_params=_cparams(("arbitrary",)),
        name="outproj",
    )(a, b, o_f, o_b, r, x2, mod_rows, lw["gla_out_norm"], lw["n2"], lw["w_out"],
      lw["wr_hi"], lw["wr_lo"], lw["br"], rconst["lst"], rconst["sl"])


def _piece_copies(src_ref, dst_ref, sem, src_row, dst_row, n_rows):
    copies = []
    for size in _PIECE_SIZES:
        done = n_rows & ~(2 * size - 1)
        pred = (n_rows & size) != 0
        s0 = pl.multiple_of(src_row + done, BF16_ROWS)
        d0 = pl.multiple_of(dst_row + done, BF16_ROWS)
        cp = pltpu.make_async_copy(src_ref.at[pl.ds(s0, size)], dst_ref.at[pl.ds(d0, size)], sem)
        copies.append((pred, cp))
    return copies


def _list_base(tile, s):
    return tile * len(_PIECE_SIZES) + s


def _start_listed_pieces(src_ref, dst_ref, sem, src_tab, dst_tab, n_tab, tile):
    for s, size in enumerate(_PIECE_SIZES):
        base = _list_base(tile, s)

        def body(i, carry, size=size, base=base):
            idx = base * MOE_EXPERTS + i
            s0 = pl.multiple_of(src_tab[idx], BF16_ROWS)
            d0 = pl.multiple_of(dst_tab[idx], BF16_ROWS)
            pltpu.make_async_copy(src_ref.at[pl.ds(s0, size)], dst_ref.at[pl.ds(d0, size)], sem).start()
            return carry

        lax.fori_loop(0, n_tab[base], body, 0)


def _wait_listed_pieces(src_ref, dst_ref, sem, n_tab, tile):
    for s, size in enumerate(_PIECE_SIZES):
        def body(i, carry, size=size):
            pltpu.make_async_copy(src_ref.at[pl.ds(0, size)], dst_ref.at[pl.ds(0, size)], sem).wait()
            return carry

        lax.fori_loop(0, n_tab[_list_base(tile, s)], body, 0)


def _stream_starts(stream_tiles):
    return [sum(stream_tiles[:s]) for s in range(len(stream_tiles))]


def _stream_block(i, start, n):
    return jnp.clip(i - start, 0, n - 1)


def _dispatch_kernel(ploc_tab, pdst_tab, pn_tab, ztab_dst, ztab_cnt, n_used, *refs, stream_tiles):
    n_streams = len(stream_tiles)
    h2_refs = refs[:n_streams]
    meta_ref, xs_ref, xloc_ref, zeros_ref, sems = refs[n_streams:]
    j = pl.program_id(0)
    n_tiles = pl.num_programs(0)
    tm = h2_refs[0].shape[0]
    slot = j % 2

    @pl.when(j == 0)
    def _():
        zeros_ref[...] = jnp.zeros_like(zeros_ref)

        def fill(t, carry):
            cp = pltpu.make_async_copy(zeros_ref, xs_ref.at[pl.ds(pl.multiple_of(t * TM_FFN, TM_FFN), TM_FFN)],
                                       sems.at[0])
            cp.start()
            cp.wait()
            return carry
        lax.fori_loop(n_used[0], xs_ref.shape[0] // TM_FFN, fill, 0)
        zcopies = []
        for e in range(MOE_EXPERTS):
            zcopies += _piece_copies(zeros_ref, xs_ref, sems.at[0], 0, ztab_dst[e], ztab_cnt[e])
        for pred, cp in zcopies:
            @pl.when(pred)
            def _(cp=cp):
                cp.start()
        for pred, cp in zcopies:
            @pl.when(pred)
            def _(cp=cp):
                cp.wait()

    pos1 = meta_ref[0, 0:1, :]
    pos2 = meta_ref[0, 1:2, :]
    s_iota = lax.broadcasted_iota(jnp.int32, (SLOTS, tm), 0).astype(F32)
    perm = jnp.where((s_iota == pos1) | (s_iota == pos2), 1.0, 0.0).astype(BF16)
    h2 = h2_refs[0][...]
    for s, start in list(enumerate(_stream_starts(stream_tiles)))[1:]:
        h2 = jnp.where(j >= start, h2_refs[s][...], h2)
    xloc_ref[slot] = _dot(perm, h2).astype(BF16)

    @pl.when(j > 0)
    def _():
        _wait_listed_pieces(xloc_ref.at[1 - slot], xs_ref, sems.at[1 - slot], pn_tab, j - 1)

    _start_listed_pieces(xloc_ref.at[slot], xs_ref, sems.at[slot], ploc_tab, pdst_tab, pn_tab, j)

    @pl.when(j == n_tiles - 1)
    def _():
        _wait_listed_pieces(xloc_ref.at[slot], xs_ref, sems.at[slot], pn_tab, j)


def _dispatch(h2s, meta, tabs, r_max):
    n_tiles = meta.shape[0]
    tm = T_LOC
    stream_tiles = tuple(h.shape[0] // tm for h in h2s)
    assert sum(stream_tiles) == n_tiles
    h2_specs = [pl.BlockSpec((tm, D_MODEL), lambda i, *_, start=start, n=n: (_stream_block(i, start, n), 0))
                for start, n in zip(_stream_starts(stream_tiles), stream_tiles)]
    grid_spec = pltpu.PrefetchScalarGridSpec(
        num_scalar_prefetch=6,
        grid=(n_tiles,),
        in_specs=h2_specs + [pl.BlockSpec((1, 8, tm), lambda i, *_: (i, 0, 0))],
        out_specs=pl.BlockSpec(memory_space=pl.ANY),
        scratch_shapes=[pltpu.VMEM((2, SLOTS, D_MODEL), BF16), pltpu.VMEM((TM_FFN, D_MODEL), BF16),
                        pltpu.SemaphoreType.DMA((2,))],
    )
    return pl.pallas_call(
        functools.partial(_dispatch_kernel, stream_tiles=stream_tiles),
        grid_spec=grid_spec,
        out_shape=jax.ShapeDtypeStruct((r_max, D_MODEL), BF16),
        compiler_params=_cparams(("arbitrary",)),
        name="dispatch",
    )(tabs["ploc"], tabs["pdst"], tabs["pn"], tabs["zdst"], tabs["zcnt"], tabs["n_used"], *h2s, meta)


def _ffn_chunk(i, k, n_used):
    last = n_used[0] - 1
    k_eff = jnp.where(i > last, N_FCHUNK - 1, k)
    return jnp.where(jnp.minimum(i, last) % 2 == 0, k_eff, N_FCHUNK - 1 - k_eff)


def _ffn_kernel(tile_e, tile_blk, n_used, x_ref, w1_ref, w3_ref, w2_ref, y_ref, acc_ref):
    i = pl.program_id(0)
    k = pl.program_id(1)

    @pl.when(i < n_used[0])
    def _():
        x = x_ref[...]
        part = None
        for f in range(F_CHUNK // F_SUB):
            cols = slice(f * F_SUB, (f + 1) * F_SUB)
            h1 = _dot(x, w1_ref[0, 0, :, cols].astype(BF16))
            h3 = _dot(x, w3_ref[0, 0, :, cols].astype(BF16))
            act = (_silu(h1) * h3).astype(BF16)
            p_f = _dot(act, w2_ref[0, 0, cols, :].astype(BF16))
            part = p_f if part is None else part + p_f

        @pl.when(k == 0)
        def _():
            acc_ref[...] = part

        @pl.when((k > 0) & (k < N_FCHUNK - 1))
        def _():
            acc_ref[...] += part

        @pl.when(k == N_FCHUNK - 1)
        def _():
            y_ref[...] = (acc_ref[...] + part).astype(BF16)

    @pl.when((i >= n_used[0]) & (k == N_FCHUNK - 1))
    def _():
        y_ref[...] = jnp.zeros_like(y_ref)


def _ffn(xs, layer, w1, w3, w2, tabs):
    r_max = xs.shape[0]
    n_mt = r_max // TM_FFN
    grid_spec = pltpu.PrefetchScalarGridSpec(
        num_scalar_prefetch=3,
        grid=(n_mt, N_FCHUNK),
        in_specs=[
            pl.BlockSpec((TM_FFN, D_MODEL), lambda i, k, te, tb, nu: (tb[i], 0)),
            pl.BlockSpec((1, 1, D_MODEL, F_CHUNK), lambda i, k, te, tb, nu: (layer, te[i], 0, _ffn_chunk(i, k, nu))),
            pl.BlockSpec((1, 1, D_MODEL, F_CHUNK), lambda i, k, te, tb, nu: (layer, te[i], 0, _ffn_chunk(i, k, nu))),
            pl.BlockSpec((1, 1, F_CHUNK, D_MODEL), lambda i, k, te, tb, nu: (layer, te[i], _ffn_chunk(i, k, nu), 0)),
        ],
        out_specs=pl.BlockSpec((TM_FFN, D_MODEL), lambda i, k, te, tb, nu: (i, 0)),
        scratch_shapes=[pltpu.VMEM((TM_FFN, D_MODEL), F32)],
    )
    return pl.pallas_call(
        _ffn_kernel,
        grid_spec=grid_spec,
        out_shape=jax.ShapeDtypeStruct((r_max, D_MODEL), BF16),
        compiler_params=_cparams(("arbitrary", "arbitrary")),
        name="ffn",
    )(tabs["tile_e"], tabs["tile_blk"], tabs["n_used"], xs, w1, w3, w2)


def _combine_kernel(ploc_tab, pdst_tab, pn_tab, y_ref, metac_ref, mod_ref, *refs, stream_tiles):
    n_streams = len(stream_tiles)
    x1_refs = refs[:n_streams]
    out_refs = refs[n_streams:2 * n_streams]
    yloc_ref, sems = refs[2 * n_streams:]
    j = pl.program_id(0)
    n_tiles = pl.num_programs(0)
    d = D_MODEL
    tm = x1_refs[0].shape[0]
    slot = j % 2

    @pl.when(j == 0)
    def _():
        yloc_ref[...] = jnp.zeros_like(yloc_ref)
        _start_listed_pieces(y_ref, yloc_ref.at[0], sems.at[0], pdst_tab, ploc_tab, pn_tab, 0)

    @pl.when(j + 1 < n_tiles)
    def _():
        _start_listed_pieces(y_ref, yloc_ref.at[1 - slot], sems.at[1 - slot], pdst_tab, ploc_tab, pn_tab, j + 1)

    _wait_listed_pieces(y_ref, yloc_ref.at[slot], sems.at[slot], pn_tab, j)

    mc = metac_ref[0]
    s_iota = lax.broadcasted_iota(jnp.int32, (tm, SLOTS), 1).astype(F32)
    wc = jnp.where(s_iota == mc[:, 0:1], mc[:, 2:3], 0.0) + jnp.where(s_iota == mc[:, 1:2], mc[:, 3:4], 0.0)
    moe = _dot(wc.astype(BF16), yloc_ref[slot])
    gate2 = mod_ref[0, :, 5 * d:6 * d]
    for start, n, x1_ref, out_ref in zip(_stream_starts(stream_tiles), stream_tiles, x1_refs, out_refs):
        @pl.when((j >= start) & (j < start + n))
        def _(x1_ref=x1_ref, out_ref=out_ref):
            out_ref[...] = x1_ref[...] + gate2 * moe


def _combine(y, metac, x1s, n_batch, mod_rows, mod_row0s, tabs):
    tm = T_LOC
    stream_tiles = tuple(x1.shape[0] // tm for x1 in x1s)
    starts = _stream_starts(stream_tiles)
    n_tiles = sum(stream_tiles)

    def mod_map(i, *_):
        row = jnp.int32(0)
        for start, n, x1, row0 in zip(starts, stream_tiles, x1s, mod_row0s):
            tpb = x1.shape[0] // n_batch // tm
            row_s = _stream_block(i, start, n) // tpb if row0 == 0 else row0
            row = jnp.where(i >= start, row_s, row)
        return (row, 0, 0)

    tile_specs = [pl.BlockSpec((tm, D_MODEL), lambda i, *_, start=start, n=n: (_stream_block(i, start, n), 0))
                  for start, n in zip(starts, stream_tiles)]
    grid_spec = pltpu.PrefetchScalarGridSpec(
        num_scalar_prefetch=3,
        grid=(n_tiles,),
        in_specs=[
            pl.BlockSpec(memory_space=pl.ANY),
            pl.BlockSpec((1, tm, 8), lambda i, *_: (i, 0, 0)),
            pl.BlockSpec((1, 1, 6 * D_MODEL), mod_map),
        ] + tile_specs,
        out_specs=tile_specs,
        scratch_shapes=[pltpu.VMEM((2, SLOTS, D_MODEL), BF16), pltpu.SemaphoreType.DMA((2,))],
    )
    return pl.pallas_call(
        functools.partial(_combine_kernel, stream_tiles=stream_tiles),
        grid_spec=grid_spec,
        out_shape=[jax.ShapeDtypeStruct(x1.shape, F32) for x1 in x1s],
        compiler_params=_cparams(("arbitrary",)),
        name="combine",
    )(tabs["ploc"], tabs["pdst"], tabs["pn"], y, metac, mod_rows, *x1s)


def _rope_tables(n_tokens):
    rows = n_tokens // GRID_W
    row = jnp.repeat(jnp.arange(rows, dtype=jnp.int32), GRID_W)
    col = jnp.tile(jnp.arange(GRID_W, dtype=jnp.int32), rows)
    n_freq = MLA_ROPE // 4
    inv_freq = ROPE_THETA ** (-jnp.arange(n_freq, dtype=F32) / n_freq)
    ang_r = row[:, None].astype(F32) * inv_freq
    ang_c = col[:, None].astype(F32) * inv_freq
    cos = jnp.concatenate([jnp.cos(ang_r)] * 2 + [jnp.cos(ang_c)] * 2, axis=-1)
    sin = jnp.concatenate([-jnp.sin(ang_r), jnp.sin(ang_r), -jnp.sin(ang_c), jnp.sin(ang_c)], axis=-1)
    return cos, sin


_SWAP64 = np.concatenate([np.arange(16, 32), np.arange(0, 16), np.arange(48, 64), np.arange(32, 48)])


def _position_tables(cos, sin, q_gain, k_gain):
    gq = q_gain[MLA_NOPE:]
    gk = k_gain[MLA_NOPE:]
    cq = jnp.tile(cos * gq, (1, MLA_HEADS))
    sq = jnp.tile(sin * gq[_SWAP64], (1, MLA_HEADS))
    tk = jnp.concatenate([cos * gk, sin * gk[_SWAP64]], axis=-1)
    return {"cq": cq, "sq": sq, "tk": tk}


def _layer_weights(l, p):
    d = D_MODEL
    o = _IN_OFF
    w_in = p["w_in"][l]
    seg = lambda i: w_in[:, o[i]:o[i + 1]]
    kpe = seg(2)
    w_in_r = jnp.concatenate(
        [seg(0), seg(1), seg(3), seg(4), seg(5) * (GLA_DK ** -0.5), seg(6), seg(7), seg(8),
         kpe, kpe[:, _SWAP64], seg(9), seg(10), jnp.zeros((d, Z_WIDTH - Z_GG - 2 * GLA_GATE_RANK), F32)],
        axis=-1).astype(BF16)
    w_uq = p["mla_w_uq"][l].reshape(MLA_Q_RANK, MLA_HEADS, MLA_QK)
    q_nope = w_uq[:, :, :MLA_NOPE].reshape(MLA_Q_RANK, -1)
    q_pe = w_uq[:, :, MLA_NOPE:]
    w_uq_r = jnp.concatenate([q_nope, q_pe.reshape(MLA_Q_RANK, -1), q_pe[:, :, _SWAP64].reshape(MLA_Q_RANK, -1)],
                             axis=-1).astype(BF16)
    w_ukv = p["mla_w_ukv"][l].reshape(MLA_KV_RANK, MLA_HEADS, MLA_NOPE + MLA_V)
    w_ukv_r = jnp.concatenate([w_ukv[:, :, :MLA_NOPE].reshape(MLA_KV_RANK, -1),
                               w_ukv[:, :, MLA_NOPE:].reshape(MLA_KV_RANK, -1)], axis=-1).astype(BF16)
    wg = jnp.zeros((LANES, 2 * GLA_QK_W), F32)
    wg = wg.at[0:GLA_GATE_RANK, 0:GLA_QK_W].set(p["gla_wg_f"][l])
    wg = wg.at[GLA_GATE_RANK:2 * GLA_GATE_RANK, GLA_QK_W:].set(p["gla_wg_b"][l])
    bg = jnp.concatenate([p["gla_bg_f"][l], p["gla_bg_b"][l]])[None, :]
    sgu_b = jnp.repeat(p["sgu_b"][l].T, SGU_GROUP_DIM, axis=1)
    wr = jnp.zeros((32, d), F32)
    wr = wr.at[0:MOE_EXPERTS].set(p["moe_w_expert"][l].T)
    wr = wr.at[MOE_EXPERTS:MOE_EXPERTS + MOE_GROUPS].set(p["moe_w_group"][l].T)
    wr_hi = wr.astype(BF16)
    wr_lo = (wr - wr_hi.astype(F32)).astype(BF16)
    br = jnp.zeros((32,), F32).at[0:MOE_EXPERTS].set(p["moe_b_expert"][l])
    br = br.at[MOE_EXPERTS:MOE_EXPERTS + MOE_GROUPS].set(p["moe_b_group"][l])
    return {
        "n1": p["norm1_g"][l][None, :], "n2": p["norm2_g"][l][None, :],
        "w_in": w_in_r, "q_norm": p["mla_q_norm"][l][None, :], "kv_norm": p["mla_kv_norm"][l][None, :],
        "w_uq": w_uq_r, "w_ukv": w_ukv_r,
        "gq_nope": p["mla_q_gain"][l][None, :MLA_NOPE], "gk_nope": p["mla_k_gain"][l][None, :MLA_NOPE],
        "sgu_norm": p["sgu_norm"][l].reshape(1, SGU_WIDTH), "sgu_w": p["sgu_w"][l].astype(BF16), "sgu_b": sgu_b,
        "wg": wg.astype(BF16), "bg": bg,
        "gla_out_norm": p["gla_out_norm"][l][None, :], "w_out": p["w_out"][l].astype(BF16),
        "wr_hi": wr_hi, "wr_lo": wr_lo, "br": jnp.broadcast_to(br[:, None], (32, LANES)),
        "layer": l, "w1": p["moe_w1"], "w3": p["moe_w3"], "w2": p["moe_w2"],
    }


def _router_constants():
    t = np.arange(T_LOC)
    lst = (t[:, None] < t[None, :]).astype(np.float32)
    e = np.arange(MOE_EXPERTS)
    sl = (e[None, :] < e[:, None]).astype(np.float32)
    return {"lst": jnp.asarray(lst, BF16), "sl": jnp.asarray(sl, BF16)}


def _dispatch_tables(cnt, r_max):
    n_tiles = cnt.shape[0]
    cnt = cnt.astype(jnp.int32)
    loc_off = jnp.cumsum(cnt, axis=1) - cnt
    rows_e = jnp.sum(cnt, axis=0)
    tiles_e = (rows_e + TM_FFN - 1) // TM_FFN
    tile_start = jnp.cumsum(tiles_e) - tiles_e
    base_e = tile_start * TM_FFN
    dst = base_e[None, :] + jnp.cumsum(cnt, axis=0) - cnt
    n_used = jnp.sum(tiles_e)
    n_mt = r_max // TM_FFN
    ti = jnp.arange(n_mt, dtype=jnp.int32)
    tile_blk = jnp.minimum(ti, n_used - 1)
    tile_e = jnp.sum(tile_blk[:, None] >= (tile_start + tiles_e)[None, :], axis=1).astype(jnp.int32)
    tile_e = jnp.minimum(tile_e, MOE_EXPERTS - 1)
    zdst = base_e + rows_e
    zcnt = tiles_e * TM_FFN - rows_e
    i32 = lambda a: a.astype(jnp.int32)
    sizes = jnp.asarray(_PIECE_SIZES, jnp.int32)
    has = (cnt[:, :, None] & sizes) != 0
    done = cnt[:, :, None] & ~(2 * sizes - 1)
    pos = jnp.cumsum(has, axis=1) - 1
    pick = has[..., None] & (pos[..., None] == jnp.arange(MOE_EXPERTS))
    listed = lambda a: jnp.sum(jnp.where(pick, (a[:, :, None] + done)[..., None], 0), axis=1)
    return {"ploc": i32(listed(loc_off).reshape(-1)), "pdst": i32(listed(dst).reshape(-1)),
            "pn": i32(jnp.sum(has, axis=1).reshape(-1)), "zdst": i32(zdst), "zcnt": i32(zcnt),
            "tile_e": tile_e, "tile_blk": i32(tile_blk), "n_used": i32(n_used.reshape(1))}


def _moe(h2s, metas, cnts, x1s, n_batch, mod_rows, mod_row0s, lw):
    meta = jnp.concatenate(metas, axis=0)
    cnt = jnp.concatenate(cnts, axis=0)
    n_tiles = meta.shape[0]
    worst_rows = n_tiles * SLOTS + MOE_EXPERTS * (TM_FFN - BF16_ROWS)
    r_max = -(-worst_rows // TM_FFN) * TM_FFN
    tabs = _dispatch_tables(cnt[:, :, 0], r_max)
    xs = _dispatch(h2s, meta, tabs, r_max)
    y = _ffn(xs, lw["layer"], lw["w1"], lw["w3"], lw["w2"], tabs)
    metac = jnp.swapaxes(meta, 1, 2)
    return _combine(y, metac, x1s, n_batch, mod_rows, mod_row0s, tabs)


def kernel(x, c, ctx, c_ctx, norm1_g, norm2_g, ada_w, ada_b, w_in, mla_q_norm, mla_w_uq, mla_kv_norm, mla_w_ukv,
           mla_q_gain, mla_k_gain, sgu_norm, sgu_w, sgu_b, gla_wg_f, gla_bg_f, gla_wg_b, gla_bg_b, gla_out_norm,
           w_out, moe_w_group, moe_b_group, moe_w_expert, moe_b_expert, moe_w1, moe_w3, moe_w2):
    p = dict(norm1_g=norm1_g, norm2_g=norm2_g, w_in=w_in, mla_q_norm=mla_q_norm, mla_w_uq=mla_w_uq,
             mla_kv_norm=mla_kv_norm, mla_w_ukv=mla_w_ukv, mla_q_gain=mla_q_gain, mla_k_gain=mla_k_gain,
             sgu_norm=sgu_norm, sgu_w=sgu_w, sgu_b=sgu_b, gla_wg_f=gla_wg_f, gla_bg_f=gla_bg_f,
             gla_wg_b=gla_wg_b, gla_bg_b=gla_bg_b, gla_out_norm=gla_out_norm, w_out=w_out,
             moe_w_group=moe_w_group, moe_b_group=moe_b_group, moe_w_expert=moe_w_expert,
             moe_b_expert=moe_b_expert, moe_w1=moe_w1, moe_w3=moe_w3, moe_w2=moe_w2)
    n_batch, seq, d = x.shape
    n_ctx = ctx.shape[1]
    depth = ada_w.shape[0]
    assert d == D_MODEL and n_batch <= 4 and seq % T_LOC == 0 and seq % TM_IN == 0 and n_ctx % TM_SUB == 0
    assert (n_batch * n_ctx) % T_LOC == 0

    c_rows = jnp.zeros((8, d), F32).at[0:n_batch].set(c).at[4].set(c_ctx)
    mod = _ada(c_rows, ada_w, ada_b)

    cos, sin = _rope_tables(seq)
    ones_c = jnp.ones((min(TM_IN, n_ctx), MLA_ROPE), F32)
    gla_consts_np = _gla_constants()
    gla_consts = (jnp.asarray(gla_consts_np[0], BF16), jnp.asarray(gla_consts_np[1], F32))
    rconst = _router_constants()
    zero_state = jnp.zeros((2, n_batch, GLA_DV, GLA_QK_W), F32)

    x2 = x.reshape(n_batch * seq, d)
    ctx2 = ctx.reshape(n_batch * n_ctx, d)
    for l in range(depth):
        ctx_out = l < depth - 1
        lw = _layer_weights(l, p)
        mod_rows = mod[l].reshape(8, 1, 6 * d)
        tabs_x = _position_tables(cos, sin, mla_q_gain[l], mla_k_gain[l])
        tabs_c = _position_tables(ones_c, 0.0 * ones_c, mla_q_gain[l], mla_k_gain[l])

        q_x, k_x, v_x, b_x, gla_x, r_x = _inproj(x2, n_batch, mod_rows, 0, lw, tabs_x)
        q_c, k_c, v_c, b_c, gla_c, r_c = _inproj(ctx2, n_batch, mod_rows, 4, lw, tabs_c)

        of_c, ob_c, s_ctx = _gla(gla_c.reshape(n_batch, n_ctx, -1), zero_state, gla_consts)
        of_x, ob_x, _ = _gla(gla_x.reshape(n_batch, seq, -1), s_ctx, gla_consts)
        a_x = _attention(q_x, [k_x, k_c], [v_x, v_c])

        x1, h2, meta, cnt = _outproj(a_x.reshape(n_batch * seq, -1), b_x, of_x.reshape(n_batch * seq, -1),
                                     ob_x.reshape(n_batch * seq, -1), r_x, x2, n_batch, mod_rows, 0, lw, rconst)
        if ctx_out:
            a_c = _attention(q_c, [k_c], [v_c])
            c1, h2c, meta_c, cnt_c = _outproj(a_c.reshape(n_batch * n_ctx, -1), b_c,
                                              of_c.reshape(n_batch * n_ctx, -1), ob_c.reshape(n_batch * n_ctx, -1),
                                              r_c, ctx2, n_batch, mod_rows, 4, lw, rconst)
            x2, ctx2 = _moe([h2, h2c], [meta, meta_c], [cnt, cnt_c], [x1, c1], n_batch, mod_rows, [0, 4], lw)
        else:
            x2, = _moe([h2], [meta], [cnt], [x1], n_batch, mod_rows, [0], lw)
    return x2.reshape(n_batch, seq, d)
```

```python
import functools

import numpy as np
import jax
import jax.numpy as jnp
from jax import lax
from jax.experimental import pallas as pl
from jax.experimental.pallas import tpu as pltpu

F32 = jnp.float32
BF16 = jnp.bfloat16

D_MODEL = 2048
GRID_W = 64
EPS = 1e-6

MLA_HEADS = 8
MLA_Q_RANK = 512
MLA_KV_RANK = 512
MLA_NOPE = 128
MLA_ROPE = 64
MLA_V = 128
MLA_V_AUG = 2 * MLA_V
MLA_QK = MLA_NOPE + MLA_ROPE
MLA_SCALE = MLA_QK ** -0.5
ROPE_THETA = 10000.0
LOG2_E = float(np.log2(np.e))

SGU_WIDTH = 512
SGU_GROUPS = 4
SGU_GROUP_DIM = SGU_WIDTH // SGU_GROUPS
SGU_CHUNK = 128

GLA_HEADS = 4
GLA_DK = 64
GLA_DV = 128
GLA_GATE_RANK = 16
GLA_GATE_NORM = 16.0
GLA_CHUNK = 64
GLA_QK_W = GLA_HEADS * GLA_DK
GLA_V_W = GLA_HEADS * GLA_DV

MOE_GROUPS = 4
MOE_PER_GROUP = 4
MOE_EXPERTS = MOE_GROUPS * MOE_PER_GROUP
D_EXPERT = 1024

_IN_SPLITS = (512, 512, 64, 512, 512, 256, 256, 512, 512, 16, 16)
_IN_OFF = np.concatenate([[0], np.cumsum(_IN_SPLITS)]).tolist()

Z_CQ, Z_CKV, Z_ZU, Z_ZV = 0, 512, 1024, 1536
Z_GQ, Z_GK, Z_GV, Z_GR = 2048, 2304, 2560, 3072
Z_KPE, Z_GG = 3584, 3712
Z_WIDTH = 3840

LANES = 128
BF16_ROWS = 16
VMEM_LIMIT = 56 * 1024 * 1024
VMEM_LIMIT_INPROJ = 61 * 1024 * 1024

TM_IN = 512
TM_SUB = 256
TQ_ATTN = 1024
TK_ATTN = 2048
T_LOC = 512
SLOTS = 2 * T_LOC + MOE_EXPERTS * BF16_ROWS
TM_FFN = 512
F_CHUNK = 256
_PIECE_SIZES = (512, 256, 128, 64, 32, 16)


def _cparams(sem, vmem=VMEM_LIMIT):
    return pltpu.CompilerParams(dimension_semantics=sem, vmem_limit_bytes=vmem)


def _silu(x):
    return x / (1.0 + jnp.exp(-x))


def _gelu_tanh(x):
    return 0.5 * x * (1.0 + jnp.tanh(np.sqrt(2.0 / np.pi).astype(np.float32) * (x + 0.044715 * (x * x * x))))


def _log_sigmoid(x):
    return jnp.minimum(x, 0.0) - jnp.log(1.0 + jnp.exp(-jnp.abs(x)))


def _dot(a, b):
    return jnp.dot(a, b, preferred_element_type=F32)


def _dot_nt(a, b):
    return lax.dot_general(a, b, (((1,), (1,)), ((), ())), preferred_element_type=F32)


def _dot_tn(a, b):
    return lax.dot_general(a, b, (((0,), (0,)), ((), ())), preferred_element_type=F32)


def _ada_kernel(c_ref, w_ref, b_ref, o_ref):
    a = _silu(c_ref[...]).astype(BF16)
    o_ref[0] = _dot(a, w_ref[0].astype(BF16)) + b_ref[0]


def _ada(c_rows, ada_w, ada_b):
    depth, d, n = ada_w.shape
    tn = 1536
    return pl.pallas_call(
        _ada_kernel,
        grid=(depth, n // tn),
        in_specs=[
            pl.BlockSpec((8, d), lambda l, j: (0, 0)),
            pl.BlockSpec((1, d, tn), lambda l, j: (l, 0, j)),
            pl.BlockSpec((1, 1, tn), lambda l, j: (l, 0, j)),
        ],
        out_specs=pl.BlockSpec((1, 8, tn), lambda l, j: (l, 0, j)),
        out_shape=jax.ShapeDtypeStruct((depth, 8, n), F32),
        compiler_params=_cparams(("arbitrary", "arbitrary")),
        name="ada",
    )(c_rows, ada_w, ada_b.reshape(depth, 1, n))


def _inproj_kernel(x_ref, mod_ref, n1_ref, win_ref, qn_ref, wuq_ref, kvn_ref, wukv_ref,
                   gqn_ref, gkn_ref, cq_ref, sq_ref, tk_ref,
                   sgn_ref, sgw_ref, sgb_ref, wg_ref, bg_ref,
                   q_out, k_out, v_out, b_out, gla_out, r_out):
    for t in range(x_ref.shape[0] // TM_SUB):
        _inproj_rows(slice(t * TM_SUB, (t + 1) * TM_SUB), x_ref, mod_ref, n1_ref, win_ref, qn_ref, wuq_ref,
                     kvn_ref, wukv_ref, gqn_ref, gkn_ref, cq_ref, sq_ref, tk_ref, sgn_ref, sgw_ref, sgb_ref,
                     wg_ref, bg_ref, q_out, k_out, v_out, b_out, gla_out, r_out)


def _inproj_rows(rs, x_ref, mod_ref, n1_ref, win_ref, qn_ref, wuq_ref, kvn_ref, wukv_ref,
                 gqn_ref, gkn_ref, cq_ref, sq_ref, tk_ref, sgn_ref, sgw_ref, sgb_ref, wg_ref, bg_ref,
                 q_out, k_out, v_out, b_out, gla_out, r_out):
    d = D_MODEL
    tm = rs.stop - rs.start
    x = x_ref[rs, :]
    shift = mod_ref[0, :, 0:d]
    scale = mod_ref[0, :, d:2 * d]
    ms = jnp.mean(x * x, axis=-1, keepdims=True)
    h = (x * lax.rsqrt(ms + EPS) * n1_ref[...]) * (1.0 + scale) + shift
    hb = h.astype(BF16)
    z_mla = _dot(hb, win_ref[:, 0:Z_ZU])
    z_end = _dot(hb, win_ref[:, Z_KPE:Z_WIDTH])
    z_mid = _dot(hb, win_ref[:, Z_ZU:Z_KPE])

    cq = z_mla[:, Z_CQ:Z_CQ + MLA_Q_RANK]
    ckv = z_mla[:, Z_CKV:Z_CKV + MLA_KV_RANK]
    cqn = cq * lax.rsqrt(jnp.mean(cq * cq, axis=-1, keepdims=True) + EPS) * qn_ref[...]
    ckvn = ckv * lax.rsqrt(jnp.mean(ckv * ckv, axis=-1, keepdims=True) + EPS) * kvn_ref[...]
    qa = _dot(cqn.astype(BF16), wuq_ref[...])
    kv = _dot(ckvn.astype(BF16), wukv_ref[...])
    npe = MLA_HEADS * MLA_NOPE
    rw = MLA_HEADS * MLA_ROPE
    q_pe = qa[:, npe:npe + rw]
    q_rot = q_pe * cq_ref[rs, :] + qa[:, npe + rw:npe + 2 * rw] * sq_ref[rs, :]
    zk = z_end[:, 0:2 * MLA_ROPE]
    tkz = zk * tk_ref[rs, :]
    k_rot = tkz[:, 0:MLA_ROPE] + tkz[:, MLA_ROPE:2 * MLA_ROPE]
    k_pe = zk[:, 0:MLA_ROPE]
    ss_kpe = jnp.sum(k_pe * k_pe, axis=-1, keepdims=True)
    gqn = gqn_ref[...]
    gkn = gkn_ref[...]
    for hh in range(MLA_HEADS):
        qn_h = qa[:, hh * MLA_NOPE:(hh + 1) * MLA_NOPE]
        qp_h = q_pe[:, hh * MLA_ROPE:(hh + 1) * MLA_ROPE]
        ssq = jnp.sum(qn_h * qn_h, axis=-1, keepdims=True) + jnp.sum(qp_h * qp_h, axis=-1, keepdims=True)
        rq = lax.rsqrt(ssq * (1.0 / MLA_QK) + EPS) * (MLA_SCALE * LOG2_E)
        q_h = jnp.concatenate([qn_h * gqn * rq, q_rot[:, hh * MLA_ROPE:(hh + 1) * MLA_ROPE] * rq], axis=-1)
        q_out[0, hh, rs, :] = q_h.astype(BF16)
        kn_h = kv[:, hh * MLA_NOPE:(hh + 1) * MLA_NOPE]
        ssk = jnp.sum(kn_h * kn_h, axis=-1, keepdims=True) + ss_kpe
        rk = lax.rsqrt(ssk * (1.0 / MLA_QK) + EPS)
        k_h = jnp.concatenate([kn_h * gkn * rk, k_rot * rk], axis=-1)
        k_out[0, hh, rs, :] = k_h.astype(BF16)
        v_h = kv[:, npe + hh * MLA_V:npe + (hh + 1) * MLA_V]
        v_out[0, hh, rs, :] = jnp.concatenate([v_h, jnp.ones_like(v_h)], axis=-1).astype(BF16)

    u = _gelu_tanh(z_mid[:, 0:SGU_WIDTH])
    vv = _gelu_tanh(z_mid[:, Z_ZV - Z_ZU:Z_ZV - Z_ZU + SGU_WIDTH])
    sgn = sgn_ref[...]
    for c in range(tm // SGU_CHUNK):
        rows = slice(c * SGU_CHUNK, (c + 1) * SGU_CHUNK)
        outs = []
        for g in range(SGU_GROUPS):
            cols = slice(g * SGU_GROUP_DIM, (g + 1) * SGU_GROUP_DIM)
            seg = vv[rows, cols]
            vn = seg * lax.rsqrt(jnp.mean(seg * seg, axis=-1, keepdims=True) + EPS) * sgn[:, cols]
            outs.append(_dot(sgw_ref[g], vn.astype(BF16)))
        sp = jnp.concatenate(outs, axis=-1) + sgb_ref[...]
        b_out[rs.start + c * SGU_CHUNK:rs.start + (c + 1) * SGU_CHUNK, :] = (u[rows, :] * sp).astype(BF16)

    gl = _dot(z_end[:, Z_GG - Z_KPE:Z_GG - Z_KPE + LANES].astype(BF16), wg_ref[...]) + bg_ref[...]
    gdec = _log_sigmoid(gl) * (1.0 / GLA_GATE_NORM)
    gla_out[rs, 0:2 * GLA_QK_W] = z_mid[:, Z_GQ - Z_ZU:Z_GQ - Z_ZU + 2 * GLA_QK_W]
    gla_out[rs, 2 * GLA_QK_W:4 * GLA_QK_W] = gdec
    gla_out[rs, 4 * GLA_QK_W:4 * GLA_QK_W + GLA_V_W] = z_mid[:, Z_GV - Z_ZU:Z_GV - Z_ZU + GLA_V_W]
    r_out[rs, :] = z_mid[:, Z_GR - Z_ZU:Z_GR - Z_ZU + GLA_V_W]


def _inproj(x2, n_batch, mod_rows, mod_row0, lw, tabs):
    n_total = x2.shape[0]
    n_tok = n_total // n_batch
    tm = min(TM_IN, n_tok)
    tpb = n_tok // tm
    tab_blocks = tabs["cq"].shape[0] // tm
    if mod_row0 == 0:
        mod_map = lambda i: (i // tpb, 0, 0)
    else:
        mod_map = lambda i: (mod_row0, 0, 0)
    const2 = lambda i: (0, 0)
    tab_map = lambda i: (i % tab_blocks, 0)
    tok_map = lambda i: (i, 0)
    head_map = lambda i: (i // tpb, 0, i % tpb, 0)
    in_specs = [
        pl.BlockSpec((tm, D_MODEL), tok_map),
        pl.BlockSpec((1, 1, 6 * D_MODEL), mod_map),
        pl.BlockSpec((1, D_MODEL), const2),
        pl.BlockSpec((D_MODEL, Z_WIDTH), const2, pipeline_mode=pl.Buffered(1)),
        pl.BlockSpec((1, MLA_Q_RANK), const2),
        pl.BlockSpec((MLA_Q_RANK, 2048), const2, pipeline_mode=pl.Buffered(1)),
        pl.BlockSpec((1, MLA_KV_RANK), const2),
        pl.BlockSpec((MLA_KV_RANK, 2048), const2, pipeline_mode=pl.Buffered(1)),
        pl.BlockSpec((1, MLA_NOPE), const2),
        pl.BlockSpec((1, MLA_NOPE), const2),
        pl.BlockSpec((tm, MLA_HEADS * MLA_ROPE), tab_map),
        pl.BlockSpec((tm, MLA_HEADS * MLA_ROPE), tab_map),
        pl.BlockSpec((tm, 2 * MLA_ROPE), tab_map),
        pl.BlockSpec((1, SGU_WIDTH), const2),
        pl.BlockSpec((SGU_GROUPS, SGU_CHUNK, SGU_CHUNK), lambda i: (0, 0, 0)),
        pl.BlockSpec((SGU_CHUNK, SGU_WIDTH), const2),
        pl.BlockSpec((LANES, 2 * GLA_QK_W), const2),
        pl.BlockSpec((1, 2 * GLA_QK_W), const2),
    ]
    out_shape = [
        jax.ShapeDtypeStruct((n_batch, MLA_HEADS, n_tok, MLA_QK), BF16),
        jax.ShapeDtypeStruct((n_batch, MLA_HEADS, n_tok, MLA_QK), BF16),
        jax.ShapeDtypeStruct((n_batch, MLA_HEADS, n_tok, MLA_V_AUG), BF16),
        jax.ShapeDtypeStruct((n_total, SGU_WIDTH), BF16),
        jax.ShapeDtypeStruct((n_total, 4 * GLA_QK_W + GLA_V_W), F32),
        jax.ShapeDtypeStruct((n_total, GLA_V_W), F32),
    ]
    out_specs = [
        pl.BlockSpec((1, MLA_HEADS, tm, MLA_QK), head_map),
        pl.BlockSpec((1, MLA_HEADS, tm, MLA_QK), head_map),
        pl.BlockSpec((1, MLA_HEADS, tm, MLA_V_AUG), head_map),
        pl.BlockSpec((tm, SGU_WIDTH), tok_map),
        pl.BlockSpec((tm, 4 * GLA_QK_W + GLA_V_W), tok_map),
        pl.BlockSpec((tm, GLA_V_W), tok_map),
    ]
    return pl.pallas_call(
        _inproj_kernel,
        grid=(n_total // tm,),
        in_specs=in_specs,
        out_specs=out_specs,
        out_shape=out_shape,
        compiler_params=_cparams(("arbitrary",), VMEM_LIMIT_INPROJ),
        name="inproj",
    )(x2, mod_rows, lw["n1"], lw["w_in"], lw["q_norm"], lw["w_uq"], lw["kv_norm"], lw["w_ukv"],
      lw["gq_nope"], lw["gk_nope"], tabs["cq"], tabs["sq"], tabs["tk"],
      lw["sgu_norm"], lw["sgu_w"], lw["sgu_b"], lw["wg"], lw["bg"])


_GLA_LEVELS = (32, 16, 8, 4, 2, 1)
_N_LEV = len(_GLA_LEVELS)
_G_SPLIT = 3
_R_QI = _N_LEV * GLA_CHUNK
_R_KS = _R_QI + GLA_CHUNK
_R_TOT = _R_KS + GLA_CHUNK
_M_ROWS = _R_TOT + 8


def _gla_constants():
    c = GLA_CHUNK
    t = np.arange(c)[:, None]
    j = np.arange(c)[None, :]
    lev, masks = [], []
    for m in _GLA_LEVELS:
        upper = (t // m) % 2 == 1
        bound = (t // (2 * m)) * 2 * m + m - 1
        lev.append(np.where(upper, (j > bound) & (j <= t), (j > t) & (j <= bound)).astype(np.float32))
        s = j
        masks.append(((t // (2 * m) == s // (2 * m)) & upper & ((s // m) % 2 == 0)).astype(np.float32))
    masks.append((t == j).astype(np.float32))
    mqi = (j <= t).astype(np.float32)
    mks = (j > t).astype(np.float32)
    tot = np.ones((8, c), np.float32)
    flip = lambda a: a[::-1, ::-1]
    m_f = np.concatenate(lev + [mqi, mks, tot], axis=0)
    m_b = np.concatenate([flip(a) for a in lev] + [flip(mqi), flip(mks), tot], axis=0)
    mask_f = np.stack([np.tile(a, (GLA_HEADS, 1)) for a in masks])
    mask_b = np.stack([np.tile(flip(a), (GLA_HEADS, 1)) for a in masks])
    m_all = np.tile(np.stack([m_f, m_b]), (1, 1, _G_SPLIT))
    return m_all, np.stack([mask_f, mask_b])


def _stack_heads(a, lane_head):
    return jnp.concatenate([jnp.where(lane_head == hh, a, jnp.zeros_like(a)) for hh in range(GLA_HEADS)], axis=0)


def _gla_chunk(q, k, g, v, st, mall, masks, lane_head):
    c = GLA_CHUNK
    g_hi = g.astype(BF16)
    r1 = g - g_hi.astype(F32)
    g_mid = r1.astype(BF16)
    g_lo = (r1 - g_mid.astype(F32)).astype(BF16)
    e = _dot(mall, jnp.concatenate([g_hi, g_mid, g_lo], axis=0))
    a = masks[_N_LEV] * _dot_nt(_stack_heads(q.astype(BF16), lane_head), k.astype(BF16))
    for lev in range(_N_LEV):
        ex = jnp.exp(e[lev * c:(lev + 1) * c])
        a = a + masks[lev] * _dot_nt(_stack_heads((q * ex).astype(BF16), lane_head), (k * ex).astype(BF16))
    qi = (q * jnp.exp(e[_R_QI:_R_QI + c])).astype(BF16)
    o_inter = _dot_nt(_stack_heads(qi, lane_head), st.astype(BF16))
    vb = v.astype(BF16)
    a16 = a.astype(BF16)
    outs = []
    for hh in range(GLA_HEADS):
        rows = slice(hh * c, (hh + 1) * c)
        outs.append(o_inter[rows] + _dot(a16[rows], vb[:, hh * GLA_DV:(hh + 1) * GLA_DV]))
    o = jnp.concatenate(outs, axis=-1)
    kd = (k * jnp.exp(e[_R_KS:_R_KS + c])).astype(BF16)
    upd = _dot_tn(vb, kd)
    st_new = st * jnp.exp(e[_R_TOT:_R_TOT + 1])
    for hh in range(GLA_HEADS):
        st_new = st_new + jnp.where(lane_head == hh, upd[hh * GLA_DV:(hh + 1) * GLA_DV], 0.0)
    return o, st_new


def _gla_kernel(fwd_ref, bwd_ref, s0_ref, mall_ref, mask_ref, of_ref, ob_ref, sfin_ref, st_ref):
    step = pl.program_id(0)
    n_batch = fwd_ref.shape[0]

    @pl.when(step == 0)
    def _():
        st_ref[...] = s0_ref[...]

    lane_head = lax.broadcasted_iota(jnp.int32, (1, GLA_QK_W), 1) // GLA_DK
    w = GLA_QK_W
    for b in range(n_batch):
        for direction, (in_ref, o_ref) in enumerate(((fwd_ref, of_ref), (bwd_ref, ob_ref))):
            mall = mall_ref[direction]
            masks = [mask_ref[direction, i] for i in range(_N_LEV + 1)]
            q = in_ref[b, :, 0:w]
            k = in_ref[b, :, w:2 * w]
            g = in_ref[b, :, (2 + direction) * w:(3 + direction) * w]
            v = in_ref[b, :, 4 * w:4 * w + GLA_V_W]
            o, st_new = _gla_chunk(q, k, g, v, st_ref[direction, b], mall, masks, lane_head)
            o_ref[b] = o
            st_ref[direction, b] = st_new

    @pl.when(step == pl.num_programs(0) - 1)
    def _():
        sfin_ref[...] = st_ref[...]


def _gla(gla_in, s0, consts):
    n_batch, n_tok, width = gla_in.shape
    nc = n_tok // GLA_CHUNK
    mall, masks = consts
    st_shape = (2, n_batch, GLA_DV, GLA_QK_W)
    return pl.pallas_call(
        _gla_kernel,
        grid=(nc,),
        in_specs=[
            pl.BlockSpec((n_batch, GLA_CHUNK, width), lambda c: (0, c, 0)),
            pl.BlockSpec((n_batch, GLA_CHUNK, width), lambda c: (0, nc - 1 - c, 0)),
            pl.BlockSpec(st_shape, lambda c: (0, 0, 0, 0)),
            pl.BlockSpec(mall.shape, lambda c: (0, 0, 0)),
            pl.BlockSpec(masks.shape, lambda c: (0, 0, 0, 0)),
        ],
        out_specs=[
            pl.BlockSpec((n_batch, GLA_CHUNK, GLA_V_W), lambda c: (0, c, 0)),
            pl.BlockSpec((n_batch, GLA_CHUNK, GLA_V_W), lambda c: (0, nc - 1 - c, 0)),
            pl.BlockSpec(st_shape, lambda c: (0, 0, 0, 0)),
        ],
        out_shape=[
            jax.ShapeDtypeStruct((n_batch, n_tok, GLA_V_W), F32),
            jax.ShapeDtypeStruct((n_batch, n_tok, GLA_V_W), F32),
            jax.ShapeDtypeStruct(st_shape, F32),
        ],
        scratch_shapes=[pltpu.VMEM(st_shape, F32)],
        compiler_params=_cparams(("arbitrary",)),
        name="gla",
    )(gla_in, gla_in, s0, mall, masks)


def _attn_kernel(*refs, n_seg, n_cast):
    q_ref = refs[0]
    k_refs = refs[1:1 + n_seg]
    v_refs = refs[1 + n_seg:1 + 2 * n_seg]
    cast_in = refs[1 + 2 * n_seg:1 + 2 * n_seg + n_cast]
    o_ref = refs[1 + 2 * n_seg + n_cast]
    cast_out = refs[2 + 2 * n_seg + n_cast:]
    for src, dst in zip(cast_in, cast_out):
        dst[...] = src[...].astype(BF16)
    q = q_ref[0, 0]
    n_main = k_refs[0].shape[2]
    tk = min(TK_ATTN, n_main)
    groups = [[(0, slice(c * tk, (c + 1) * tk))] for c in range(n_main // tk)]
    groups[0] += [(i, slice(0, k_refs[i].shape[2])) for i in range(1, n_seg)]
    m = acc = None
    for group in groups:
        scores = [_dot_nt(q, k_refs[i][0, 0, keys, :]) for i, keys in group]
        m_g = functools.reduce(jnp.maximum, [jnp.max(s, axis=-1, keepdims=True) for s in scores])
        m_new = m_g if m is None else jnp.maximum(m, m_g)
        acc_g = functools.reduce(jnp.add, [_dot(jnp.exp2(s - m_new).astype(BF16), v_refs[i][0, 0, keys, :])
                                           for s, (i, keys) in zip(scores, group)])
        acc = acc_g if m is None else jnp.exp2(m - m_new) * acc + acc_g
        m = m_new
    o_ref[0] = (acc[:, 0:MLA_V] / acc[:, MLA_V:MLA_V_AUG]).astype(BF16)


def _attention(q, ks, vs, casts=(), cast_layer=0):
    n_batch, n_heads, nq, _ = q.shape
    tq = min(TQ_ATTN, nq)
    n_seg = len(ks)
    n_q = nq // tq
    n_steps = n_batch * n_heads * n_q
    in_specs = [pl.BlockSpec((1, 1, tq, MLA_QK), lambda b, h, i: (b, h, i, 0))]
    for k in ks:
        in_specs.append(pl.BlockSpec((1, 1, k.shape[2], MLA_QK), lambda b, h, i: (b, h, 0, 0)))
    for v in vs:
        in_specs.append(pl.BlockSpec((1, 1, v.shape[2], MLA_V_AUG), lambda b, h, i: (b, h, 0, 0)))
    out_specs = [pl.BlockSpec((1, tq, MLA_V), lambda b, h, i: (b, i, h))]
    out_shape = [jax.ShapeDtypeStruct((n_batch, nq, n_heads * MLA_V), BF16)]
    slabs = []
    for w in casts:
        depth, cols = w.shape[0], w.shape[-1]
        rows = w.size // (depth * cols * n_steps)
        assert rows * depth * cols * n_steps == w.size and rows % BF16_ROWS == 0
        slabs.append(w.reshape(depth * n_steps, rows, cols))
        in_specs.append(pl.BlockSpec(
            (1, rows, cols), lambda b, h, i: (cast_layer * n_steps + (b * n_heads + h) * n_q + i, 0, 0)))
        out_specs.append(pl.BlockSpec((1, rows, cols), lambda b, h, i: ((b * n_heads + h) * n_q + i, 0, 0)))
        out_shape.append(jax.ShapeDtypeStruct((n_steps, rows, cols), BF16))
    outs = pl.pallas_call(
        functools.partial(_attn_kernel, n_seg=n_seg, n_cast=len(casts)),
        grid=(n_batch, n_heads, n_q),
        in_specs=in_specs,
        out_specs=out_specs,
        out_shape=out_shape,
        compiler_params=_cparams(("arbitrary", "arbitrary", "arbitrary")),
        name="attn",
    )(q, *ks, *vs, *slabs)
    return [outs[0]] + [o.reshape(w.shape[1:]) for o, w in zip(outs[1:], casts)]


def _outproj_kernel(a_ref, b_ref, of_ref, ob_ref, r_ref, x_ref, mod_ref, gon_ref, n2_ref, wout_ref,
                    wr_hi_ref, wr_lo_ref, br_ref, lst_ref, sl_ref,
                    x1_out, h2_out, meta_out, cnt_out):
    d = D_MODEL
    tm = x_ref.shape[0]
    gate1 = mod_ref[0, :, 2 * d:3 * d]
    shift2 = mod_ref[0, :, 3 * d:4 * d]
    scale2 = mod_ref[0, :, 4 * d:5 * d]

    og = of_ref[...] + ob_ref[...]
    gon = gon_ref[...]
    parts = []
    for hh in range(GLA_HEADS):
        seg = og[:, hh * GLA_DV:(hh + 1) * GLA_DV]
        parts.append(seg * lax.rsqrt(jnp.mean(seg * seg, axis=-1, keepdims=True) + EPS) * gon)
    cg = jnp.concatenate(parts, axis=-1) * _silu(r_ref[...])
    na = MLA_HEADS * MLA_V
    y = _dot(a_ref[...], wout_ref[0:na, :])
    y = y + _dot(b_ref[...], wout_ref[na:na + SGU_WIDTH, :])
    y = y + _dot(cg.astype(BF16), wout_ref[na + SGU_WIDTH:na + SGU_WIDTH + GLA_V_W, :])
    x1 = x_ref[...] + gate1 * y
    x1_out[...] = x1

    ms = jnp.mean(x1 * x1, axis=-1, keepdims=True)
    h2 = (x1 * lax.rsqrt(ms + EPS) * n2_ref[...]) * (1.0 + scale2) + shift2
    h_hi = h2.astype(BF16)
    h2_out[...] = h_hi
    h_lo = (h2 - h_hi.astype(F32)).astype(BF16)

    wr_hi = wr_hi_ref[...]
    lt2 = _dot_nt(jnp.concatenate([wr_hi, wr_lo_ref[...]], axis=0), h_hi)
    lt = lt2[0:32] + lt2[32:64] + _dot_nt(wr_hi, h_lo) + br_ref[:, 0:1]
    grp = [lt[MOE_EXPERTS + g:MOE_EXPERTS + g + 1, :] for g in range(MOE_GROUPS)]
    gmax = functools.reduce(jnp.maximum, grp)
    gsel = jnp.full(gmax.shape, MOE_GROUPS - 1, jnp.int32)
    for g in range(MOE_GROUPS - 2, -1, -1):
        gsel = jnp.where(grp[g] == gmax, g, gsel)
    gsum = functools.reduce(jnp.add, [jnp.exp(gl - gmax) for gl in grp])
    g_w = 1.0 / gsum
    e_in = []
    for j in range(MOE_PER_GROUP):
        ej = lt[j:j + 1, :]
        for g in range(1, MOE_GROUPS):
            ej = jnp.where(gsel == g, lt[g * MOE_PER_GROUP + j:g * MOE_PER_GROUP + j + 1, :], ej)
        e_in.append(ej)
    m1 = functools.reduce(jnp.maximum, e_in)
    i1 = jnp.full(m1.shape, MOE_PER_GROUP - 1, jnp.int32)
    for j in range(MOE_PER_GROUP - 2, -1, -1):
        i1 = jnp.where(e_in[j] == m1, j, i1)
    e_rest = [jnp.where(i1 == j, -jnp.inf, e_in[j]) for j in range(MOE_PER_GROUP)]
    m2 = functools.reduce(jnp.maximum, e_rest)
    i2 = jnp.full(m2.shape, MOE_PER_GROUP - 1, jnp.int32)
    for j in range(MOE_PER_GROUP - 2, -1, -1):
        i2 = jnp.where(e_rest[j] == m2, j, i2)
    p2 = jnp.exp(m2 - m1)
    w1 = g_w / (1.0 + p2)
    w2 = g_w * p2 / (1.0 + p2)
    eid1 = gsel * MOE_PER_GROUP + i1
    eid2 = gsel * MOE_PER_GROUP + i2

    e_iota = lax.broadcasted_iota(jnp.int32, (MOE_EXPERTS, tm), 0)
    hit1 = e_iota == eid1
    hit2 = e_iota == eid2
    onehot = jnp.where(hit1, 1.0, 0.0) + jnp.where(hit2, 1.0, 0.0)
    rank = _dot(onehot.astype(BF16), lst_ref[...])
    counts = jnp.sum(onehot, axis=1, keepdims=True)
    padded = jnp.ceil(counts * (1.0 / BF16_ROWS)) * BF16_ROWS
    padded_l = jnp.broadcast_to(padded, (MOE_EXPERTS, LANES))
    off = _dot(sl_ref[...], padded_l.astype(BF16))
    pos_e = off[:, 0:1] + rank
    pos1 = jnp.sum(jnp.where(hit1, pos_e, 0.0), axis=0, keepdims=True)
    pos2 = jnp.sum(jnp.where(hit2, pos_e, 0.0), axis=0, keepdims=True)
    zero = jnp.zeros_like(pos1)
    meta_out[0] = jnp.concatenate(
        [pos1, pos2, w1, w2, eid1.astype(F32), eid2.astype(F32), zero, zero], axis=0)
    cnt_out[0] = padded_l


def _outproj(a, b, o_f, o_b, r, x2, n_batch, mod_rows, mod_row0, lw, rconst):
    n_total = x2.shape[0]
    n_tok = n_total // n_batch
    tm = T_LOC
    tpb = n_tok // tm
    if mod_row0 == 0:
        mod_map = lambda i: (i // tpb, 0, 0)
    else:
        mod_map = lambda i: (mod_row0, 0, 0)
    const2 = lambda i: (0, 0)
    tok_map = lambda i: (i, 0)
    n_tiles = n_total // tm
    in_specs = [
        pl.BlockSpec((tm, MLA_HEADS * MLA_V), tok_map),
        pl.BlockSpec((tm, SGU_WIDTH), tok_map),
        pl.BlockSpec((tm, GLA_V_W), tok_map),
        pl.BlockSpec((tm, GLA_V_W), tok_map),
        pl.BlockSpec((tm, GLA_V_W), tok_map),
        pl.BlockSpec((tm, D_MODEL), tok_map),
        pl.BlockSpec((1, 1, 6 * D_MODEL), mod_map),
        pl.BlockSpec((1, GLA_DV), const2),
        pl.BlockSpec((1, D_MODEL), const2),
        pl.BlockSpec((D_MODEL, D_MODEL), const2, pipeline_mode=pl.Buffered(1)),
        pl.BlockSpec((32, D_MODEL), const2),
        pl.BlockSpec((32, D_MODEL), const2),
        pl.BlockSpec((32, LANES), const2),
        pl.BlockSpec((tm, tm), const2),
        pl.BlockSpec((MOE_EXPERTS, MOE_EXPERTS), const2),
    ]
    out_shape = [
        jax.ShapeDtypeStruct((n_total, D_MODEL), F32),
        jax.ShapeDtypeStruct((n_total, D_MODEL), BF16),
        jax.ShapeDtypeStruct((n_tiles, 8, tm), F32),
        jax.ShapeDtypeStruct((n_tiles, MOE_EXPERTS, LANES), F32),
    ]
    out_specs = [
        pl.BlockSpec((tm, D_MODEL), tok_map),
        pl.BlockSpec((tm, D_MODEL), tok_map),
        pl.BlockSpec((1, 8, tm), lambda i: (i, 0, 0)),
        pl.BlockSpec((1, MOE_EXPERTS, LANES), lambda i: (i, 0, 0)),
    ]
    return pl.pallas_call(
        _outproj_kernel,
        grid=(n_tiles,),
        in_specs=in_specs,
        out_specs=out_specs,
        out_shape=out_shape,
        compiler_params=_cparams(("arbitrary",)),
        name="outproj",
    )(a, b, o_f, o_b, r, x2, mod_rows, lw["gla_out_norm"], lw["n2"], lw["w_out"],
      lw["wr_hi"], lw["wr_lo"], lw["br"], rconst["lst"], rconst["sl"])


def _piece_copies(src_ref, dst_ref, sem, src_row, dst_row, n_rows):
    copies = []
    for size in _PIECE_SIZES:
        done = n_rows & ~(2 * size - 1)
        pred = (n_rows & size) != 0
        s0 = pl.multiple_of(src_row + done, BF16_ROWS)
        d0 = pl.multiple_of(dst_row + done, BF16_ROWS)
        cp = pltpu.make_async_copy(src_ref.at[pl.ds(s0, size)], dst_ref.at[pl.ds(d0, size)], sem)
        copies.append((pred, cp))
    return copies


def _list_base(tile, s):
    return tile * len(_PIECE_SIZES) + s


def _start_listed_pieces(src_ref, dst_ref, sem, src_tab, dst_tab, n_tab, tile):
    for s, size in enumerate(_PIECE_SIZES):
        base = _list_base(tile, s)

        def body(i, carry, size=size, base=base):
            idx = base * MOE_EXPERTS + i
            s0 = pl.multiple_of(src_tab[idx], BF16_ROWS)
            d0 = pl.multiple_of(dst_tab[idx], BF16_ROWS)
            pltpu.make_async_copy(src_ref.at[pl.ds(s0, size)], dst_ref.at[pl.ds(d0, size)], sem).start()
            return carry

        lax.fori_loop(0, n_tab[base], body, 0)


def _wait_listed_pieces(src_ref, dst_ref, sem, n_tab, tile):
    for s, size in enumerate(_PIECE_SIZES):
        def body(i, carry, size=size):
            pltpu.make_async_copy(src_ref.at[pl.ds(0, size)], dst_ref.at[pl.ds(0, size)], sem).wait()
            return carry

        lax.fori_loop(0, n_tab[_list_base(tile, s)], body, 0)


def _stream_starts(stream_tiles):
    return [sum(stream_tiles[:s]) for s in range(len(stream_tiles))]


def _stream_block(i, start, n):
    return jnp.clip(i - start, 0, n - 1)


def _dispatch_kernel(ploc_tab, pdst_tab, pn_tab, ztab_dst, ztab_cnt, n_used, *refs, stream_tiles):
    n_streams = len(stream_tiles)
    h2_refs = refs[:n_streams]
    meta_ref, xs_ref, xloc_ref, zeros_ref, sems = refs[n_streams:]
    j = pl.program_id(0)
    n_tiles = pl.num_programs(0)
    tm = h2_refs[0].shape[0]
    slot = j % 2

    @pl.when(j == 0)
    def _():
        zeros_ref[...] = jnp.zeros_like(zeros_ref)

        def fill(t, carry):
            cp = pltpu.make_async_copy(zeros_ref, xs_ref.at[pl.ds(pl.multiple_of(t * TM_FFN, TM_FFN), TM_FFN)],
                                       sems.at[0])
            cp.start()
            cp.wait()
            return carry
        lax.fori_loop(n_used[0], xs_ref.shape[0] // TM_FFN, fill, 0)
        zcopies = []
        for e in range(MOE_EXPERTS):
            zcopies += _piece_copies(zeros_ref, xs_ref, sems.at[0], 0, ztab_dst[e], ztab_cnt[e])
        for pred, cp in zcopies:
            @pl.when(pred)
            def _(cp=cp):
                cp.start()
        for pred, cp in zcopies:
            @pl.when(pred)
            def _(cp=cp):
                cp.wait()

    pos1 = meta_ref[0, 0:1, :]
    pos2 = meta_ref[0, 1:2, :]
    s_iota = lax.broadcasted_iota(jnp.int32, (SLOTS, tm), 0).astype(F32)
    perm = jnp.where((s_iota == pos1) | (s_iota == pos2), 1.0, 0.0).astype(BF16)
    h2 = h2_refs[0][...]
    for s, start in list(enumerate(_stream_starts(stream_tiles)))[1:]:
        h2 = jnp.where(j >= start, h2_refs[s][...], h2)
    xloc_ref[slot] = _dot(perm, h2).astype(BF16)

    @pl.when(j > 0)
    def _():
        _wait_listed_pieces(xloc_ref.at[1 - slot], xs_ref, sems.at[1 - slot], pn_tab, j - 1)

    _start_listed_pieces(xloc_ref.at[slot], xs_ref, sems.at[slot], ploc_tab, pdst_tab, pn_tab, j)

    @pl.when(j == n_tiles - 1)
    def _():
        _wait_listed_pieces(xloc_ref.at[slot], xs_ref, sems.at[slot], pn_tab, j)


def _dispatch(h2s, meta, tabs, r_max):
    n_tiles = meta.shape[0]
    tm = T_LOC
    stream_tiles = tuple(h.shape[0] // tm for h in h2s)
    assert sum(stream_tiles) == n_tiles
    h2_specs = [pl.BlockSpec((tm, D_MODEL), lambda i, *_, start=start, n=n: (_stream_block(i, start, n), 0))
                for start, n in zip(_stream_starts(stream_tiles), stream_tiles)]
    grid_spec = pltpu.PrefetchScalarGridSpec(
        num_scalar_prefetch=6,
        grid=(n_tiles,),
        in_specs=h2_specs + [pl.BlockSpec((1, 8, tm), lambda i, *_: (i, 0, 0))],
        out_specs=pl.BlockSpec(memory_space=pl.ANY),
        scratch_shapes=[pltpu.VMEM((2, SLOTS, D_MODEL), BF16), pltpu.VMEM((TM_FFN, D_MODEL), BF16),
                        pltpu.SemaphoreType.DMA((2,))],
    )
    return pl.pallas_call(
        functools.partial(_dispatch_kernel, stream_tiles=stream_tiles),
        grid_spec=grid_spec,
        out_shape=jax.ShapeDtypeStruct((r_max, D_MODEL), BF16),
        compiler_params=_cparams(("arbitrary",)),
        name="dispatch",
    )(tabs["ploc"], tabs["pdst"], tabs["pn"], tabs["zdst"], tabs["zcnt"], tabs["n_used"], *h2s, meta)


def _ffn_kernel(tile_e, tile_blk, n_used, x_ref, w1_ref, w3_ref, w2_ref, y_ref):
    i = pl.program_id(0)

    @pl.when(i < n_used[0])
    def _():
        x = x_ref[...]
        acc = None
        for f in range(D_EXPERT // F_CHUNK):
            cols = slice(f * F_CHUNK, (f + 1) * F_CHUNK)
            h1 = _dot(x, w1_ref[0, :, cols])
            h3 = _dot(x, w3_ref[0, :, cols])
            act = (_silu(h1) * h3).astype(BF16)
            part = _dot(act, w2_ref[0, cols, :])
            acc = part if acc is None else acc + part
        y_ref[...] = acc.astype(BF16)

    @pl.when(i >= n_used[0])
    def _():
        y_ref[...] = jnp.zeros_like(y_ref)


def _ffn(xs, w1, w3, w2, tabs):
    r_max = xs.shape[0]
    n_mt = r_max // TM_FFN
    grid_spec = pltpu.PrefetchScalarGridSpec(
        num_scalar_prefetch=3,
        grid=(n_mt,),
        in_specs=[
            pl.BlockSpec((TM_FFN, D_MODEL), lambda i, te, tb, nu: (tb[i], 0)),
            pl.BlockSpec((1, D_MODEL, D_EXPERT), lambda i, te, tb, nu: (te[i], 0, 0)),
            pl.BlockSpec((1, D_MODEL, D_EXPERT), lambda i, te, tb, nu: (te[i], 0, 0)),
            pl.BlockSpec((1, D_EXPERT, D_MODEL), lambda i, te, tb, nu: (te[i], 0, 0)),
        ],
        out_specs=pl.BlockSpec((TM_FFN, D_MODEL), lambda i, te, tb, nu: (i, 0)),
    )
    return pl.pallas_call(
        _ffn_kernel,
        grid_spec=grid_spec,
        out_shape=jax.ShapeDtypeStruct((r_max, D_MODEL), BF16),
        compiler_params=_cparams(("arbitrary",)),
        name="ffn",
    )(tabs["tile_e"], tabs["tile_blk"], tabs["n_used"], xs, w1, w3, w2)


def _combine_kernel(ploc_tab, pdst_tab, pn_tab, y_ref, metac_ref, mod_ref, *refs, stream_tiles):
    n_streams = len(stream_tiles)
    x1_refs = refs[:n_streams]
    out_refs = refs[n_streams:2 * n_streams]
    yloc_ref, sems = refs[2 * n_streams:]
    j = pl.program_id(0)
    n_tiles = pl.num_programs(0)
    d = D_MODEL
    tm = x1_refs[0].shape[0]
    slot = j % 2

    @pl.when(j == 0)
    def _():
        yloc_ref[...] = jnp.zeros_like(yloc_ref)
        _start_listed_pieces(y_ref, yloc_ref.at[0], sems.at[0], pdst_tab, ploc_tab, pn_tab, 0)

    @pl.when(j + 1 < n_tiles)
    def _():
        _start_listed_pieces(y_ref, yloc_ref.at[1 - slot], sems.at[1 - slot], pdst_tab, ploc_tab, pn_tab, j + 1)

    _wait_listed_pieces(y_ref, yloc_ref.at[slot], sems.at[slot], pn_tab, j)

    mc = metac_ref[0]
    s_iota = lax.broadcasted_iota(jnp.int32, (tm, SLOTS), 1).astype(F32)
    wc = jnp.where(s_iota == mc[:, 0:1], mc[:, 2:3], 0.0) + jnp.where(s_iota == mc[:, 1:2], mc[:, 3:4], 0.0)
    moe = _dot(wc.astype(BF16), yloc_ref[slot])
    gate2 = mod_ref[0, :, 5 * d:6 * d]
    for start, n, x1_ref, out_ref in zip(_stream_starts(stream_tiles), stream_tiles, x1_refs, out_refs):
        @pl.when((j >= start) & (j < start + n))
        def _(x1_ref=x1_ref, out_ref=out_ref):
            out_ref[...] = x1_ref[...] + gate2 * moe


def _combine(y, metac, x1s, n_batch, mod_rows, mod_row0s, tabs):
    tm = T_LOC
    stream_tiles = tuple(x1.shape[0] // tm for x1 in x1s)
    starts = _stream_starts(stream_tiles)
    n_tiles = sum(stream_tiles)

    def mod_map(i, *_):
        row = jnp.int32(0)
        for start, n, x1, row0 in zip(starts, stream_tiles, x1s, mod_row0s):
            tpb = x1.shape[0] // n_batch // tm
            row_s = _stream_block(i, start, n) // tpb if row0 == 0 else row0
            row = jnp.where(i >= start, row_s, row)
        return (row, 0, 0)

    tile_specs = [pl.BlockSpec((tm, D_MODEL), lambda i, *_, start=start, n=n: (_stream_block(i, start, n), 0))
                  for start, n in zip(starts, stream_tiles)]
    grid_spec = pltpu.PrefetchScalarGridSpec(
        num_scalar_prefetch=3,
        grid=(n_tiles,),
        in_specs=[
            pl.BlockSpec(memory_space=pl.ANY),
            pl.BlockSpec((1, tm, 8), lambda i, *_: (i, 0, 0)),
            pl.BlockSpec((1, 1, 6 * D_MODEL), mod_map),
        ] + tile_specs,
        out_specs=tile_specs,
        scratch_shapes=[pltpu.VMEM((2, SLOTS, D_MODEL), BF16), pltpu.SemaphoreType.DMA((2,))],
    )
    return pl.pallas_call(
        functools.partial(_combine_kernel, stream_tiles=stream_tiles),
        grid_spec=grid_spec,
        out_shape=[jax.ShapeDtypeStruct(x1.shape, F32) for x1 in x1s],
        compiler_params=_cparams(("arbitrary",)),
        name="combine",
    )(tabs["ploc"], tabs["pdst"], tabs["pn"], y, metac, mod_rows, *x1s)


def _rope_tables(n_tokens):
    rows = n_tokens // GRID_W
    row = jnp.repeat(jnp.arange(rows, dtype=jnp.int32), GRID_W)
    col = jnp.tile(jnp.arange(GRID_W, dtype=jnp.int32), rows)
    n_freq = MLA_ROPE // 4
    inv_freq = ROPE_THETA ** (-jnp.arange(n_freq, dtype=F32) / n_freq)
    ang_r = row[:, None].astype(F32) * inv_freq
    ang_c = col[:, None].astype(F32) * inv_freq
    cos = jnp.concatenate([jnp.cos(ang_r)] * 2 + [jnp.cos(ang_c)] * 2, axis=-1)
    sin = jnp.concatenate([-jnp.sin(ang_r), jnp.sin(ang_r), -jnp.sin(ang_c), jnp.sin(ang_c)], axis=-1)
    return cos, sin


_SWAP64 = np.concatenate([np.arange(16, 32), np.arange(0, 16), np.arange(48, 64), np.arange(32, 48)])


def _position_tables(cos, sin, q_gain, k_gain):
    gq = q_gain[MLA_NOPE:]
    gk = k_gain[MLA_NOPE:]
    cq = jnp.tile(cos * gq, (1, MLA_HEADS))
    sq = jnp.tile(sin * gq[_SWAP64], (1, MLA_HEADS))
    tk = jnp.concatenate([cos * gk, sin * gk[_SWAP64]], axis=-1)
    return {"cq": cq, "sq": sq, "tk": tk}


def _layer_weights(l, p):
    d = D_MODEL
    o = _IN_OFF
    w_in = p["w_in"][l]
    seg = lambda i: w_in[:, o[i]:o[i + 1]]
    kpe = seg(2)
    w_in_r = jnp.concatenate(
        [seg(0), seg(1), seg(3), seg(4), seg(5) * (GLA_DK ** -0.5), seg(6), seg(7), seg(8),
         kpe, kpe[:, _SWAP64], seg(9), seg(10), jnp.zeros((d, Z_WIDTH - Z_GG - 2 * GLA_GATE_RANK), F32)],
        axis=-1).astype(BF16)
    w_uq = p["mla_w_uq"][l].reshape(MLA_Q_RANK, MLA_HEADS, MLA_QK)
    q_nope = w_uq[:, :, :MLA_NOPE].reshape(MLA_Q_RANK, -1)
    q_pe = w_uq[:, :, MLA_NOPE:]
    w_uq_r = jnp.concatenate([q_nope, q_pe.reshape(MLA_Q_RANK, -1), q_pe[:, :, _SWAP64].reshape(MLA_Q_RANK, -1)],
                             axis=-1).astype(BF16)
    w_ukv = p["mla_w_ukv"][l].reshape(MLA_KV_RANK, MLA_HEADS, MLA_NOPE + MLA_V)
    w_ukv_r = jnp.concatenate([w_ukv[:, :, :MLA_NOPE].reshape(MLA_KV_RANK, -1),
                               w_ukv[:, :, MLA_NOPE:].reshape(MLA_KV_RANK, -1)], axis=-1).astype(BF16)
    wg = jnp.zeros((LANES, 2 * GLA_QK_W), F32)
    wg = wg.at[0:GLA_GATE_RANK, 0:GLA_QK_W].set(p["gla_wg_f"][l])
    wg = wg.at[GLA_GATE_RANK:2 * GLA_GATE_RANK, GLA_QK_W:].set(p["gla_wg_b"][l])
    bg = jnp.concatenate([p["gla_bg_f"][l], p["gla_bg_b"][l]])[None, :]
    sgu_b = jnp.repeat(p["sgu_b"][l].T, SGU_GROUP_DIM, axis=1)
    wr = jnp.zeros((32, d), F32)
    wr = wr.at[0:MOE_EXPERTS].set(p["moe_w_expert"][l].T)
    wr = wr.at[MOE_EXPERTS:MOE_EXPERTS + MOE_GROUPS].set(p["moe_w_group"][l].T)
    wr_hi = wr.astype(BF16)
    wr_lo = (wr - wr_hi.astype(F32)).astype(BF16)
    br = jnp.zeros((32,), F32).at[0:MOE_EXPERTS].set(p["moe_b_expert"][l])
    br = br.at[MOE_EXPERTS:MOE_EXPERTS + MOE_GROUPS].set(p["moe_b_group"][l])
    return {
        "n1": p["norm1_g"][l][None, :], "n2": p["norm2_g"][l][None, :],
        "w_in": w_in_r, "q_norm": p["mla_q_norm"][l][None, :], "kv_norm": p["mla_kv_norm"][l][None, :],
        "w_uq": w_uq_r, "w_ukv": w_ukv_r,
        "gq_nope": p["mla_q_gain"][l][None, :MLA_NOPE], "gk_nope": p["mla_k_gain"][l][None, :MLA_NOPE],
        "sgu_norm": p["sgu_norm"][l].reshape(1, SGU_WIDTH), "sgu_w": p["sgu_w"][l].astype(BF16), "sgu_b": sgu_b,
        "wg": wg.astype(BF16), "bg": bg,
        "gla_out_norm": p["gla_out_norm"][l][None, :], "w_out": p["w_out"][l].astype(BF16),
        "wr_hi": wr_hi, "wr_lo": wr_lo, "br": jnp.broadcast_to(br[:, None], (32, LANES)),
    }


def _router_constants():
    t = np.arange(T_LOC)
    lst = (t[:, None] < t[None, :]).astype(np.float32)
    e = np.arange(MOE_EXPERTS)
    sl = (e[None, :] < e[:, None]).astype(np.float32)
    return {"lst": jnp.asarray(lst, BF16), "sl": jnp.asarray(sl, BF16)}


def _dispatch_tables(cnt, r_max):
    n_tiles = cnt.shape[0]
    cnt = cnt.astype(jnp.int32)
    loc_off = jnp.cumsum(cnt, axis=1) - cnt
    rows_e = jnp.sum(cnt, axis=0)
    tiles_e = (rows_e + TM_FFN - 1) // TM_FFN
    tile_start = jnp.cumsum(tiles_e) - tiles_e
    base_e = tile_start * TM_FFN
    dst = base_e[None, :] + jnp.cumsum(cnt, axis=0) - cnt
    n_used = jnp.sum(tiles_e)
    n_mt = r_max // TM_FFN
    ti = jnp.arange(n_mt, dtype=jnp.int32)
    tile_blk = jnp.minimum(ti, n_used - 1)
    tile_e = jnp.sum(tile_blk[:, None] >= (tile_start + tiles_e)[None, :], axis=1).astype(jnp.int32)
    tile_e = jnp.minimum(tile_e, MOE_EXPERTS - 1)
    zdst = base_e + rows_e
    zcnt = tiles_e * TM_FFN - rows_e
    i32 = lambda a: a.astype(jnp.int32)
    sizes = jnp.asarray(_PIECE_SIZES, jnp.int32)
    has = (cnt[:, :, None] & sizes) != 0
    done = cnt[:, :, None] & ~(2 * sizes - 1)
    pos = jnp.cumsum(has, axis=1) - 1
    pick = has[..., None] & (pos[..., None] == jnp.arange(MOE_EXPERTS))
    listed = lambda a: jnp.sum(jnp.where(pick, (a[:, :, None] + done)[..., None], 0), axis=1)
    return {"ploc": i32(listed(loc_off).reshape(-1)), "pdst": i32(listed(dst).reshape(-1)),
            "pn": i32(jnp.sum(has, axis=1).reshape(-1)), "zdst": i32(zdst), "zcnt": i32(zcnt),
            "tile_e": tile_e, "tile_blk": i32(tile_blk), "n_used": i32(n_used.reshape(1))}


def _moe(h2s, metas, cnts, x1s, n_batch, mod_rows, mod_row0s, lw):
    meta = jnp.concatenate(metas, axis=0)
    cnt = jnp.concatenate(cnts, axis=0)
    n_tiles = meta.shape[0]
    worst_rows = n_tiles * SLOTS + MOE_EXPERTS * (TM_FFN - BF16_ROWS)
    r_max = -(-worst_rows // TM_FFN) * TM_FFN
    tabs = _dispatch_tables(cnt[:, :, 0], r_max)
    xs = _dispatch(h2s, meta, tabs, r_max)
    y = _ffn(xs, lw["w1"], lw["w3"], lw["w2"], tabs)
    metac = jnp.swapaxes(meta, 1, 2)
    return _combine(y, metac, x1s, n_batch, mod_rows, mod_row0s, tabs)


def kernel(x, c, ctx, c_ctx, norm1_g, norm2_g, ada_w, ada_b, w_in, mla_q_norm, mla_w_uq, mla_kv_norm, mla_w_ukv,
           mla_q_gain, mla_k_gain, sgu_norm, sgu_w, sgu_b, gla_wg_f, gla_bg_f, gla_wg_b, gla_bg_b, gla_out_norm,
           w_out, moe_w_group, moe_b_group, moe_w_expert, moe_b_expert, moe_w1, moe_w3, moe_w2):
    p = dict(norm1_g=norm1_g, norm2_g=norm2_g, w_in=w_in, mla_q_norm=mla_q_norm, mla_w_uq=mla_w_uq,
             mla_kv_norm=mla_kv_norm, mla_w_ukv=mla_w_ukv, mla_q_gain=mla_q_gain, mla_k_gain=mla_k_gain,
             sgu_norm=sgu_norm, sgu_w=sgu_w, sgu_b=sgu_b, gla_wg_f=gla_wg_f, gla_bg_f=gla_bg_f,
             gla_wg_b=gla_wg_b, gla_bg_b=gla_bg_b, gla_out_norm=gla_out_norm, w_out=w_out,
             moe_w_group=moe_w_group, moe_b_group=moe_b_group, moe_w_expert=moe_w_expert,
             moe_b_expert=moe_b_expert, moe_w1=moe_w1, moe_w3=moe_w3, moe_w2=moe_w2)
    n_batch, seq, d = x.shape
    n_ctx = ctx.shape[1]
    depth = ada_w.shape[0]
    assert d == D_MODEL and n_batch <= 4 and seq % T_LOC == 0 and seq % TM_IN == 0 and n_ctx % TM_SUB == 0
    assert (n_batch * n_ctx) % T_LOC == 0

    c_rows = jnp.zeros((8, d), F32).at[0:n_batch].set(c).at[4].set(c_ctx)
    mod = _ada(c_rows, ada_w, ada_b)

    cos, sin = _rope_tables(seq)
    ones_c = jnp.ones((min(TM_IN, n_ctx), MLA_ROPE), F32)
    gla_consts_np = _gla_constants()
    gla_consts = (jnp.asarray(gla_consts_np[0], BF16), jnp.asarray(gla_consts_np[1], F32))
    rconst = _router_constants()
    zero_state = jnp.zeros((2, n_batch, GLA_DV, GLA_QK_W), F32)

    x2 = x.reshape(n_batch * seq, d)
    ctx2 = ctx.reshape(n_batch * n_ctx, d)
    for l in range(depth):
        ctx_out = l < depth - 1
        lw = _layer_weights(l, p)
        mod_rows = mod[l].reshape(8, 1, 6 * d)
        tabs_x = _position_tables(cos, sin, mla_q_gain[l], mla_k_gain[l])
        tabs_c = _position_tables(ones_c, 0.0 * ones_c, mla_q_gain[l], mla_k_gain[l])

        q_x, k_x, v_x, b_x, gla_x, r_x = _inproj(x2, n_batch, mod_rows, 0, lw, tabs_x)
        q_c, k_c, v_c, b_c, gla_c, r_c = _inproj(ctx2, n_batch, mod_rows, 4, lw, tabs_c)

        of_c, ob_c, s_ctx = _gla(gla_c.reshape(n_batch, n_ctx, -1), zero_state, gla_consts)
        of_x, ob_x, _ = _gla(gla_x.reshape(n_batch, seq, -1), s_ctx, gla_consts)
        a_x, lw["w1"], lw["w3"], lw["w2"] = _attention(q_x, [k_x, k_c], [v_x, v_c],
                                                       casts=(moe_w1, moe_w3, moe_w2), cast_layer=l)

        x1, h2, meta, cnt = _outproj(a_x.reshape(n_batch * seq, -1), b_x, of_x.reshape(n_batch * seq, -1),
                                     ob_x.reshape(n_batch * seq, -1), r_x, x2, n_batch, mod_rows, 0, lw, rconst)
        if ctx_out:
            a_c, = _attention(q_c, [k_c], [v_c])
            c1, h2c, meta_c, cnt_c = _outproj(a_c.reshape(n_batch * n_ctx, -1), b_c,
                                              of_c.reshape(n_batch * n_ctx, -1), ob_c.reshape(n_batch * n_ctx, -1),
                                              r_c, ctx2, n_batch, mod_rows, 4, lw, rconst)
            x2, ctx2 = _moe([h2, h2c], [meta, meta_c], [cnt, cnt_c], [x1, c1], n_batch, mod_rows, [0, 4], lw)
        else:
            x2, = _moe([h2], [meta], [cnt], [x1], n_batch, mod_rows, [0], lw)
    return x2.reshape(n_batch, seq, d)
```

```python
import functools

import numpy as np
import jax
import jax.numpy as jnp
from jax import lax
from jax.experimental import pallas as pl
from jax.experimental.pallas import tpu as pltpu

F32 = jnp.float32
BF16 = jnp.bfloat16

D_MODEL = 2048
GRID_W = 64
EPS = 1e-6

MLA_HEADS = 8
MLA_Q_RANK = 512
MLA_KV_RANK = 512
MLA_NOPE = 128
MLA_ROPE = 64
MLA_V = 128
MLA_V_AUG = 2 * MLA_V
MLA_QK = MLA_NOPE + MLA_ROPE
MLA_SCALE = MLA_QK ** -0.5
ROPE_THETA = 10000.0
LOG2_E = float(np.log2(np.e))

SGU_WIDTH = 512
SGU_GROUPS = 4
SGU_GROUP_DIM = SGU_WIDTH // SGU_GROUPS
SGU_CHUNK = 128

GLA_HEADS = 4
GLA_DK = 64
GLA_DV = 128
GLA_GATE_RANK = 16
GLA_GATE_NORM = 16.0
GLA_CHUNK = 64
GLA_QK_W = GLA_HEADS * GLA_DK
GLA_V_W = GLA_HEADS * GLA_DV

MOE_GROUPS = 4
MOE_PER_GROUP = 4
MOE_EXPERTS = MOE_GROUPS * MOE_PER_GROUP
D_EXPERT = 1024

_IN_SPLITS = (512, 512, 64, 512, 512, 256, 256, 512, 512, 16, 16)
_IN_OFF = np.concatenate([[0], np.cumsum(_IN_SPLITS)]).tolist()

Z_CQ, Z_CKV, Z_ZU, Z_ZV = 0, 512, 1024, 1536
Z_GQ, Z_GK, Z_GV, Z_GR = 2048, 2304, 2560, 3072
Z_KPE, Z_GG = 3584, 3712
Z_WIDTH = 3840

MLA_Q_COLS = MLA_HEADS * (MLA_NOPE + 2 * MLA_ROPE)
MLA_KV_COLS = MLA_HEADS * (MLA_NOPE + MLA_V)

LANES = 128
SUBLANES = 8
BF16_ROWS = 16
VMEM_LIMIT = 56 * 1024 * 1024
VMEM_LIMIT_INPROJ = 61 * 1024 * 1024

MOD_ROWS = SUBLANES
CTX_MOD_ROW = 4
ROUTER_ROWS = 2 * BF16_ROWS
META_ROWS = SUBLANES

ADA_TN = 1536
TM_IN = 512
TM_SUB = 256
TQ_ATTN = 1024
TK_ATTN = 2048
T_LOC = 512
SLOTS = 2 * T_LOC + MOE_EXPERTS * BF16_ROWS
TM_FFN = 512
F_CHUNK = 256
_PIECE_SIZES = (512, 256, 128, 64, 32, 16)


def _cparams(sem, vmem=VMEM_LIMIT):
    return pltpu.CompilerParams(dimension_semantics=sem, vmem_limit_bytes=vmem)


def _silu(x):
    return x / (1.0 + jnp.exp(-x))


def _gelu_tanh(x):
    return 0.5 * x * (1.0 + jnp.tanh(np.sqrt(2.0 / np.pi).astype(np.float32) * (x + 0.044715 * (x * x * x))))


def _log_sigmoid(x):
    return jnp.minimum(x, 0.0) - jnp.log(1.0 + jnp.exp(-jnp.abs(x)))


def _dot(a, b):
    return jnp.dot(a, b, preferred_element_type=F32)


def _dot_nt(a, b):
    return lax.dot_general(a, b, (((1,), (1,)), ((), ())), preferred_element_type=F32)


def _dot_tn(a, b):
    return lax.dot_general(a, b, (((0,), (0,)), ((), ())), preferred_element_type=F32)


def _ada_kernel(c_ref, w_ref, b_ref, o_ref):
    a = _silu(c_ref[...]).astype(BF16)
    o_ref[0] = _dot(a, w_ref[0].astype(BF16)) + b_ref[0]


def _ada(c_rows, ada_w, ada_b):
    depth, d, n = ada_w.shape
    tn = ADA_TN
    return pl.pallas_call(
        _ada_kernel,
        grid=(depth, n // tn),
        in_specs=[
            pl.BlockSpec((MOD_ROWS, d), lambda l, j: (0, 0)),
            pl.BlockSpec((1, d, tn), lambda l, j: (l, 0, j)),
            pl.BlockSpec((1, 1, tn), lambda l, j: (l, 0, j)),
        ],
        out_specs=pl.BlockSpec((1, MOD_ROWS, tn), lambda l, j: (l, 0, j)),
        out_shape=jax.ShapeDtypeStruct((depth, MOD_ROWS, n), F32),
        compiler_params=_cparams(("arbitrary", "arbitrary")),
        name="ada",
    )(c_rows, ada_w, ada_b.reshape(depth, 1, n))


def _inproj_kernel(x_ref, mod_ref, n1_ref, win_ref, qn_ref, wuq_ref, kvn_ref, wukv_ref,
                   gqn_ref, gkn_ref, cq_ref, sq_ref, tk_ref,
                   sgn_ref, sgw_ref, sgb_ref, wg_ref, bg_ref,
                   q_out, k_out, v_out, b_out, gla_out, r_out):
    for t in range(x_ref.shape[0] // TM_SUB):
        _inproj_rows(slice(t * TM_SUB, (t + 1) * TM_SUB), x_ref, mod_ref, n1_ref, win_ref, qn_ref, wuq_ref,
                     kvn_ref, wukv_ref, gqn_ref, gkn_ref, cq_ref, sq_ref, tk_ref, sgn_ref, sgw_ref, sgb_ref,
                     wg_ref, bg_ref, q_out, k_out, v_out, b_out, gla_out, r_out)


def _inproj_rows(rs, x_ref, mod_ref, n1_ref, win_ref, qn_ref, wuq_ref, kvn_ref, wukv_ref,
                 gqn_ref, gkn_ref, cq_ref, sq_ref, tk_ref, sgn_ref, sgw_ref, sgb_ref, wg_ref, bg_ref,
                 q_out, k_out, v_out, b_out, gla_out, r_out):
    d = D_MODEL
    tm = rs.stop - rs.start
    x = x_ref[rs, :]
    shift = mod_ref[0, :, 0:d]
    scale = mod_ref[0, :, d:2 * d]
    ms = jnp.mean(x * x, axis=-1, keepdims=True)
    h = (x * lax.rsqrt(ms + EPS) * n1_ref[...]) * (1.0 + scale) + shift
    hb = h.astype(BF16)
    z_mla = _dot(hb, win_ref[:, 0:Z_ZU])
    z_end = _dot(hb, win_ref[:, Z_KPE:Z_WIDTH])
    z_mid = _dot(hb, win_ref[:, Z_ZU:Z_KPE])

    cq = z_mla[:, Z_CQ:Z_CQ + MLA_Q_RANK]
    ckv = z_mla[:, Z_CKV:Z_CKV + MLA_KV_RANK]
    cqn = cq * lax.rsqrt(jnp.mean(cq * cq, axis=-1, keepdims=True) + EPS) * qn_ref[...]
    ckvn = ckv * lax.rsqrt(jnp.mean(ckv * ckv, axis=-1, keepdims=True) + EPS) * kvn_ref[...]
    qa = _dot(cqn.astype(BF16), wuq_ref[...])
    kv = _dot(ckvn.astype(BF16), wukv_ref[...])
    npe = MLA_HEADS * MLA_NOPE
    rw = MLA_HEADS * MLA_ROPE
    q_pe = qa[:, npe:npe + rw]
    q_pe_sq = q_pe * q_pe
    lane_pair = lax.broadcasted_iota(jnp.int32, (1, LANES), 1)
    q_rot = q_pe * cq_ref[rs, :] + qa[:, npe + rw:npe + 2 * rw] * sq_ref[rs, :]
    zk = z_end[:, 0:2 * MLA_ROPE]
    tkz = zk * tk_ref[rs, :]
    k_rot = tkz[:, 0:MLA_ROPE] + tkz[:, MLA_ROPE:2 * MLA_ROPE]
    k_pe = zk[:, 0:MLA_ROPE]
    ss_kpe = jnp.sum(k_pe * k_pe, axis=-1, keepdims=True)
    gqn = gqn_ref[...]
    gkn = gkn_ref[...]
    for hh in range(MLA_HEADS):
        qn_h = qa[:, hh * MLA_NOPE:(hh + 1) * MLA_NOPE]
        pair = q_pe_sq[:, (hh // 2) * LANES:(hh // 2 + 1) * LANES]
        own = jnp.where((lane_pair < MLA_ROPE) == (hh % 2 == 0), pair, 0.0)
        ssq = jnp.sum(qn_h * qn_h + own, axis=-1, keepdims=True)
        rq = lax.rsqrt(ssq * (1.0 / MLA_QK) + EPS) * (MLA_SCALE * LOG2_E)
        q_h = jnp.concatenate([qn_h * gqn * rq, q_rot[:, hh * MLA_ROPE:(hh + 1) * MLA_ROPE] * rq], axis=-1)
        q_out[0, hh, rs, :] = q_h.astype(BF16)
        kn_h = kv[:, hh * MLA_NOPE:(hh + 1) * MLA_NOPE]
        ssk = jnp.sum(kn_h * kn_h, axis=-1, keepdims=True) + ss_kpe
        rk = lax.rsqrt(ssk * (1.0 / MLA_QK) + EPS)
        k_h = jnp.concatenate([kn_h * gkn * rk, k_rot * rk], axis=-1)
        k_out[0, hh, rs, :] = k_h.astype(BF16)
        v_h = kv[:, npe + hh * MLA_V:npe + (hh + 1) * MLA_V]
        v_out[0, hh, rs, :] = jnp.concatenate([v_h, jnp.ones_like(v_h)], axis=-1).astype(BF16)

    u = _gelu_tanh(z_mid[:, 0:SGU_WIDTH])
    vv = _gelu_tanh(z_mid[:, Z_ZV - Z_ZU:Z_ZV - Z_ZU + SGU_WIDTH])
    sgn = sgn_ref[...]
    for c in range(tm // SGU_CHUNK):
        rows = slice(c * SGU_CHUNK, (c + 1) * SGU_CHUNK)
        outs = []
        for g in range(SGU_GROUPS):
            cols = slice(g * SGU_GROUP_DIM, (g + 1) * SGU_GROUP_DIM)
            seg = vv[rows, cols]
            vn = seg * lax.rsqrt(jnp.mean(seg * seg, axis=-1, keepdims=True) + EPS) * sgn[:, cols]
            outs.append(_dot(sgw_ref[g], vn.astype(BF16)))
        sp = jnp.concatenate(outs, axis=-1) + sgb_ref[...]
        b_out[rs.start + c * SGU_CHUNK:rs.start + (c + 1) * SGU_CHUNK, :] = (u[rows, :] * sp).astype(BF16)

    gl = _dot(z_end[:, Z_GG - Z_KPE:Z_GG - Z_KPE + LANES].astype(BF16), wg_ref[...]) + bg_ref[...]
    gdec = _log_sigmoid(gl) * (1.0 / GLA_GATE_NORM)
    gla_out[rs, 0:2 * GLA_QK_W] = z_mid[:, Z_GQ - Z_ZU:Z_GQ - Z_ZU + 2 * GLA_QK_W]
    gla_out[rs, 2 * GLA_QK_W:4 * GLA_QK_W] = gdec
    gla_out[rs, 4 * GLA_QK_W:4 * GLA_QK_W + GLA_V_W] = z_mid[:, Z_GV - Z_ZU:Z_GV - Z_ZU + GLA_V_W]
    r_out[rs, :] = z_mid[:, Z_GR - Z_ZU:Z_GR - Z_ZU + GLA_V_W]


def _inproj(x2, n_batch, mod_rows, mod_row0, lw, tabs):
    n_total = x2.shape[0]
    n_tok = n_total // n_batch
    tm = min(TM_IN, n_tok)
    tpb = n_tok // tm
    tab_blocks = tabs["cq"].shape[0] // tm
    if mod_row0 == 0:
        mod_map = lambda i: (i // tpb, 0, 0)
    else:
        mod_map = lambda i: (mod_row0, 0, 0)
    const2 = lambda i: (0, 0)
    tab_map = lambda i: (i % tab_blocks, 0)
    tok_map = lambda i: (i, 0)
    head_map = lambda i: (i // tpb, 0, i % tpb, 0)
    in_specs = [
        pl.BlockSpec((tm, D_MODEL), tok_map),
        pl.BlockSpec((1, 1, 6 * D_MODEL), mod_map),
        pl.BlockSpec((1, D_MODEL), const2),
        pl.BlockSpec((D_MODEL, Z_WIDTH), const2, pipeline_mode=pl.Buffered(1)),
        pl.BlockSpec((1, MLA_Q_RANK), const2),
        pl.BlockSpec((MLA_Q_RANK, MLA_Q_COLS), const2, pipeline_mode=pl.Buffered(1)),
        pl.BlockSpec((1, MLA_KV_RANK), const2),
        pl.BlockSpec((MLA_KV_RANK, MLA_KV_COLS), const2, pipeline_mode=pl.Buffered(1)),
        pl.BlockSpec((1, MLA_NOPE), const2),
        pl.BlockSpec((1, MLA_NOPE), const2),
        pl.BlockSpec((tm, MLA_HEADS * MLA_ROPE), tab_map),
        pl.BlockSpec((tm, MLA_HEADS * MLA_ROPE), tab_map),
        pl.BlockSpec((tm, 2 * MLA_ROPE), tab_map),
        pl.BlockSpec((1, SGU_WIDTH), const2),
        pl.BlockSpec((SGU_GROUPS, SGU_CHUNK, SGU_CHUNK), lambda i: (0, 0, 0)),
        pl.BlockSpec((SGU_CHUNK, SGU_WIDTH), const2),
        pl.BlockSpec((LANES, 2 * GLA_QK_W), const2),
        pl.BlockSpec((1, 2 * GLA_QK_W), const2),
    ]
    out_shape = [
        jax.ShapeDtypeStruct((n_batch, MLA_HEADS, n_tok, MLA_QK), BF16),
        jax.ShapeDtypeStruct((n_batch, MLA_HEADS, n_tok, MLA_QK), BF16),
        jax.ShapeDtypeStruct((n_batch, MLA_HEADS, n_tok, MLA_V_AUG), BF16),
        jax.ShapeDtypeStruct((n_total, SGU_WIDTH), BF16),
        jax.ShapeDtypeStruct((n_total, 4 * GLA_QK_W + GLA_V_W), F32),
        jax.ShapeDtypeStruct((n_total, GLA_V_W), F32),
    ]
    out_specs = [
        pl.BlockSpec((1, MLA_HEADS, tm, MLA_QK), head_map),
        pl.BlockSpec((1, MLA_HEADS, tm, MLA_QK), head_map),
        pl.BlockSpec((1, MLA_HEADS, tm, MLA_V_AUG), head_map),
        pl.BlockSpec((tm, SGU_WIDTH), tok_map),
        pl.BlockSpec((tm, 4 * GLA_QK_W + GLA_V_W), tok_map),
        pl.BlockSpec((tm, GLA_V_W), tok_map),
    ]
    return pl.pallas_call(
        _inproj_kernel,
        grid=(n_total // tm,),
        in_specs=in_specs,
        out_specs=out_specs,
        out_shape=out_shape,
        compiler_params=_cparams(("arbitrary",), VMEM_LIMIT_INPROJ),
        name="inproj",
    )(x2, mod_rows, lw["n1"], lw["w_in"], lw["q_norm"], lw["w_uq"], lw["kv_norm"], lw["w_ukv"],
      lw["gq_nope"], lw["gk_nope"], tabs["cq"], tabs["sq"], tabs["tk"],
      lw["sgu_norm"], lw["sgu_w"], lw["sgu_b"], lw["wg"], lw["bg"])


_GLA_LEVELS = (32, 16, 8, 4, 2, 1)
_N_LEV = len(_GLA_LEVELS)
_G_SPLIT = 3
_R_QI = _N_LEV * GLA_CHUNK
_R_KS = _R_QI + GLA_CHUNK
_R_TOT = _R_KS + GLA_CHUNK
_M_ROWS = _R_TOT + SUBLANES


def _gla_constants():
    c = GLA_CHUNK
    t = np.arange(c)[:, None]
    j = np.arange(c)[None, :]
    lev, masks = [], []
    for m in _GLA_LEVELS:
        upper = (t // m) % 2 == 1
        bound = (t // (2 * m)) * 2 * m + m - 1
        lev.append(np.where(upper, (j > bound) & (j <= t), (j > t) & (j <= bound)).astype(np.float32))
        s = j
        masks.append(((t // (2 * m) == s // (2 * m)) & upper & ((s // m) % 2 == 0)).astype(np.float32))
    masks.append((t == j).astype(np.float32))
    mqi = (j <= t).astype(np.float32)
    mks = (j > t).astype(np.float32)
    tot = np.ones((SUBLANES, c), np.float32)
    flip = lambda a: a[::-1, ::-1]
    m_f = np.concatenate(lev + [mqi, mks, tot], axis=0)
    m_b = np.concatenate([flip(a) for a in lev] + [flip(mqi), flip(mks), tot], axis=0)
    mask_f = np.stack([np.tile(a, (GLA_HEADS, 1)) for a in masks])
    mask_b = np.stack([np.tile(flip(a), (GLA_HEADS, 1)) for a in masks])
    m_all = np.tile(np.stack([m_f, m_b]), (1, 1, _G_SPLIT))
    return m_all, np.stack([mask_f, mask_b])


def _stack_heads(a, lane_head):
    return jnp.concatenate([jnp.where(lane_head == hh, a, jnp.zeros_like(a)) for hh in range(GLA_HEADS)], axis=0)


def _gla_chunk(q, k, g, v, st, mall, masks, lane_head):
    c = GLA_CHUNK
    g_hi = g.astype(BF16)
    r1 = g - g_hi.astype(F32)
    g_mid = r1.astype(BF16)
    g_lo = (r1 - g_mid.astype(F32)).astype(BF16)
    e = _dot(mall, jnp.concatenate([g_hi, g_mid, g_lo], axis=0))
    a = masks[_N_LEV] * _dot_nt(_stack_heads(q.astype(BF16), lane_head), k.astype(BF16))
    for lev in range(_N_LEV):
        ex = jnp.exp(e[lev * c:(lev + 1) * c])
        a = a + masks[lev] * _dot_nt(_stack_heads((q * ex).astype(BF16), lane_head), (k * ex).astype(BF16))
    qi = (q * jnp.exp(e[_R_QI:_R_QI + c])).astype(BF16)
    o_inter = _dot_nt(_stack_heads(qi, lane_head), st.astype(BF16))
    vb = v.astype(BF16)
    a16 = a.astype(BF16)
    outs = []
    for hh in range(GLA_HEADS):
        rows = slice(hh * c, (hh + 1) * c)
        outs.append(o_inter[rows] + _dot(a16[rows], vb[:, hh * GLA_DV:(hh + 1) * GLA_DV]))
    o = jnp.concatenate(outs, axis=-1)
    kd = (k * jnp.exp(e[_R_KS:_R_KS + c])).astype(BF16)
    upd = _dot_tn(vb, kd)
    st_new = st * jnp.exp(e[_R_TOT:_R_TOT + 1])
    for hh in range(GLA_HEADS):
        st_new = st_new + jnp.where(lane_head == hh, upd[hh * GLA_DV:(hh + 1) * GLA_DV], 0.0)
    return o, st_new


def _gla_kernel(fwd_ref, bwd_ref, s0_ref, mall_ref, mask_ref, of_ref, ob_ref, sfin_ref, st_ref):
    step = pl.program_id(0)
    n_batch = fwd_ref.shape[0]

    @pl.when(step == 0)
    def _():
        st_ref[...] = s0_ref[...]

    lane_head = lax.broadcasted_iota(jnp.int32, (1, GLA_QK_W), 1) // GLA_DK
    w = GLA_QK_W
    for b in range(n_batch):
        for direction, (in_ref, o_ref) in enumerate(((fwd_ref, of_ref), (bwd_ref, ob_ref))):
            mall = mall_ref[direction]
            masks = [mask_ref[direction, i] for i in range(_N_LEV + 1)]
            q = in_ref[b, :, 0:w]
            k = in_ref[b, :, w:2 * w]
            g = in_ref[b, :, (2 + direction) * w:(3 + direction) * w]
            v = in_ref[b, :, 4 * w:4 * w + GLA_V_W]
            o, st_new = _gla_chunk(q, k, g, v, st_ref[direction, b], mall, masks, lane_head)
            o_ref[b] = o
            st_ref[direction, b] = st_new

    @pl.when(step == pl.num_programs(0) - 1)
    def _():
        sfin_ref[...] = st_ref[...]


def _gla(gla_in, s0, consts):
    n_batch, n_tok, width = gla_in.shape
    nc = n_tok // GLA_CHUNK
    mall, masks = consts
    st_shape = (2, n_batch, GLA_DV, GLA_QK_W)
    return pl.pallas_call(
        _gla_kernel,
        grid=(nc,),
        in_specs=[
            pl.BlockSpec((n_batch, GLA_CHUNK, width), lambda c: (0, c, 0)),
            pl.BlockSpec((n_batch, GLA_CHUNK, width), lambda c: (0, nc - 1 - c, 0)),
            pl.BlockSpec(st_shape, lambda c: (0, 0, 0, 0)),
            pl.BlockSpec(mall.shape, lambda c: (0, 0, 0)),
            pl.BlockSpec(masks.shape, lambda c: (0, 0, 0, 0)),
        ],
        out_specs=[
            pl.BlockSpec((n_batch, GLA_CHUNK, GLA_V_W), lambda c: (0, c, 0)),
            pl.BlockSpec((n_batch, GLA_CHUNK, GLA_V_W), lambda c: (0, nc - 1 - c, 0)),
            pl.BlockSpec(st_shape, lambda c: (0, 0, 0, 0)),
        ],
        out_shape=[
            jax.ShapeDtypeStruct((n_batch, n_tok, GLA_V_W), F32),
            jax.ShapeDtypeStruct((n_batch, n_tok, GLA_V_W), F32),
            jax.ShapeDtypeStruct(st_shape, F32),
        ],
        scratch_shapes=[pltpu.VMEM(st_shape, F32)],
        compiler_params=_cparams(("arbitrary",)),
        name="gla",
    )(gla_in, gla_in, s0, mall, masks)


def _attn_kernel(*refs, n_seg, n_cast):
    q_ref = refs[0]
    k_refs = refs[1:1 + n_seg]
    v_refs = refs[1 + n_seg:1 + 2 * n_seg]
    cast_in = refs[1 + 2 * n_seg:1 + 2 * n_seg + n_cast]
    o_ref = refs[1 + 2 * n_seg + n_cast]
    cast_out = refs[2 + 2 * n_seg + n_cast:]
    for src, dst in zip(cast_in, cast_out):
        dst[...] = src[...].astype(BF16)
    q = q_ref[0, 0]
    n_main = k_refs[0].shape[2]
    tk = min(TK_ATTN, n_main)
    groups = [[(0, slice(c * tk, (c + 1) * tk))] for c in range(n_main // tk)]
    groups[0] += [(i, slice(0, k_refs[i].shape[2])) for i in range(1, n_seg)]
    m = acc = None
    for group in groups:
        scores = [_dot_nt(q, k_refs[i][0, 0, keys, :]) for i, keys in group]
        m_g = functools.reduce(jnp.maximum, [jnp.max(s, axis=-1, keepdims=True) for s in scores])
        m_new = m_g if m is None else jnp.maximum(m, m_g)
        acc_g = functools.reduce(jnp.add, [_dot(jnp.exp2(s - m_new).astype(BF16), v_refs[i][0, 0, keys, :])
                                           for s, (i, keys) in zip(scores, group)])
        acc = acc_g if m is None else jnp.exp2(m - m_new) * acc + acc_g
        m = m_new
    o_ref[0] = (acc[:, 0:MLA_V] / acc[:, MLA_V:MLA_V_AUG]).astype(BF16)


def _attention(q, ks, vs, casts=(), cast_layer=0):
    n_batch, n_heads, nq, _ = q.shape
    tq = min(TQ_ATTN, nq)
    n_seg = len(ks)
    n_q = nq // tq
    n_steps = n_batch * n_heads * n_q
    in_specs = [pl.BlockSpec((1, 1, tq, MLA_QK), lambda b, h, i: (b, h, i, 0))]
    for k in ks:
        in_specs.append(pl.BlockSpec((1, 1, k.shape[2], MLA_QK), lambda b, h, i: (b, h, 0, 0)))
    for v in vs:
        in_specs.append(pl.BlockSpec((1, 1, v.shape[2], MLA_V_AUG), lambda b, h, i: (b, h, 0, 0)))
    out_specs = [pl.BlockSpec((1, tq, MLA_V), lambda b, h, i: (b, i, h))]
    out_shape = [jax.ShapeDtypeStruct((n_batch, nq, n_heads * MLA_V), BF16)]
    slabs = []
    for w in casts:
        depth, cols = w.shape[0], w.shape[-1]
        rows = w.size // (depth * cols * n_steps)
        assert rows * depth * cols * n_steps == w.size and rows % BF16_ROWS == 0
        slabs.append(w.reshape(depth * n_steps, rows, cols))
        in_specs.append(pl.BlockSpec(
            (1, rows, cols), lambda b, h, i: (cast_layer * n_steps + (b * n_heads + h) * n_q + i, 0, 0)))
        out_specs.append(pl.BlockSpec((1, rows, cols), lambda b, h, i: ((b * n_heads + h) * n_q + i, 0, 0)))
        out_shape.append(jax.ShapeDtypeStruct((n_steps, rows, cols), BF16))
    outs = pl.pallas_call(
        functools.partial(_attn_kernel, n_seg=n_seg, n_cast=len(casts)),
        grid=(n_batch, n_heads, n_q),
        in_specs=in_specs,
        out_specs=out_specs,
        out_shape=out_shape,
        compiler_params=_cparams(("arbitrary", "arbitrary", "arbitrary")),
        name="attn",
    )(q, *ks, *vs, *slabs)
    return [outs[0]] + [o.reshape(w.shape[1:]) for o, w in zip(outs[1:], casts)]


def _outproj_kernel(a_ref, b_ref, of_ref, ob_ref, r_ref, x_ref, mod_ref, gon_ref, n2_ref, wout_ref,
                    wr_hi_ref, wr_lo_ref, br_ref, lst_ref, sl_ref,
                    x1_out, h2_out, meta_out, cnt_out):
    d = D_MODEL
    tm = x_ref.shape[0]
    gate1 = mod_ref[0, :, 2 * d:3 * d]
    shift2 = mod_ref[0, :, 3 * d:4 * d]
    scale2 = mod_ref[0, :, 4 * d:5 * d]

    og = of_ref[...] + ob_ref[...]
    gon = gon_ref[...]
    parts = []
    for hh in range(GLA_HEADS):
        seg = og[:, hh * GLA_DV:(hh + 1) * GLA_DV]
        parts.append(seg * lax.rsqrt(jnp.mean(seg * seg, axis=-1, keepdims=True) + EPS) * gon)
    cg = jnp.concatenate(parts, axis=-1) * _silu(r_ref[...])
    na = MLA_HEADS * MLA_V
    y = _dot(a_ref[...], wout_ref[0:na, :])
    y = y + _dot(b_ref[...], wout_ref[na:na + SGU_WIDTH, :])
    y = y + _dot(cg.astype(BF16), wout_ref[na + SGU_WIDTH:na + SGU_WIDTH + GLA_V_W, :])
    x1 = x_ref[...] + gate1 * y
    x1_out[...] = x1

    ms = jnp.mean(x1 * x1, axis=-1, keepdims=True)
    h2 = (x1 * lax.rsqrt(ms + EPS) * n2_ref[...]) * (1.0 + scale2) + shift2
    h_hi = h2.astype(BF16)
    h2_out[...] = h_hi
    h_lo = (h2 - h_hi.astype(F32)).astype(BF16)

    wr_hi = wr_hi_ref[...]
    lt2 = _dot_nt(jnp.concatenate([wr_hi, wr_lo_ref[...]], axis=0), h_hi)
    lt = lt2[0:ROUTER_ROWS] + lt2[ROUTER_ROWS:2 * ROUTER_ROWS] + _dot_nt(wr_hi, h_lo) + br_ref[:, 0:1]
    grp = [lt[MOE_EXPERTS + g:MOE_EXPERTS + g + 1, :] for g in range(MOE_GROUPS)]
    gmax = functools.reduce(jnp.maximum, grp)
    gsel = jnp.full(gmax.shape, MOE_GROUPS - 1, jnp.int32)
    for g in range(MOE_GROUPS - 2, -1, -1):
        gsel = jnp.where(grp[g] == gmax, g, gsel)
    gsum = functools.reduce(jnp.add, [jnp.exp(gl - gmax) for gl in grp])
    g_w = 1.0 / gsum
    e_in = []
    for j in range(MOE_PER_GROUP):
        ej = lt[j:j + 1, :]
        for g in range(1, MOE_GROUPS):
            ej = jnp.where(gsel == g, lt[g * MOE_PER_GROUP + j:g * MOE_PER_GROUP + j + 1, :], ej)
        e_in.append(ej)
    m1 = functools.reduce(jnp.maximum, e_in)
    i1 = jnp.full(m1.shape, MOE_PER_GROUP - 1, jnp.int32)
    for j in range(MOE_PER_GROUP - 2, -1, -1):
        i1 = jnp.where(e_in[j] == m1, j, i1)
    e_rest = [jnp.where(i1 == j, -jnp.inf, e_in[j]) for j in range(MOE_PER_GROUP)]
    m2 = functools.reduce(jnp.maximum, e_rest)
    i2 = jnp.full(m2.shape, MOE_PER_GROUP - 1, jnp.int32)
    for j in range(MOE_PER_GROUP - 2, -1, -1):
        i2 = jnp.where(e_rest[j] == m2, j, i2)
    p2 = jnp.exp(m2 - m1)
    w1 = g_w / (1.0 + p2)
    w2 = g_w * p2 / (1.0 + p2)
    eid1 = gsel * MOE_PER_GROUP + i1
    eid2 = gsel * MOE_PER_GROUP + i2

    e_iota = lax.broadcasted_iota(jnp.int32, (MOE_EXPERTS, tm), 0)
    hit1 = e_iota == eid1
    hit2 = e_iota == eid2
    onehot = jnp.where(hit1, 1.0, 0.0) + jnp.where(hit2, 1.0, 0.0)
    rank = _dot(onehot.astype(BF16), lst_ref[...])
    counts = jnp.sum(onehot, axis=1, keepdims=True)
    padded = jnp.ceil(counts * (1.0 / BF16_ROWS)) * BF16_ROWS
    padded_l = jnp.broadcast_to(padded, (MOE_EXPERTS, LANES))
    off = _dot(sl_ref[...], padded_l.astype(BF16))
    pos_e = off[:, 0:1] + rank
    pos1 = jnp.sum(jnp.where(hit1, pos_e, 0.0), axis=0, keepdims=True)
    pos2 = jnp.sum(jnp.where(hit2, pos_e, 0.0), axis=0, keepdims=True)
    zero = jnp.zeros_like(pos1)
    meta_out[0] = jnp.concatenate(
        [pos1, pos2, w1, w2, eid1.astype(F32), eid2.astype(F32), zero, zero], axis=0)
    cnt_out[0] = padded_l


def _outproj(a, b, o_f, o_b, r, x2, n_batch, mod_rows, mod_row0, lw, rconst):
    n_total = x2.shape[0]
    n_tok = n_total // n_batch
    tm = T_LOC
    tpb = n_tok // tm
    if mod_row0 == 0:
        mod_map = lambda i: (i // tpb, 0, 0)
    else:
        mod_map = lambda i: (mod_row0, 0, 0)
    const2 = lambda i: (0, 0)
    tok_map = lambda i: (i, 0)
    n_tiles = n_total // tm
    in_specs = [
        pl.BlockSpec((tm, MLA_HEADS * MLA_V), tok_map),
        pl.BlockSpec((tm, SGU_WIDTH), tok_map),
        pl.BlockSpec((tm, GLA_V_W), tok_map),
        pl.BlockSpec((tm, GLA_V_W), tok_map),
        pl.BlockSpec((tm, GLA_V_W), tok_map),
        pl.BlockSpec((tm, D_MODEL), tok_map),
        pl.BlockSpec((1, 1, 6 * D_MODEL), mod_map),
        pl.BlockSpec((1, GLA_DV), const2),
        pl.BlockSpec((1, D_MODEL), const2),
        pl.BlockSpec((D_MODEL, D_MODEL), const2, pipeline_mode=pl.Buffered(1)),
        pl.BlockSpec((ROUTER_ROWS, D_MODEL), const2),
        pl.BlockSpec((ROUTER_ROWS, D_MODEL), const2),
        pl.BlockSpec((ROUTER_ROWS, LANES), const2),
        pl.BlockSpec((tm, tm), const2),
        pl.BlockSpec((MOE_EXPERTS, MOE_EXPERTS), const2),
    ]
    out_shape = [
        jax.ShapeDtypeStruct((n_total, D_MODEL), F32),
        jax.ShapeDtypeStruct((n_total, D_MODEL), BF16),
        jax.ShapeDtypeStruct((n_tiles, META_ROWS, tm), F32),
        jax.ShapeDtypeStruct((n_tiles, MOE_EXPERTS, LANES), F32),
    ]
    out_specs = [
        pl.BlockSpec((tm, D_MODEL), tok_map),
        pl.BlockSpec((tm, D_MODEL), tok_map),
        pl.BlockSpec((1, META_ROWS, tm), lambda i: (i, 0, 0)),
        pl.BlockSpec((1, MOE_EXPERTS, LANES), lambda i: (i, 0, 0)),
    ]
    return pl.pallas_call(
        _outproj_kernel,
        grid=(n_tiles,),
        in_specs=in_specs,
        out_specs=out_specs,
        out_shape=out_shape,
        compiler_params=_cparams(("arbitrary",)),
        name="outproj",
    )(a, b, o_f, o_b, r, x2, mod_rows, lw["gla_out_norm"], lw["n2"], lw["w_out"],
      lw["wr_hi"], lw["wr_lo"], lw["br"], rconst["lst"], rconst["sl"])


def _piece_copies(src_ref, dst_ref, sem, src_row, dst_row, n_rows):
    copies = []
    for size in _PIECE_SIZES:
        done = n_rows & ~(2 * size - 1)
        pred = (n_rows & size) != 0
        s0 = pl.multiple_of(src_row + done, BF16_ROWS)
        d0 = pl.multiple_of(dst_row + done, BF16_ROWS)
        cp = pltpu.make_async_copy(src_ref.at[pl.ds(s0, size)], dst_ref.at[pl.ds(d0, size)], sem)
        copies.append((pred, cp))
    return copies


def _list_base(tile, s):
    return tile * len(_PIECE_SIZES) + s


def _start_listed_pieces(src_ref, dst_ref, sem, src_tab, dst_tab, n_tab, tile):
    for s, size in enumerate(_PIECE_SIZES):
        base = _list_base(tile, s)

        def body(i, carry, size=size, base=base):
            idx = base * MOE_EXPERTS + i
            s0 = pl.multiple_of(src_tab[idx], BF16_ROWS)
            d0 = pl.multiple_of(dst_tab[idx], BF16_ROWS)
            pltpu.make_async_copy(src_ref.at[pl.ds(s0, size)], dst_ref.at[pl.ds(d0, size)], sem).start()
            return carry

        lax.fori_loop(0, n_tab[base], body, 0)


def _wait_listed_pieces(src_ref, dst_ref, sem, n_tab, tile):
    for s, size in enumerate(_PIECE_SIZES):
        def body(i, carry, size=size):
            pltpu.make_async_copy(src_ref.at[pl.ds(0, size)], dst_ref.at[pl.ds(0, size)], sem).wait()
            return carry

        lax.fori_loop(0, n_tab[_list_base(tile, s)], body, 0)


def _stream_starts(stream_tiles):
    return [sum(stream_tiles[:s]) for s in range(len(stream_tiles))]


def _stream_block(i, start, n):
    return jnp.clip(i - start, 0, n - 1)


def _dispatch_kernel(ploc_tab, pdst_tab, pn_tab, ztab_dst, ztab_cnt, n_used, *refs, stream_tiles):
    n_streams = len(stream_tiles)
    h2_refs = refs[:n_streams]
    meta_ref, xs_ref, xloc_ref, zeros_ref, sems = refs[n_streams:]
    j = pl.program_id(0)
    n_tiles = pl.num_programs(0)
    tm = h2_refs[0].shape[0]
    slot = j % 2

    @pl.when(j == 0)
    def _():
        zeros_ref[...] = jnp.zeros_like(zeros_ref)

        def fill(t, carry, start):
            cp = pltpu.make_async_copy(zeros_ref, xs_ref.at[pl.ds(pl.multiple_of(t * TM_FFN, TM_FFN), TM_FFN)],
                                       sems.at[0])
            if start:
                cp.start()
            else:
                cp.wait()
            return carry
        lax.fori_loop(n_used[0], xs_ref.shape[0] // TM_FFN, functools.partial(fill, start=True), 0)
        lax.fori_loop(n_used[0], xs_ref.shape[0] // TM_FFN, functools.partial(fill, start=False), 0)
        zcopies = []
        for e in range(MOE_EXPERTS):
            zcopies += _piece_copies(zeros_ref, xs_ref, sems.at[0], 0, ztab_dst[e], ztab_cnt[e])
        for pred, cp in zcopies:
            @pl.when(pred)
            def _(cp=cp):
                cp.start()
        for pred, cp in zcopies:
            @pl.when(pred)
            def _(cp=cp):
                cp.wait()

    pos1 = meta_ref[0, 0:1, :]
    pos2 = meta_ref[0, 1:2, :]
    s_iota = lax.broadcasted_iota(jnp.int32, (SLOTS, tm), 0).astype(F32)
    perm = jnp.where((s_iota == pos1) | (s_iota == pos2), 1.0, 0.0).astype(BF16)
    h2 = h2_refs[0][...]
    for s, start in list(enumerate(_stream_starts(stream_tiles)))[1:]:
        h2 = jnp.where(j >= start, h2_refs[s][...], h2)
    xloc_ref[slot] = _dot(perm, h2).astype(BF16)

    @pl.when(j > 0)
    def _():
        _wait_listed_pieces(xloc_ref.at[1 - slot], xs_ref, sems.at[1 - slot], pn_tab, j - 1)

    _start_listed_pieces(xloc_ref.at[slot], xs_ref, sems.at[slot], ploc_tab, pdst_tab, pn_tab, j)

    @pl.when(j == n_tiles - 1)
    def _():
        _wait_listed_pieces(xloc_ref.at[slot], xs_ref, sems.at[slot], pn_tab, j)


def _dispatch(h2s, meta, tabs, r_max):
    n_tiles = meta.shape[0]
    tm = T_LOC
    stream_tiles = tuple(h.shape[0] // tm for h in h2s)
    assert sum(stream_tiles) == n_tiles
    h2_specs = [pl.BlockSpec((tm, D_MODEL), lambda i, *_, start=start, n=n: (_stream_block(i, start, n), 0))
                for start, n in zip(_stream_starts(stream_tiles), stream_tiles)]
    grid_spec = pltpu.PrefetchScalarGridSpec(
        num_scalar_prefetch=6,
        grid=(n_tiles,),
        in_specs=h2_specs + [pl.BlockSpec((1, META_ROWS, tm), lambda i, *_: (i, 0, 0))],
        out_specs=pl.BlockSpec(memory_space=pl.ANY),
        scratch_shapes=[pltpu.VMEM((2, SLOTS, D_MODEL), BF16), pltpu.VMEM((TM_FFN, D_MODEL), BF16),
                        pltpu.SemaphoreType.DMA((2,))],
    )
    return pl.pallas_call(
        functools.partial(_dispatch_kernel, stream_tiles=stream_tiles),
        grid_spec=grid_spec,
        out_shape=jax.ShapeDtypeStruct((r_max, D_MODEL), BF16),
        compiler_params=_cparams(("arbitrary",)),
        name="dispatch",
    )(tabs["ploc"], tabs["pdst"], tabs["pn"], tabs["zdst"], tabs["zcnt"], tabs["n_used"], *h2s, meta)


def _ffn_kernel(tile_e, tile_blk, n_used, x_ref, w1_ref, w3_ref, w2_ref, y_ref):
    i = pl.program_id(0)

    @pl.when(i < n_used[0])
    def _():
        x = x_ref[...]
        acc = None
        for f in range(D_EXPERT // F_CHUNK):
            cols = slice(f * F_CHUNK, (f + 1) * F_CHUNK)
            h1 = _dot(x, w1_ref[0, :, cols])
            h3 = _dot(x, w3_ref[0, :, cols])
            act = (_silu(h1) * h3).astype(BF16)
            part = _dot(act, w2_ref[0, cols, :])
            acc = part if acc is None else acc + part
        y_ref[...] = acc.astype(BF16)

    @pl.when(i >= n_used[0])
    def _():
        y_ref[...] = jnp.zeros_like(y_ref)


def _ffn(xs, w1, w3, w2, tabs):
    r_max = xs.shape[0]
    n_mt = r_max // TM_FFN
    grid_spec = pltpu.PrefetchScalarGridSpec(
        num_scalar_prefetch=3,
        grid=(n_mt,),
        in_specs=[
            pl.BlockSpec((TM_FFN, D_MODEL), lambda i, te, tb, nu: (tb[i], 0)),
            pl.BlockSpec((1, D_MODEL, D_EXPERT), lambda i, te, tb, nu: (te[i], 0, 0)),
            pl.BlockSpec((1, D_MODEL, D_EXPERT), lambda i, te, tb, nu: (te[i], 0, 0)),
            pl.BlockSpec((1, D_EXPERT, D_MODEL), lambda i, te, tb, nu: (te[i], 0, 0)),
        ],
        out_specs=pl.BlockSpec((TM_FFN, D_MODEL), lambda i, te, tb, nu: (i, 0)),
    )
    return pl.pallas_call(
        _ffn_kernel,
        grid_spec=grid_spec,
        out_shape=jax.ShapeDtypeStruct((r_max, D_MODEL), BF16),
        compiler_params=_cparams(("arbitrary",)),
        name="ffn",
    )(tabs["tile_e"], tabs["tile_blk"], tabs["n_used"], xs, w1, w3, w2)


def _combine_kernel(ploc_tab, pdst_tab, pn_tab, y_ref, metac_ref, mod_ref, *refs, stream_tiles):
    n_streams = len(stream_tiles)
    x1_refs = refs[:n_streams]
    out_refs = refs[n_streams:2 * n_streams]
    yloc_ref, sems = refs[2 * n_streams:]
    j = pl.program_id(0)
    n_tiles = pl.num_programs(0)
    d = D_MODEL
    tm = x1_refs[0].shape[0]
    slot = j % 2

    @pl.when(j == 0)
    def _():
        yloc_ref[...] = jnp.zeros_like(yloc_ref)
        _start_listed_pieces(y_ref, yloc_ref.at[0], sems.at[0], pdst_tab, ploc_tab, pn_tab, 0)

    @pl.when(j + 1 < n_tiles)
    def _():
        _start_listed_pieces(y_ref, yloc_ref.at[1 - slot], sems.at[1 - slot], pdst_tab, ploc_tab, pn_tab, j + 1)

    _wait_listed_pieces(y_ref, yloc_ref.at[slot], sems.at[slot], pn_tab, j)

    mc = metac_ref[0]
    s_iota = lax.broadcasted_iota(jnp.int32, (tm, SLOTS), 1).astype(F32)
    wc = jnp.where(s_iota == mc[:, 0:1], mc[:, 2:3], 0.0) + jnp.where(s_iota == mc[:, 1:2], mc[:, 3:4], 0.0)
    moe = _dot(wc.astype(BF16), yloc_ref[slot])
    gate2 = mod_ref[0, :, 5 * d:6 * d]
    for start, n, x1_ref, out_ref in zip(_stream_starts(stream_tiles), stream_tiles, x1_refs, out_refs):
        @pl.when((j >= start) & (j < start + n))
        def _(x1_ref=x1_ref, out_ref=out_ref):
            out_ref[...] = x1_ref[...] + gate2 * moe


def _combine(y, metac, x1s, n_batch, mod_rows, mod_row0s, tabs):
    tm = T_LOC
    stream_tiles = tuple(x1.shape[0] // tm for x1 in x1s)
    starts = _stream_starts(stream_tiles)
    n_tiles = sum(stream_tiles)

    def mod_map(i, *_):
        row = jnp.int32(0)
        for start, n, x1, row0 in zip(starts, stream_tiles, x1s, mod_row0s):
            tpb = x1.shape[0] // n_batch // tm
            row_s = _stream_block(i, start, n) // tpb if row0 == 0 else row0
            row = jnp.where(i >= start, row_s, row)
        return (row, 0, 0)

    tile_specs = [pl.BlockSpec((tm, D_MODEL), lambda i, *_, start=start, n=n: (_stream_block(i, start, n), 0))
                  for start, n in zip(starts, stream_tiles)]
    grid_spec = pltpu.PrefetchScalarGridSpec(
        num_scalar_prefetch=3,
        grid=(n_tiles,),
        in_specs=[
            pl.BlockSpec(memory_space=pl.ANY),
            pl.BlockSpec((1, tm, META_ROWS), lambda i, *_: (i, 0, 0)),
            pl.BlockSpec((1, 1, 6 * D_MODEL), mod_map),
        ] + tile_specs,
        out_specs=tile_specs,
        scratch_shapes=[pltpu.VMEM((2, SLOTS, D_MODEL), BF16), pltpu.SemaphoreType.DMA((2,))],
    )
    return pl.pallas_call(
        functools.partial(_combine_kernel, stream_tiles=stream_tiles),
        grid_spec=grid_spec,
        out_shape=[jax.ShapeDtypeStruct(x1.shape, F32) for x1 in x1s],
        compiler_params=_cparams(("arbitrary",)),
        name="combine",
    )(tabs["ploc"], tabs["pdst"], tabs["pn"], y, metac, mod_rows, *x1s)


def _rope_tables(n_tokens):
    rows = n_tokens // GRID_W
    row = jnp.repeat(jnp.arange(rows, dtype=jnp.int32), GRID_W)
    col = jnp.tile(jnp.arange(GRID_W, dtype=jnp.int32), rows)
    n_freq = MLA_ROPE // 4
    inv_freq = ROPE_THETA ** (-jnp.arange(n_freq, dtype=F32) / n_freq)
    ang_r = row[:, None].astype(F32) * inv_freq
    ang_c = col[:, None].astype(F32) * inv_freq
    cos = jnp.concatenate([jnp.cos(ang_r)] * 2 + [jnp.cos(ang_c)] * 2, axis=-1)
    sin = jnp.concatenate([-jnp.sin(ang_r), jnp.sin(ang_r), -jnp.sin(ang_c), jnp.sin(ang_c)], axis=-1)
    return cos, sin


_SWAP64 = np.concatenate([np.arange(16, 32), np.arange(0, 16), np.arange(48, 64), np.arange(32, 48)])


def _position_tables(cos, sin, q_gain, k_gain):
    gq = q_gain[MLA_NOPE:]
    gk = k_gain[MLA_NOPE:]
    cq = jnp.tile(cos * gq, (1, MLA_HEADS))
    sq = jnp.tile(sin * gq[_SWAP64], (1, MLA_HEADS))
    tk = jnp.concatenate([cos * gk, sin * gk[_SWAP64]], axis=-1)
    return {"cq": cq, "sq": sq, "tk": tk}


def _layer_weights(l, p):
    d = D_MODEL
    o = _IN_OFF
    w_in = p["w_in"][l]
    seg = lambda i: w_in[:, o[i]:o[i + 1]]
    kpe = seg(2)
    w_in_r = jnp.concatenate(
        [seg(0), seg(1), seg(3), seg(4), seg(5) * (GLA_DK ** -0.5), seg(6), seg(7), seg(8),
         kpe, kpe[:, _SWAP64], seg(9), seg(10), jnp.zeros((d, Z_WIDTH - Z_GG - 2 * GLA_GATE_RANK), F32)],
        axis=-1).astype(BF16)
    w_uq = p["mla_w_uq"][l].reshape(MLA_Q_RANK, MLA_HEADS, MLA_QK)
    q_nope = w_uq[:, :, :MLA_NOPE].reshape(MLA_Q_RANK, -1)
    q_pe = w_uq[:, :, MLA_NOPE:]
    w_uq_r = jnp.concatenate([q_nope, q_pe.reshape(MLA_Q_RANK, -1), q_pe[:, :, _SWAP64].reshape(MLA_Q_RANK, -1)],
                             axis=-1).astype(BF16)
    w_ukv = p["mla_w_ukv"][l].reshape(MLA_KV_RANK, MLA_HEADS, MLA_NOPE + MLA_V)
    w_ukv_r = jnp.concatenate([w_ukv[:, :, :MLA_NOPE].reshape(MLA_KV_RANK, -1),
                               w_ukv[:, :, MLA_NOPE:].reshape(MLA_KV_RANK, -1)], axis=-1).astype(BF16)
    wg = jnp.zeros((LANES, 2 * GLA_QK_W), F32)
    wg = wg.at[0:GLA_GATE_RANK, 0:GLA_QK_W].set(p["gla_wg_f"][l])
    wg = wg.at[GLA_GATE_RANK:2 * GLA_GATE_RANK, GLA_QK_W:].set(p["gla_wg_b"][l])
    bg = jnp.concatenate([p["gla_bg_f"][l], p["gla_bg_b"][l]])[None, :]
    sgu_b = jnp.repeat(p["sgu_b"][l].T, SGU_GROUP_DIM, axis=1)
    wr = jnp.zeros((ROUTER_ROWS, d), F32)
    wr = wr.at[0:MOE_EXPERTS].set(p["moe_w_expert"][l].T)
    wr = wr.at[MOE_EXPERTS:MOE_EXPERTS + MOE_GROUPS].set(p["moe_w_group"][l].T)
    wr_hi = wr.astype(BF16)
    wr_lo = (wr - wr_hi.astype(F32)).astype(BF16)
    br = jnp.zeros((ROUTER_ROWS,), F32).at[0:MOE_EXPERTS].set(p["moe_b_expert"][l])
    br = br.at[MOE_EXPERTS:MOE_EXPERTS + MOE_GROUPS].set(p["moe_b_group"][l])
    return {
        "n1": p["norm1_g"][l][None, :], "n2": p["norm2_g"][l][None, :],
        "w_in": w_in_r, "q_norm": p["mla_q_norm"][l][None, :], "kv_norm": p["mla_kv_norm"][l][None, :],
        "w_uq": w_uq_r, "w_ukv": w_ukv_r,
        "gq_nope": p["mla_q_gain"][l][None, :MLA_NOPE], "gk_nope": p["mla_k_gain"][l][None, :MLA_NOPE],
        "sgu_norm": p["sgu_norm"][l].reshape(1, SGU_WIDTH), "sgu_w": p["sgu_w"][l].astype(BF16), "sgu_b": sgu_b,
        "wg": wg.astype(BF16), "bg": bg,
        "gla_out_norm": p["gla_out_norm"][l][None, :], "w_out": p["w_out"][l].astype(BF16),
        "wr_hi": wr_hi, "wr_lo": wr_lo, "br": jnp.broadcast_to(br[:, None], (ROUTER_ROWS, LANES)),
    }


def _router_constants():
    t = np.arange(T_LOC)
    lst = (t[:, None] < t[None, :]).astype(np.float32)
    e = np.arange(MOE_EXPERTS)
    sl = (e[None, :] < e[:, None]).astype(np.float32)
    return {"lst": jnp.asarray(lst, BF16), "sl": jnp.asarray(sl, BF16)}


def _dispatch_tables(cnt, r_max):
    n_tiles = cnt.shape[0]
    cnt = cnt.astype(jnp.int32)
    loc_off = jnp.cumsum(cnt, axis=1) - cnt
    rows_e = jnp.sum(cnt, axis=0)
    tiles_e = (rows_e + TM_FFN - 1) // TM_FFN
    tile_start = jnp.cumsum(tiles_e) - tiles_e
    base_e = tile_start * TM_FFN
    dst = base_e[None, :] + jnp.cumsum(cnt, axis=0) - cnt
    n_used = jnp.sum(tiles_e)
    n_mt = r_max // TM_FFN
    ti = jnp.arange(n_mt, dtype=jnp.int32)
    tile_blk = jnp.minimum(ti, n_used - 1)
    tile_e = jnp.sum(tile_blk[:, None] >= (tile_start + tiles_e)[None, :], axis=1).astype(jnp.int32)
    tile_e = jnp.minimum(tile_e, MOE_EXPERTS - 1)
    zdst = base_e + rows_e
    zcnt = tiles_e * TM_FFN - rows_e
    i32 = lambda a: a.astype(jnp.int32)
    sizes = jnp.asarray(_PIECE_SIZES, jnp.int32)
    has = (cnt[:, :, None] & sizes) != 0
    done = cnt[:, :, None] & ~(2 * sizes - 1)
    pos = jnp.cumsum(has, axis=1) - 1
    pick = has[..., None] & (pos[..., None] == jnp.arange(MOE_EXPERTS))
    listed = lambda a: jnp.sum(jnp.where(pick, (a[:, :, None] + done)[..., None], 0), axis=1)
    return {"ploc": i32(listed(loc_off).reshape(-1)), "pdst": i32(listed(dst).reshape(-1)),
            "pn": i32(jnp.sum(has, axis=1).reshape(-1)), "zdst": i32(zdst), "zcnt": i32(zcnt),
            "tile_e": tile_e, "tile_blk": i32(tile_blk), "n_used": i32(n_used.reshape(1))}


def _moe(h2s, metas, cnts, x1s, n_batch, mod_rows, mod_row0s, lw):
    meta = jnp.concatenate(metas, axis=0)
    cnt = jnp.concatenate(cnts, axis=0)
    n_tiles = meta.shape[0]
    worst_rows = n_tiles * SLOTS + MOE_EXPERTS * (TM_FFN - BF16_ROWS)
    r_max = -(-worst_rows // TM_FFN) * TM_FFN
    tabs = _dispatch_tables(cnt[:, :, 0], r_max)
    xs = _dispatch(h2s, meta, tabs, r_max)
    y = _ffn(xs, lw["w1"], lw["w3"], lw["w2"], tabs)
    metac = jnp.swapaxes(meta, 1, 2)
    return _combine(y, metac, x1s, n_batch, mod_rows, mod_row0s, tabs)


def kernel(x, c, ctx, c_ctx, norm1_g, norm2_g, ada_w, ada_b, w_in, mla_q_norm, mla_w_uq, mla_kv_norm, mla_w_ukv,
           mla_q_gain, mla_k_gain, sgu_norm, sgu_w, sgu_b, gla_wg_f, gla_bg_f, gla_wg_b, gla_bg_b, gla_out_norm,
           w_out, moe_w_group, moe_b_group, moe_w_expert, moe_b_expert, moe_w1, moe_w3, moe_w2):
    p = dict(norm1_g=norm1_g, norm2_g=norm2_g, w_in=w_in, mla_q_norm=mla_q_norm, mla_w_uq=mla_w_uq,
             mla_kv_norm=mla_kv_norm, mla_w_ukv=mla_w_ukv, mla_q_gain=mla_q_gain, mla_k_gain=mla_k_gain,
             sgu_norm=sgu_norm, sgu_w=sgu_w, sgu_b=sgu_b, gla_wg_f=gla_wg_f, gla_bg_f=gla_bg_f,
             gla_wg_b=gla_wg_b, gla_bg_b=gla_bg_b, gla_out_norm=gla_out_norm, w_out=w_out,
             moe_w_group=moe_w_group, moe_b_group=moe_b_group, moe_w_expert=moe_w_expert,
             moe_b_expert=moe_b_expert, moe_w1=moe_w1, moe_w3=moe_w3, moe_w2=moe_w2)
    n_batch, seq, d = x.shape
    n_ctx = ctx.shape[1]
    depth = ada_w.shape[0]
    assert d == D_MODEL and n_batch <= CTX_MOD_ROW and seq % T_LOC == 0 and seq % TM_IN == 0 and n_ctx % TM_SUB == 0
    assert (n_batch * n_ctx) % T_LOC == 0

    c_rows = jnp.zeros((MOD_ROWS, d), F32).at[0:n_batch].set(c).at[CTX_MOD_ROW].set(c_ctx)
    mod = _ada(c_rows, ada_w, ada_b)

    cos, sin = _rope_tables(seq)
    ones_c = jnp.ones((min(TM_IN, n_ctx), MLA_ROPE), F32)
    gla_consts_np = _gla_constants()
    gla_consts = (jnp.asarray(gla_consts_np[0], BF16), jnp.asarray(gla_consts_np[1], F32))
    rconst = _router_constants()
    zero_state = jnp.zeros((2, n_batch, GLA_DV, GLA_QK_W), F32)

    x2 = x.reshape(n_batch * seq, d)
    ctx2 = ctx.reshape(n_batch * n_ctx, d)
    for l in range(depth):
        ctx_out = l < depth - 1
        lw = _layer_weights(l, p)
        mod_rows = mod[l].reshape(MOD_ROWS, 1, 6 * d)
        tabs_x = _position_tables(cos, sin, mla_q_gain[l], mla_k_gain[l])
        tabs_c = _position_tables(ones_c, 0.0 * ones_c, mla_q_gain[l], mla_k_gain[l])

        q_x, k_x, v_x, b_x, gla_x, r_x = _inproj(x2, n_batch, mod_rows, 0, lw, tabs_x)
        q_c, k_c, v_c, b_c, gla_c, r_c = _inproj(ctx2, n_batch, mod_rows, CTX_MOD_ROW, lw, tabs_c)

        of_c, ob_c, s_ctx = _gla(gla_c.reshape(n_batch, n_ctx, -1), zero_state, gla_consts)
        of_x, ob_x, _ = _gla(gla_x.reshape(n_batch, seq, -1), s_ctx, gla_consts)
        a_x, lw["w1"], lw["w3"], lw["w2"] = _attention(q_x, [k_x, k_c], [v_x, v_c],
                                                       casts=(moe_w1, moe_w3, moe_w2), cast_layer=l)

        x1, h2, meta, cnt = _outproj(a_x.reshape(n_batch * seq, -1), b_x, of_x.reshape(n_batch * seq, -1),
                                     ob_x.reshape(n_batch * seq, -1), r_x, x2, n_batch, mod_rows, 0, lw, rconst)
        if ctx_out:
            a_c, = _attention(q_c, [k_c], [v_c])
            c1, h2c, meta_c, cnt_c = _outproj(a_c.reshape(n_batch * n_ctx, -1), b_c,
                                              of_c.reshape(n_batch * n_ctx, -1), ob_c.reshape(n_batch * n_ctx, -1),
                                              r_c, ctx2, n_batch, mod_rows, CTX_MOD_ROW, lw, rconst)
            x2, ctx2 = _moe([h2, h2c], [meta, meta_c], [cnt, cnt_c], [x1, c1], n_batch, mod_rows, [0, CTX_MOD_ROW], lw)
        else:
            x2, = _moe([h2], [meta], [cnt], [x1], n_batch, mod_rows, [0], lw)
    return x2.reshape(n_batch, seq, d)
```

```python
import functools

import numpy as np
import jax
import jax.numpy as jnp
from jax import lax
from jax.experimental import pallas as pl
from jax.experimental.pallas import tpu as pltpu

F32 = jnp.float32
BF16 = jnp.bfloat16

D_MODEL = 2048
GRID_W = 64
EPS = 1e-6

MLA_HEADS = 8
MLA_Q_RANK = 512
MLA_KV_RANK = 512
MLA_NOPE = 128
MLA_ROPE = 64
MLA_V = 128
MLA_V_AUG = 2 * MLA_V
MLA_QK = MLA_NOPE + MLA_ROPE
MLA_SCALE = MLA_QK ** -0.5
ROPE_THETA = 10000.0
LOG2_E = float(np.log2(np.e))

SGU_WIDTH = 512
SGU_GROUPS = 4
SGU_GROUP_DIM = SGU_WIDTH // SGU_GROUPS
SGU_CHUNK = 128

GLA_HEADS = 4
GLA_DK = 64
GLA_DV = 128
GLA_GATE_RANK = 16
GLA_GATE_NORM = 16.0
GLA_CHUNK = 64
GLA_STEP_CHUNKS = 4
GLA_QK_W = GLA_HEADS * GLA_DK
GLA_V_W = GLA_HEADS * GLA_DV

MOE_GROUPS = 4
MOE_PER_GROUP = 4
MOE_EXPERTS = MOE_GROUPS * MOE_PER_GROUP
D_EXPERT = 1024

_IN_SPLITS = (512, 512, 64, 512, 512, 256, 256, 512, 512, 16, 16)
_IN_OFF = np.concatenate([[0], np.cumsum(_IN_SPLITS)]).tolist()

Z_CQ, Z_CKV, Z_ZU, Z_ZV = 0, 512, 1024, 1536
Z_GQ, Z_GK, Z_GV, Z_GR = 2048, 2304, 2560, 3072
Z_KPE, Z_GG = 3584, 3712
Z_WIDTH = 3840

MLA_Q_COLS = MLA_HEADS * (MLA_NOPE + 2 * MLA_ROPE)
MLA_KV_COLS = MLA_HEADS * (MLA_NOPE + MLA_V)

LANES = 128
SUBLANES = 8
BF16_ROWS = 16
VMEM_LIMIT = 56 * 1024 * 1024
VMEM_LIMIT_INPROJ = 61 * 1024 * 1024

MOD_ROWS = SUBLANES
CTX_MOD_ROW = 4
ROUTER_ROWS = 2 * BF16_ROWS
META_ROWS = SUBLANES

ADA_TN = 1536
TM_IN = 512
TM_SUB = 256
TQ_ATTN = 1024
TK_ATTN = 2048
T_LOC = 512
SLOTS = 2 * T_LOC + MOE_EXPERTS * BF16_ROWS
TM_FFN = 512
F_CHUNK = 256
_PIECE_SIZES = (512, 256, 128, 64, 32, 16)


def _cparams(sem, vmem=VMEM_LIMIT):
    return pltpu.CompilerParams(dimension_semantics=sem, vmem_limit_bytes=vmem)


def _silu(x):
    return x / (1.0 + jnp.exp(-x))


def _gelu_tanh(x):
    return 0.5 * x * (1.0 + jnp.tanh(np.sqrt(2.0 / np.pi).astype(np.float32) * (x + 0.044715 * (x * x * x))))


def _log_sigmoid(x):
    return jnp.minimum(x, 0.0) - jnp.log(1.0 + jnp.exp(-jnp.abs(x)))


def _dot(a, b):
    return jnp.dot(a, b, preferred_element_type=F32)


def _dot_nt(a, b):
    return lax.dot_general(a, b, (((1,), (1,)), ((), ())), preferred_element_type=F32)


def _dot_tn(a, b):
    return lax.dot_general(a, b, (((0,), (0,)), ((), ())), preferred_element_type=F32)


def _ada_kernel(c_ref, w_ref, b_ref, o_ref):
    a = _silu(c_ref[...]).astype(BF16)
    o_ref[0] = _dot(a, w_ref[0].astype(BF16)) + b_ref[0]


def _ada(c_rows, ada_w, ada_b):
    depth, d, n = ada_w.shape
    tn = ADA_TN
    return pl.pallas_call(
        _ada_kernel,
        grid=(depth, n // tn),
        in_specs=[
            pl.BlockSpec((MOD_ROWS, d), lambda l, j: (0, 0)),
            pl.BlockSpec((1, d, tn), lambda l, j: (l, 0, j)),
            pl.BlockSpec((1, 1, tn), lambda l, j: (l, 0, j)),
        ],
        out_specs=pl.BlockSpec((1, MOD_ROWS, tn), lambda l, j: (l, 0, j)),
        out_shape=jax.ShapeDtypeStruct((depth, MOD_ROWS, n), F32),
        compiler_params=_cparams(("arbitrary", "arbitrary")),
        name="ada",
    )(c_rows, ada_w, ada_b.reshape(depth, 1, n))


def _inproj_kernel(x_ref, mod_ref, n1_ref, win_ref, qn_ref, wuq_ref, kvn_ref, wukv_ref,
                   gqn_ref, gkn_ref, cq_ref, sq_ref, tk_ref,
                   sgn_ref, sgw_ref, sgb_ref, wg_ref, bg_ref,
                   q_out, k_out, v_out, b_out, gla_out, r_out):
    for t in range(x_ref.shape[0] // TM_SUB):
        _inproj_rows(slice(t * TM_SUB, (t + 1) * TM_SUB), x_ref, mod_ref, n1_ref, win_ref, qn_ref, wuq_ref,
                     kvn_ref, wukv_ref, gqn_ref, gkn_ref, cq_ref, sq_ref, tk_ref, sgn_ref, sgw_ref, sgb_ref,
                     wg_ref, bg_ref, q_out, k_out, v_out, b_out, gla_out, r_out)


def _inproj_rows(rs, x_ref, mod_ref, n1_ref, win_ref, qn_ref, wuq_ref, kvn_ref, wukv_ref,
                 gqn_ref, gkn_ref, cq_ref, sq_ref, tk_ref, sgn_ref, sgw_ref, sgb_ref, wg_ref, bg_ref,
                 q_out, k_out, v_out, b_out, gla_out, r_out):
    d = D_MODEL
    tm = rs.stop - rs.start
    x = x_ref[rs, :]
    shift = mod_ref[0, :, 0:d]
    scale = mod_ref[0, :, d:2 * d]
    ms = jnp.mean(x * x, axis=-1, keepdims=True)
    h = (x * lax.rsqrt(ms + EPS) * n1_ref[...]) * (1.0 + scale) + shift
    hb = h.astype(BF16)
    z_mla = _dot(hb, win_ref[:, 0:Z_ZU])
    z_end = _dot(hb, win_ref[:, Z_KPE:Z_WIDTH])
    z_mid = _dot(hb, win_ref[:, Z_ZU:Z_KPE])

    cq = z_mla[:, Z_CQ:Z_CQ + MLA_Q_RANK]
    ckv = z_mla[:, Z_CKV:Z_CKV + MLA_KV_RANK]
    cqn = cq * lax.rsqrt(jnp.mean(cq * cq, axis=-1, keepdims=True) + EPS) * qn_ref[...]
    ckvn = ckv * lax.rsqrt(jnp.mean(ckv * ckv, axis=-1, keepdims=True) + EPS) * kvn_ref[...]
    qa = _dot(cqn.astype(BF16), wuq_ref[...])
    kv = _dot(ckvn.astype(BF16), wukv_ref[...])
    npe = MLA_HEADS * MLA_NOPE
    rw = MLA_HEADS * MLA_ROPE
    q_pe = qa[:, npe:npe + rw]
    q_pe_sq = q_pe * q_pe
    lane_pair = lax.broadcasted_iota(jnp.int32, (1, LANES), 1)
    q_rot = q_pe * cq_ref[rs, :] + qa[:, npe + rw:npe + 2 * rw] * sq_ref[rs, :]
    zk = z_end[:, 0:2 * MLA_ROPE]
    tkz = zk * tk_ref[rs, :]
    k_rot = tkz[:, 0:MLA_ROPE] + tkz[:, MLA_ROPE:2 * MLA_ROPE]
    k_pe = zk[:, 0:MLA_ROPE]
    ss_kpe = jnp.sum(k_pe * k_pe, axis=-1, keepdims=True)
    gqn = gqn_ref[...]
    gkn = gkn_ref[...]
    for hh in range(MLA_HEADS):
        qn_h = qa[:, hh * MLA_NOPE:(hh + 1) * MLA_NOPE]
        pair = q_pe_sq[:, (hh // 2) * LANES:(hh // 2 + 1) * LANES]
        own = jnp.where((lane_pair < MLA_ROPE) == (hh % 2 == 0), pair, 0.0)
        ssq = jnp.sum(qn_h * qn_h + own, axis=-1, keepdims=True)
        rq = lax.rsqrt(ssq * (1.0 / MLA_QK) + EPS) * (MLA_SCALE * LOG2_E)
        q_h = jnp.concatenate([qn_h * gqn * rq, q_rot[:, hh * MLA_ROPE:(hh + 1) * MLA_ROPE] * rq], axis=-1)
        q_out[0, hh, rs, :] = q_h.astype(BF16)
        kn_h = kv[:, hh * MLA_NOPE:(hh + 1) * MLA_NOPE]
        ssk = jnp.sum(kn_h * kn_h, axis=-1, keepdims=True) + ss_kpe
        rk = lax.rsqrt(ssk * (1.0 / MLA_QK) + EPS)
        k_h = jnp.concatenate([kn_h * gkn * rk, k_rot * rk], axis=-1)
        k_out[0, hh, rs, :] = k_h.astype(BF16)
        v_h = kv[:, npe + hh * MLA_V:npe + (hh + 1) * MLA_V]
        v_out[0, hh, rs, :] = jnp.concatenate([v_h, jnp.ones_like(v_h)], axis=-1).astype(BF16)

    u = _gelu_tanh(z_mid[:, 0:SGU_WIDTH])
    vv = _gelu_tanh(z_mid[:, Z_ZV - Z_ZU:Z_ZV - Z_ZU + SGU_WIDTH])
    sgn = sgn_ref[...]
    for c in range(tm // SGU_CHUNK):
        rows = slice(c * SGU_CHUNK, (c + 1) * SGU_CHUNK)
        outs = []
        for g in range(SGU_GROUPS):
            cols = slice(g * SGU_GROUP_DIM, (g + 1) * SGU_GROUP_DIM)
            seg = vv[rows, cols]
            vn = seg * lax.rsqrt(jnp.mean(seg * seg, axis=-1, keepdims=True) + EPS) * sgn[:, cols]
            outs.append(_dot(sgw_ref[g], vn.astype(BF16)))
        sp = jnp.concatenate(outs, axis=-1) + sgb_ref[...]
        b_out[rs.start + c * SGU_CHUNK:rs.start + (c + 1) * SGU_CHUNK, :] = (u[rows, :] * sp).astype(BF16)

    gl = _dot(z_end[:, Z_GG - Z_KPE:Z_GG - Z_KPE + LANES].astype(BF16), wg_ref[...]) + bg_ref[...]
    gdec = _log_sigmoid(gl) * (1.0 / GLA_GATE_NORM)
    gla_out[rs, 0:2 * GLA_QK_W] = z_mid[:, Z_GQ - Z_ZU:Z_GQ - Z_ZU + 2 * GLA_QK_W]
    gla_out[rs, 2 * GLA_QK_W:4 * GLA_QK_W] = gdec
    gla_out[rs, 4 * GLA_QK_W:4 * GLA_QK_W + GLA_V_W] = z_mid[:, Z_GV - Z_ZU:Z_GV - Z_ZU + GLA_V_W]
    r_out[rs, :] = z_mid[:, Z_GR - Z_ZU:Z_GR - Z_ZU + GLA_V_W]


def _inproj(x2, n_batch, mod_rows, mod_row0, lw, tabs):
    n_total = x2.shape[0]
    n_tok = n_total // n_batch
    tm = min(TM_IN, n_tok)
    tpb = n_tok // tm
    tab_blocks = tabs["cq"].shape[0] // tm
    if mod_row0 == 0:
        mod_map = lambda i: (i // tpb, 0, 0)
    else:
        mod_map = lambda i: (mod_row0, 0, 0)
    const2 = lambda i: (0, 0)
    tab_map = lambda i: (i % tab_blocks, 0)
    tok_map = lambda i: (i, 0)
    head_map = lambda i: (i // tpb, 0, i % tpb, 0)
    in_specs = [
        pl.BlockSpec((tm, D_MODEL), tok_map),
        pl.BlockSpec((1, 1, 6 * D_MODEL), mod_map),
        pl.BlockSpec((1, D_MODEL), const2),
        pl.BlockSpec((D_MODEL, Z_WIDTH), const2, pipeline_mode=pl.Buffered(1)),
        pl.BlockSpec((1, MLA_Q_RANK), const2),
        pl.BlockSpec((MLA_Q_RANK, MLA_Q_COLS), const2, pipeline_mode=pl.Buffered(1)),
        pl.BlockSpec((1, MLA_KV_RANK), const2),
        pl.BlockSpec((MLA_KV_RANK, MLA_KV_COLS), const2, pipeline_mode=pl.Buffered(1)),
        pl.BlockSpec((1, MLA_NOPE), const2),
        pl.BlockSpec((1, MLA_NOPE), const2),
        pl.BlockSpec((tm, MLA_HEADS * MLA_ROPE), tab_map),
        pl.BlockSpec((tm, MLA_HEADS * MLA_ROPE), tab_map),
        pl.BlockSpec((tm, 2 * MLA_ROPE), tab_map),
        pl.BlockSpec((1, SGU_WIDTH), const2),
        pl.BlockSpec((SGU_GROUPS, SGU_CHUNK, SGU_CHUNK), lambda i: (0, 0, 0)),
        pl.BlockSpec((SGU_CHUNK, SGU_WIDTH), const2),
        pl.BlockSpec((LANES, 2 * GLA_QK_W), const2),
        pl.BlockSpec((1, 2 * GLA_QK_W), const2),
    ]
    out_shape = [
        jax.ShapeDtypeStruct((n_batch, MLA_HEADS, n_tok, MLA_QK), BF16),
        jax.ShapeDtypeStruct((n_batch, MLA_HEADS, n_tok, MLA_QK), BF16),
        jax.ShapeDtypeStruct((n_batch, MLA_HEADS, n_tok, MLA_V_AUG), BF16),
        jax.ShapeDtypeStruct((n_total, SGU_WIDTH), BF16),
        jax.ShapeDtypeStruct((n_total, 4 * GLA_QK_W + GLA_V_W), F32),
        jax.ShapeDtypeStruct((n_total, GLA_V_W), F32),
    ]
    out_specs = [
        pl.BlockSpec((1, MLA_HEADS, tm, MLA_QK), head_map),
        pl.BlockSpec((1, MLA_HEADS, tm, MLA_QK), head_map),
        pl.BlockSpec((1, MLA_HEADS, tm, MLA_V_AUG), head_map),
        pl.BlockSpec((tm, SGU_WIDTH), tok_map),
        pl.BlockSpec((tm, 4 * GLA_QK_W + GLA_V_W), tok_map),
        pl.BlockSpec((tm, GLA_V_W), tok_map),
    ]
    return pl.pallas_call(
        _inproj_kernel,
        grid=(n_total // tm,),
        in_specs=in_specs,
        out_specs=out_specs,
        out_shape=out_shape,
        compiler_params=_cparams(("arbitrary",), VMEM_LIMIT_INPROJ),
        name="inproj",
    )(x2, mod_rows, lw["n1"], lw["w_in"], lw["q_norm"], lw["w_uq"], lw["kv_norm"], lw["w_ukv"],
      lw["gq_nope"], lw["gk_nope"], tabs["cq"], tabs["sq"], tabs["tk"],
      lw["sgu_norm"], lw["sgu_w"], lw["sgu_b"], lw["wg"], lw["bg"])


_GLA_LEVELS = (32, 16, 8, 4, 2, 1)
_N_LEV = len(_GLA_LEVELS)
_G_SPLIT = 3
_R_QI = _N_LEV * GLA_CHUNK
_R_KS = _R_QI + GLA_CHUNK
_R_TOT = _R_KS + GLA_CHUNK
_M_ROWS = _R_TOT + SUBLANES


def _gla_constants():
    c = GLA_CHUNK
    t = np.arange(c)[:, None]
    j = np.arange(c)[None, :]
    lev, masks = [], []
    for m in _GLA_LEVELS:
        upper = (t // m) % 2 == 1
        bound = (t // (2 * m)) * 2 * m + m - 1
        lev.append(np.where(upper, (j > bound) & (j <= t), (j > t) & (j <= bound)).astype(np.float32))
        s = j
        masks.append(((t // (2 * m) == s // (2 * m)) & upper & ((s // m) % 2 == 0)).astype(np.float32))
    masks.append((t == j).astype(np.float32))
    mqi = (j <= t).astype(np.float32)
    mks = (j > t).astype(np.float32)
    tot = np.ones((SUBLANES, c), np.float32)
    flip = lambda a: a[::-1, ::-1]
    m_f = np.concatenate(lev + [mqi, mks, tot], axis=0)
    m_b = np.concatenate([flip(a) for a in lev] + [flip(mqi), flip(mks), tot], axis=0)
    mask_f = np.stack([np.tile(a, (GLA_HEADS, 1)) for a in masks])
    mask_b = np.stack([np.tile(flip(a), (GLA_HEADS, 1)) for a in masks])
    m_all = np.tile(np.stack([m_f, m_b]), (1, 1, _G_SPLIT))
    return m_all, np.stack([mask_f, mask_b])


def _stack_heads(a, lane_head):
    return jnp.concatenate([jnp.where(lane_head == hh, a, jnp.zeros_like(a)) for hh in range(GLA_HEADS)], axis=0)


def _gla_chunk(q, k, g, v, st, mall, masks, lane_head):
    c = GLA_CHUNK
    g_hi = g.astype(BF16)
    r1 = g - g_hi.astype(F32)
    g_mid = r1.astype(BF16)
    g_lo = (r1 - g_mid.astype(F32)).astype(BF16)
    e = _dot(mall, jnp.concatenate([g_hi, g_mid, g_lo], axis=0))
    a = masks[_N_LEV] * _dot_nt(_stack_heads(q.astype(BF16), lane_head), k.astype(BF16))
    for lev in range(_N_LEV):
        ex = jnp.exp(e[lev * c:(lev + 1) * c])
        a = a + masks[lev] * _dot_nt(_stack_heads((q * ex).astype(BF16), lane_head), (k * ex).astype(BF16))
    qi = (q * jnp.exp(e[_R_QI:_R_QI + c])).astype(BF16)
    o_inter = _dot_nt(_stack_heads(qi, lane_head), st.astype(BF16))
    vb = v.astype(BF16)
    a16 = a.astype(BF16)
    outs = []
    for hh in range(GLA_HEADS):
        rows = slice(hh * c, (hh + 1) * c)
        outs.append(o_inter[rows] + _dot(a16[rows], vb[:, hh * GLA_DV:(hh + 1) * GLA_DV]))
    o = jnp.concatenate(outs, axis=-1)
    kd = (k * jnp.exp(e[_R_KS:_R_KS + c])).astype(BF16)
    upd = _dot_tn(vb, kd)
    st_new = st * jnp.exp(e[_R_TOT:_R_TOT + 1])
    for hh in range(GLA_HEADS):
        st_new = st_new + jnp.where(lane_head == hh, upd[hh * GLA_DV:(hh + 1) * GLA_DV], 0.0)
    return o, st_new


def _gla_kernel(fwd_ref, bwd_ref, s0_ref, mall_ref, mask_ref, of_ref, ob_ref, sfin_ref, st_ref):
    step = pl.program_id(0)
    n_batch = fwd_ref.shape[0]

    @pl.when(step == 0)
    def _():
        st_ref[...] = s0_ref[...]

    lane_head = lax.broadcasted_iota(jnp.int32, (1, GLA_QK_W), 1) // GLA_DK
    w = GLA_QK_W
    n_sub = fwd_ref.shape[1] // GLA_CHUNK
    for b in range(n_batch):
        for direction, (in_ref, o_ref) in enumerate(((fwd_ref, of_ref), (bwd_ref, ob_ref))):
            mall = mall_ref[direction]
            masks = [mask_ref[direction, i] for i in range(_N_LEV + 1)]
            st = st_ref[direction, b]
            for sub in range(n_sub):
                c0 = (sub if direction == 0 else n_sub - 1 - sub) * GLA_CHUNK
                rows = slice(c0, c0 + GLA_CHUNK)
                q = in_ref[b, rows, 0:w]
                k = in_ref[b, rows, w:2 * w]
                g = in_ref[b, rows, (2 + direction) * w:(3 + direction) * w]
                v = in_ref[b, rows, 4 * w:4 * w + GLA_V_W]
                o, st = _gla_chunk(q, k, g, v, st, mall, masks, lane_head)
                o_ref[b, rows, :] = o
            st_ref[direction, b] = st

    @pl.when(step == pl.num_programs(0) - 1)
    def _():
        sfin_ref[...] = st_ref[...]


def _gla(gla_in, s0, consts):
    n_batch, n_tok, width = gla_in.shape
    blk = GLA_STEP_CHUNKS * GLA_CHUNK
    nc = n_tok // blk
    mall, masks = consts
    st_shape = (2, n_batch, GLA_DV, GLA_QK_W)
    return pl.pallas_call(
        _gla_kernel,
        grid=(nc,),
        in_specs=[
            pl.BlockSpec((n_batch, blk, width), lambda c: (0, c, 0)),
            pl.BlockSpec((n_batch, blk, width), lambda c: (0, nc - 1 - c, 0)),
            pl.BlockSpec(st_shape, lambda c: (0, 0, 0, 0)),
            pl.BlockSpec(mall.shape, lambda c: (0, 0, 0)),
            pl.BlockSpec(masks.shape, lambda c: (0, 0, 0, 0)),
        ],
        out_specs=[
            pl.BlockSpec((n_batch, blk, GLA_V_W), lambda c: (0, c, 0)),
            pl.BlockSpec((n_batch, blk, GLA_V_W), lambda c: (0, nc - 1 - c, 0)),
            pl.BlockSpec(st_shape, lambda c: (0, 0, 0, 0)),
        ],
        out_shape=[
            jax.ShapeDtypeStruct((n_batch, n_tok, GLA_V_W), F32),
            jax.ShapeDtypeStruct((n_batch, n_tok, GLA_V_W), F32),
            jax.ShapeDtypeStruct(st_shape, F32),
        ],
        scratch_shapes=[pltpu.VMEM(st_shape, F32)],
        compiler_params=_cparams(("arbitrary",)),
        name="gla",
    )(gla_in, gla_in, s0, mall, masks)


def _attn_kernel(*refs, n_seg, n_cast):
    q_ref = refs[0]
    k_refs = refs[1:1 + n_seg]
    v_refs = refs[1 + n_seg:1 + 2 * n_seg]
    cast_in = refs[1 + 2 * n_seg:1 + 2 * n_seg + n_cast]
    o_ref = refs[1 + 2 * n_seg + n_cast]
    cast_out = refs[2 + 2 * n_seg + n_cast:]
    for src, dst in zip(cast_in, cast_out):
        dst[...] = src[...].astype(BF16)
    q = q_ref[0, 0]
    n_main = k_refs[0].shape[2]
    tk = min(TK_ATTN, n_main)
    groups = [[(0, slice(c * tk, (c + 1) * tk))] for c in range(n_main // tk)]
    groups[0] += [(i, slice(0, k_refs[i].shape[2])) for i in range(1, n_seg)]
    m = acc = None
    for group in groups:
        scores = [_dot_nt(q, k_refs[i][0, 0, keys, :]) for i, keys in group]
        m_g = functools.reduce(jnp.maximum, [jnp.max(s, axis=-1, keepdims=True) for s in scores])
        m_new = m_g if m is None else jnp.maximum(m, m_g)
        acc_g = functools.reduce(jnp.add, [_dot(jnp.exp2(s - m_new).astype(BF16), v_refs[i][0, 0, keys, :])
                                           for s, (i, keys) in zip(scores, group)])
        acc = acc_g if m is None else jnp.exp2(m - m_new) * acc + acc_g
        m = m_new
    o_ref[0] = (acc[:, 0:MLA_V] / acc[:, MLA_V:MLA_V_AUG]).astype(BF16)


def _attention(q, ks, vs, casts=(), cast_layer=0):
    n_batch, n_heads, nq, _ = q.shape
    tq = min(TQ_ATTN, nq)
    n_seg = len(ks)
    n_q = nq // tq
    n_steps = n_batch * n_heads * n_q
    in_specs = [pl.BlockSpec((1, 1, tq, MLA_QK), lambda b, h, i: (b, h, i, 0))]
    for k in ks:
        in_specs.append(pl.BlockSpec((1, 1, k.shape[2], MLA_QK), lambda b, h, i: (b, h, 0, 0)))
    for v in vs:
        in_specs.append(pl.BlockSpec((1, 1, v.shape[2], MLA_V_AUG), lambda b, h, i: (b, h, 0, 0)))
    out_specs = [pl.BlockSpec((1, tq, MLA_V), lambda b, h, i: (b, i, h))]
    out_shape = [jax.ShapeDtypeStruct((n_batch, nq, n_heads * MLA_V), BF16)]
    slabs = []
    for w in casts:
        depth, cols = w.shape[0], w.shape[-1]
        rows = w.size // (depth * cols * n_steps)
        assert rows * depth * cols * n_steps == w.size and rows % BF16_ROWS == 0
        slabs.append(w.reshape(depth * n_steps, rows, cols))
        in_specs.append(pl.BlockSpec(
            (1, rows, cols), lambda b, h, i: (cast_layer * n_steps + (b * n_heads + h) * n_q + i, 0, 0)))
        out_specs.append(pl.BlockSpec((1, rows, cols), lambda b, h, i: ((b * n_heads + h) * n_q + i, 0, 0)))
        out_shape.append(jax.ShapeDtypeStruct((n_steps, rows, cols), BF16))
    outs = pl.pallas_call(
        functools.partial(_attn_kernel, n_seg=n_seg, n_cast=len(casts)),
        grid=(n_batch, n_heads, n_q),
        in_specs=in_specs,
        out_specs=out_specs,
        out_shape=out_shape,
        compiler_params=_cparams(("arbitrary", "arbitrary", "arbitrary")),
        name="attn",
    )(q, *ks, *vs, *slabs)
    return [outs[0]] + [o.reshape(w.shape[1:]) for o, w in zip(outs[1:], casts)]


def _outproj_kernel(a_ref, b_ref, of_ref, ob_ref, r_ref, x_ref, mod_ref, gon_ref, n2_ref, wout_ref,
                    wr_hi_ref, wr_lo_ref, br_ref, lst_ref, sl_ref,
                    x1_out, h2_out, meta_out, cnt_out):
    d = D_MODEL
    tm = x_ref.shape[0]
    gate1 = mod_ref[0, :, 2 * d:3 * d]
    shift2 = mod_ref[0, :, 3 * d:4 * d]
    scale2 = mod_ref[0, :, 4 * d:5 * d]

    og = of_ref[...] + ob_ref[...]
    gon = gon_ref[...]
    parts = []
    for hh in range(GLA_HEADS):
        seg = og[:, hh * GLA_DV:(hh + 1) * GLA_DV]
        parts.append(seg * lax.rsqrt(jnp.mean(seg * seg, axis=-1, keepdims=True) + EPS) * gon)
    cg = jnp.concatenate(parts, axis=-1) * _silu(r_ref[...])
    na = MLA_HEADS * MLA_V
    y = _dot(a_ref[...], wout_ref[0:na, :])
    y = y + _dot(b_ref[...], wout_ref[na:na + SGU_WIDTH, :])
    y = y + _dot(cg.astype(BF16), wout_ref[na + SGU_WIDTH:na + SGU_WIDTH + GLA_V_W, :])
    x1 = x_ref[...] + gate1 * y
    x1_out[...] = x1

    ms = jnp.mean(x1 * x1, axis=-1, keepdims=True)
    h2 = (x1 * lax.rsqrt(ms + EPS) * n2_ref[...]) * (1.0 + scale2) + shift2
    h_hi = h2.astype(BF16)
    h2_out[...] = h_hi
    h_lo = (h2 - h_hi.astype(F32)).astype(BF16)

    wr_hi = wr_hi_ref[...]
    lt2 = _dot_nt(jnp.concatenate([wr_hi, wr_lo_ref[...]], axis=0), h_hi)
    lt = lt2[0:ROUTER_ROWS] + lt2[ROUTER_ROWS:2 * ROUTER_ROWS] + _dot_nt(wr_hi, h_lo) + br_ref[:, 0:1]
    grp = [lt[MOE_EXPERTS + g:MOE_EXPERTS + g + 1, :] for g in range(MOE_GROUPS)]
    gmax = functools.reduce(jnp.maximum, grp)
    gsel = jnp.full(gmax.shape, MOE_GROUPS - 1, jnp.int32)
    for g in range(MOE_GROUPS - 2, -1, -1):
        gsel = jnp.where(grp[g] == gmax, g, gsel)
    gsum = functools.reduce(jnp.add, [jnp.exp(gl - gmax) for gl in grp])
    g_w = 1.0 / gsum
    e_in = []
    for j in range(MOE_PER_GROUP):
        ej = lt[j:j + 1, :]
        for g in range(1, MOE_GROUPS):
            ej = jnp.where(gsel == g, lt[g * MOE_PER_GROUP + j:g * MOE_PER_GROUP + j + 1, :], ej)
        e_in.append(ej)
    m1 = functools.reduce(jnp.maximum, e_in)
    i1 = jnp.full(m1.shape, MOE_PER_GROUP - 1, jnp.int32)
    for j in range(MOE_PER_GROUP - 2, -1, -1):
        i1 = jnp.where(e_in[j] == m1, j, i1)
    e_rest = [jnp.where(i1 == j, -jnp.inf, e_in[j]) for j in range(MOE_PER_GROUP)]
    m2 = functools.reduce(jnp.maximum, e_rest)
    i2 = jnp.full(m2.shape, MOE_PER_GROUP - 1, jnp.int32)
    for j in range(MOE_PER_GROUP - 2, -1, -1):
        i2 = jnp.where(e_rest[j] == m2, j, i2)
    p2 = jnp.exp(m2 - m1)
    w1 = g_w / (1.0 + p2)
    w2 = g_w * p2 / (1.0 + p2)
    eid1 = gsel * MOE_PER_GROUP + i1
    eid2 = gsel * MOE_PER_GROUP + i2

    e_iota = lax.broadcasted_iota(jnp.int32, (MOE_EXPERTS, tm), 0)
    hit1 = e_iota == eid1
    hit2 = e_iota == eid2
    onehot = jnp.where(hit1, 1.0, 0.0) + jnp.where(hit2, 1.0, 0.0)
    rank = _dot(onehot.astype(BF16), lst_ref[...])
    counts = jnp.sum(onehot, axis=1, keepdims=True)
    padded = jnp.ceil(counts * (1.0 / BF16_ROWS)) * BF16_ROWS
    padded_l = jnp.broadcast_to(padded, (MOE_EXPERTS, LANES))
    off = _dot(sl_ref[...], padded_l.astype(BF16))
    pos_e = off[:, 0:1] + rank
    pos1 = jnp.sum(jnp.where(hit1, pos_e, 0.0), axis=0, keepdims=True)
    pos2 = jnp.sum(jnp.where(hit2, pos_e, 0.0), axis=0, keepdims=True)
    zero = jnp.zeros_like(pos1)
    meta_out[0] = jnp.concatenate(
        [pos1, pos2, w1, w2, eid1.astype(F32), eid2.astype(F32), zero, zero], axis=0)
    cnt_out[0] = padded_l


def _outproj(a, b, o_f, o_b, r, x2, n_batch, mod_rows, mod_row0, lw, rconst):
    n_total = x2.shape[0]
    n_tok = n_total // n_batch
    tm = T_LOC
    tpb = n_tok // tm
    if mod_row0 == 0:
        mod_map = lambda i: (i // tpb, 0, 0)
    else:
        mod_map = lambda i: (mod_row0, 0, 0)
    const2 = lambda i: (0, 0)
    tok_map = lambda i: (i, 0)
    n_tiles = n_total // tm
    in_specs = [
        pl.BlockSpec((tm, MLA_HEADS * MLA_V), tok_map),
        pl.BlockSpec((tm, SGU_WIDTH), tok_map),
        pl.BlockSpec((tm, GLA_V_W), tok_map),
        pl.BlockSpec((tm, GLA_V_W), tok_map),
        pl.BlockSpec((tm, GLA_V_W), tok_map),
        pl.BlockSpec((tm, D_MODEL), tok_map),
        pl.BlockSpec((1, 1, 6 * D_MODEL), mod_map),
        pl.BlockSpec((1, GLA_DV), const2),
        pl.BlockSpec((1, D_MODEL), const2),
        pl.BlockSpec((D_MODEL, D_MODEL), const2, pipeline_mode=pl.Buffered(1)),
        pl.BlockSpec((ROUTER_ROWS, D_MODEL), const2),
        pl.BlockSpec((ROUTER_ROWS, D_MODEL), const2),
        pl.BlockSpec((ROUTER_ROWS, LANES), const2),
        pl.BlockSpec((tm, tm), const2),
        pl.BlockSpec((MOE_EXPERTS, MOE_EXPERTS), const2),
    ]
    out_shape = [
        jax.ShapeDtypeStruct((n_total, D_MODEL), F32),
        jax.ShapeDtypeStruct((n_total, D_MODEL), BF16),
        jax.ShapeDtypeStruct((n_tiles, META_ROWS, tm), F32),
        jax.ShapeDtypeStruct((n_tiles, MOE_EXPERTS, LANES), F32),
    ]
    out_specs = [
        pl.BlockSpec((tm, D_MODEL), tok_map),
        pl.BlockSpec((tm, D_MODEL), tok_map),
        pl.BlockSpec((1, META_ROWS, tm), lambda i: (i, 0, 0)),
        pl.BlockSpec((1, MOE_EXPERTS, LANES), lambda i: (i, 0, 0)),
    ]
    return pl.pallas_call(
        _outproj_kernel,
        grid=(n_tiles,),
        in_specs=in_specs,
        out_specs=out_specs,
        out_shape=out_shape,
        compiler_params=_cparams(("arbitrary",)),
        name="outproj",
    )(a, b, o_f, o_b, r, x2, mod_rows, lw["gla_out_norm"], lw["n2"], lw["w_out"],
      lw["wr_hi"], lw["wr_lo"], lw["br"], rconst["lst"], rconst["sl"])


def _piece_copies(src_ref, dst_ref, sem, src_row, dst_row, n_rows):
    copies = []
    for size in _PIECE_SIZES:
        done = n_rows & ~(2 * size - 1)
        pred = (n_rows & size) != 0
        s0 = pl.multiple_of(src_row + done, BF16_ROWS)
        d0 = pl.multiple_of(dst_row + done, BF16_ROWS)
        cp = pltpu.make_async_copy(src_ref.at[pl.ds(s0, size)], dst_ref.at[pl.ds(d0, size)], sem)
        copies.append((pred, cp))
    return copies


def _list_base(tile, s):
    return tile * len(_PIECE_SIZES) + s


def _start_listed_pieces(src_ref, dst_ref, sem, src_tab, dst_tab, n_tab, tile):
    for s, size in enumerate(_PIECE_SIZES):
        base = _list_base(tile, s)

        def body(i, carry, size=size, base=base):
            idx = base * MOE_EXPERTS + i
            s0 = pl.multiple_of(src_tab[idx], BF16_ROWS)
            d0 = pl.multiple_of(dst_tab[idx], BF16_ROWS)
            pltpu.make_async_copy(src_ref.at[pl.ds(s0, size)], dst_ref.at[pl.ds(d0, size)], sem).start()
            return carry

        lax.fori_loop(0, n_tab[base], body, 0)


def _wait_listed_pieces(src_ref, dst_ref, sem, n_tab, tile):
    for s, size in enumerate(_PIECE_SIZES):
        def body(i, carry, size=size):
            pltpu.make_async_copy(src_ref.at[pl.ds(0, size)], dst_ref.at[pl.ds(0, size)], sem).wait()
            return carry

        lax.fori_loop(0, n_tab[_list_base(tile, s)], body, 0)


def _stream_starts(stream_tiles):
    return [sum(stream_tiles[:s]) for s in range(len(stream_tiles))]


def _stream_block(i, start, n):
    return jnp.clip(i - start, 0, n - 1)


def _dispatch_kernel(ploc_tab, pdst_tab, pn_tab, ztab_dst, ztab_cnt, n_used, *refs, stream_tiles):
    n_streams = len(stream_tiles)
    h2_refs = refs[:n_streams]
    meta_ref, xs_ref, xloc_ref, zeros_ref, sems = refs[n_streams:]
    j = pl.program_id(0)
    n_tiles = pl.num_programs(0)
    tm = h2_refs[0].shape[0]
    slot = j % 2

    @pl.when(j == 0)
    def _():
        zeros_ref[...] = jnp.zeros_like(zeros_ref)

        def fill(t, carry, start):
            cp = pltpu.make_async_copy(zeros_ref, xs_ref.at[pl.ds(pl.multiple_of(t * TM_FFN, TM_FFN), TM_FFN)],
                                       sems.at[0])
            if start:
                cp.start()
            else:
                cp.wait()
            return carry
        lax.fori_loop(n_used[0], xs_ref.shape[0] // TM_FFN, functools.partial(fill, start=True), 0)
        lax.fori_loop(n_used[0], xs_ref.shape[0] // TM_FFN, functools.partial(fill, start=False), 0)
        zcopies = []
        for e in range(MOE_EXPERTS):
            zcopies += _piece_copies(zeros_ref, xs_ref, sems.at[0], 0, ztab_dst[e], ztab_cnt[e])
        for pred, cp in zcopies:
            @pl.when(pred)
            def _(cp=cp):
                cp.start()
        for pred, cp in zcopies:
            @pl.when(pred)
            def _(cp=cp):
                cp.wait()

    pos1 = meta_ref[0, 0:1, :]
    pos2 = meta_ref[0, 1:2, :]
    s_iota = lax.broadcasted_iota(jnp.int32, (SLOTS, tm), 0).astype(F32)
    perm = jnp.where((s_iota == pos1) | (s_iota == pos2), 1.0, 0.0).astype(BF16)
    h2 = h2_refs[0][...]
    for s, start in list(enumerate(_stream_starts(stream_tiles)))[1:]:
        h2 = jnp.where(j >= start, h2_refs[s][...], h2)
    xloc_ref[slot] = _dot(perm, h2).astype(BF16)

    @pl.when(j > 0)
    def _():
        _wait_listed_pieces(xloc_ref.at[1 - slot], xs_ref, sems.at[1 - slot], pn_tab, j - 1)

    _start_listed_pieces(xloc_ref.at[slot], xs_ref, sems.at[slot], ploc_tab, pdst_tab, pn_tab, j)

    @pl.when(j == n_tiles - 1)
    def _():
        _wait_listed_pieces(xloc_ref.at[slot], xs_ref, sems.at[slot], pn_tab, j)


def _dispatch(h2s, meta, tabs, r_max):
    n_tiles = meta.shape[0]
    tm = T_LOC
    stream_tiles = tuple(h.shape[0] // tm for h in h2s)
    assert sum(stream_tiles) == n_tiles
    h2_specs = [pl.BlockSpec((tm, D_MODEL), lambda i, *_, start=start, n=n: (_stream_block(i, start, n), 0))
                for start, n in zip(_stream_starts(stream_tiles), stream_tiles)]
    grid_spec = pltpu.PrefetchScalarGridSpec(
        num_scalar_prefetch=6,
        grid=(n_tiles,),
        in_specs=h2_specs + [pl.BlockSpec((1, META_ROWS, tm), lambda i, *_: (i, 0, 0))],
        out_specs=pl.BlockSpec(memory_space=pl.ANY),
        scratch_shapes=[pltpu.VMEM((2, SLOTS, D_MODEL), BF16), pltpu.VMEM((TM_FFN, D_MODEL), BF16),
                        pltpu.SemaphoreType.DMA((2,))],
    )
    return pl.pallas_call(
        functools.partial(_dispatch_kernel, stream_tiles=stream_tiles),
        grid_spec=grid_spec,
        out_shape=jax.ShapeDtypeStruct((r_max, D_MODEL), BF16),
        compiler_params=_cparams(("arbitrary",)),
        name="dispatch",
    )(tabs["ploc"], tabs["pdst"], tabs["pn"], tabs["zdst"], tabs["zcnt"], tabs["n_used"], *h2s, meta)


def _ffn_kernel(tile_e, tile_blk, n_used, x_ref, w1_ref, w3_ref, w2_ref, y_ref):
    i = pl.program_id(0)

    @pl.when(i < n_used[0])
    def _():
        x = x_ref[...]
        acc = None
        for f in range(D_EXPERT // F_CHUNK):
            cols = slice(f * F_CHUNK, (f + 1) * F_CHUNK)
            h1 = _dot(x, w1_ref[0, :, cols])
            h3 = _dot(x, w3_ref[0, :, cols])
            act = (_silu(h1) * h3).astype(BF16)
            part = _dot(act, w2_ref[0, cols, :])
            acc = part if acc is None else acc + part
        y_ref[...] = acc.astype(BF16)

    @pl.when(i >= n_used[0])
    def _():
        y_ref[...] = jnp.zeros_like(y_ref)


def _ffn(xs, w1, w3, w2, tabs):
    r_max = xs.shape[0]
    n_mt = r_max // TM_FFN
    grid_spec = pltpu.PrefetchScalarGridSpec(
        num_scalar_prefetch=3,
        grid=(n_mt,),
        in_specs=[
            pl.BlockSpec((TM_FFN, D_MODEL), lambda i, te, tb, nu: (tb[i], 0)),
            pl.BlockSpec((1, D_MODEL, D_EXPERT), lambda i, te, tb, nu: (te[i], 0, 0)),
            pl.BlockSpec((1, D_MODEL, D_EXPERT), lambda i, te, tb, nu: (te[i], 0, 0)),
            pl.BlockSpec((1, D_EXPERT, D_MODEL), lambda i, te, tb, nu: (te[i], 0, 0)),
        ],
        out_specs=pl.BlockSpec((TM_FFN, D_MODEL), lambda i, te, tb, nu: (i, 0)),
    )
    return pl.pallas_call(
        _ffn_kernel,
        grid_spec=grid_spec,
        out_shape=jax.ShapeDtypeStruct((r_max, D_MODEL), BF16),
        compiler_params=_cparams(("arbitrary",)),
        name="ffn",
    )(tabs["tile_e"], tabs["tile_blk"], tabs["n_used"], xs, w1, w3, w2)


def _combine_kernel(ploc_tab, pdst_tab, pn_tab, y_ref, metac_ref, mod_ref, *refs, stream_tiles):
    n_streams = len(stream_tiles)
    x1_refs = refs[:n_streams]
    out_refs = refs[n_streams:2 * n_streams]
    yloc_ref, sems = refs[2 * n_streams:]
    j = pl.program_id(0)
    n_tiles = pl.num_programs(0)
    d = D_MODEL
    tm = x1_refs[0].shape[0]
    slot = j % 2

    @pl.when(j == 0)
    def _():
        yloc_ref[...] = jnp.zeros_like(yloc_ref)
        _start_listed_pieces(y_ref, yloc_ref.at[0], sems.at[0], pdst_tab, ploc_tab, pn_tab, 0)

    @pl.when(j + 1 < n_tiles)
    def _():
        _start_listed_pieces(y_ref, yloc_ref.at[1 - slot], sems.at[1 - slot], pdst_tab, ploc_tab, pn_tab, j + 1)

    _wait_listed_pieces(y_ref, yloc_ref.at[slot], sems.at[slot], pn_tab, j)

    mc = metac_ref[0]
    s_iota = lax.broadcasted_iota(jnp.int32, (tm, SLOTS), 1).astype(F32)
    wc = jnp.where(s_iota == mc[:, 0:1], mc[:, 2:3], 0.0) + jnp.where(s_iota == mc[:, 1:2], mc[:, 3:4], 0.0)
    moe = _dot(wc.astype(BF16), yloc_ref[slot])
    gate2 = mod_ref[0, :, 5 * d:6 * d]
    for start, n, x1_ref, out_ref in zip(_stream_starts(stream_tiles), stream_tiles, x1_refs, out_refs):
        @pl.when((j >= start) & (j < start + n))
        def _(x1_ref=x1_ref, out_ref=out_ref):
            out_ref[...] = x1_ref[...] + gate2 * moe


def _combine(y, metac, x1s, n_batch, mod_rows, mod_row0s, tabs):
    tm = T_LOC
    stream_tiles = tuple(x1.shape[0] // tm for x1 in x1s)
    starts = _stream_starts(stream_tiles)
    n_tiles = sum(stream_tiles)

    def mod_map(i, *_):
        row = jnp.int32(0)
        for start, n, x1, row0 in zip(starts, stream_tiles, x1s, mod_row0s):
            tpb = x1.shape[0] // n_batch // tm
            row_s = _stream_block(i, start, n) // tpb if row0 == 0 else row0
            row = jnp.where(i >= start, row_s, row)
        return (row, 0, 0)

    tile_specs = [pl.BlockSpec((tm, D_MODEL), lambda i, *_, start=start, n=n: (_stream_block(i, start, n), 0))
                  for start, n in zip(starts, stream_tiles)]
    grid_spec = pltpu.PrefetchScalarGridSpec(
        num_scalar_prefetch=3,
        grid=(n_tiles,),
        in_specs=[
            pl.BlockSpec(memory_space=pl.ANY),
            pl.BlockSpec((1, tm, META_ROWS), lambda i, *_: (i, 0, 0)),
            pl.BlockSpec((1, 1, 6 * D_MODEL), mod_map),
        ] + tile_specs,
        out_specs=tile_specs,
        scratch_shapes=[pltpu.VMEM((2, SLOTS, D_MODEL), BF16), pltpu.SemaphoreType.DMA((2,))],
    )
    return pl.pallas_call(
        functools.partial(_combine_kernel, stream_tiles=stream_tiles),
        grid_spec=grid_spec,
        out_shape=[jax.ShapeDtypeStruct(x1.shape, F32) for x1 in x1s],
        compiler_params=_cparams(("arbitrary",)),
        name="combine",
    )(tabs["ploc"], tabs["pdst"], tabs["pn"], y, metac, mod_rows, *x1s)


def _rope_tables(n_tokens):
    rows = n_tokens // GRID_W
    row = jnp.repeat(jnp.arange(rows, dtype=jnp.int32), GRID_W)
    col = jnp.tile(jnp.arange(GRID_W, dtype=jnp.int32), rows)
    n_freq = MLA_ROPE // 4
    inv_freq = ROPE_THETA ** (-jnp.arange(n_freq, dtype=F32) / n_freq)
    ang_r = row[:, None].astype(F32) * inv_freq
    ang_c = col[:, None].astype(F32) * inv_freq
    cos = jnp.concatenate([jnp.cos(ang_r)] * 2 + [jnp.cos(ang_c)] * 2, axis=-1)
    sin = jnp.concatenate([-jnp.sin(ang_r), jnp.sin(ang_r), -jnp.sin(ang_c), jnp.sin(ang_c)], axis=-1)
    return cos, sin


_SWAP64 = np.concatenate([np.arange(16, 32), np.arange(0, 16), np.arange(48, 64), np.arange(32, 48)])


def _position_tables(cos, sin, q_gain, k_gain):
    gq = q_gain[MLA_NOPE:]
    gk = k_gain[MLA_NOPE:]
    cq = jnp.tile(cos * gq, (1, MLA_HEADS))
    sq = jnp.tile(sin * gq[_SWAP64], (1, MLA_HEADS))
    tk = jnp.concatenate([cos * gk, sin * gk[_SWAP64]], axis=-1)
    return {"cq": cq, "sq": sq, "tk": tk}


def _layer_weights(l, p):
    d = D_MODEL
    o = _IN_OFF
    w_in = p["w_in"][l]
    seg = lambda i: w_in[:, o[i]:o[i + 1]]
    kpe = seg(2)
    w_in_r = jnp.concatenate(
        [seg(0), seg(1), seg(3), seg(4), seg(5) * (GLA_DK ** -0.5), seg(6), seg(7), seg(8),
         kpe, kpe[:, _SWAP64], seg(9), seg(10), jnp.zeros((d, Z_WIDTH - Z_GG - 2 * GLA_GATE_RANK), F32)],
        axis=-1).astype(BF16)
    w_uq = p["mla_w_uq"][l].reshape(MLA_Q_RANK, MLA_HEADS, MLA_QK)
    q_nope = w_uq[:, :, :MLA_NOPE].reshape(MLA_Q_RANK, -1)
    q_pe = w_uq[:, :, MLA_NOPE:]
    w_uq_r = jnp.concatenate([q_nope, q_pe.reshape(MLA_Q_RANK, -1), q_pe[:, :, _SWAP64].reshape(MLA_Q_RANK, -1)],
                             axis=-1).astype(BF16)
    w_ukv = p["mla_w_ukv"][l].reshape(MLA_KV_RANK, MLA_HEADS, MLA_NOPE + MLA_V)
    w_ukv_r = jnp.concatenate([w_ukv[:, :, :MLA_NOPE].reshape(MLA_KV_RANK, -1),
                               w_ukv[:, :, MLA_NOPE:].reshape(MLA_KV_RANK, -1)], axis=-1).astype(BF16)
    wg = jnp.zeros((LANES, 2 * GLA_QK_W), F32)
    wg = wg.at[0:GLA_GATE_RANK, 0:GLA_QK_W].set(p["gla_wg_f"][l])
    wg = wg.at[GLA_GATE_RANK:2 * GLA_GATE_RANK, GLA_QK_W:].set(p["gla_wg_b"][l])
    bg = jnp.concatenate([p["gla_bg_f"][l], p["gla_bg_b"][l]])[None, :]
    sgu_b = jnp.repeat(p["sgu_b"][l].T, SGU_GROUP_DIM, axis=1)
    wr = jnp.zeros((ROUTER_ROWS, d), F32)
    wr = wr.at[0:MOE_EXPERTS].set(p["moe_w_expert"][l].T)
    wr = wr.at[MOE_EXPERTS:MOE_EXPERTS + MOE_GROUPS].set(p["moe_w_group"][l].T)
    wr_hi = wr.astype(BF16)
    wr_lo = (wr - wr_hi.astype(F32)).astype(BF16)
    br = jnp.zeros((ROUTER_ROWS,), F32).at[0:MOE_EXPERTS].set(p["moe_b_expert"][l])
    br = br.at[MOE_EXPERTS:MOE_EXPERTS + MOE_GROUPS].set(p["moe_b_group"][l])
    return {
        "n1": p["norm1_g"][l][None, :], "n2": p["norm2_g"][l][None, :],
        "w_in": w_in_r, "q_norm": p["mla_q_norm"][l][None, :], "kv_norm": p["mla_kv_norm"][l][None, :],
        "w_uq": w_uq_r, "w_ukv": w_ukv_r,
        "gq_nope": p["mla_q_gain"][l][None, :MLA_NOPE], "gk_nope": p["mla_k_gain"][l][None, :MLA_NOPE],
        "sgu_norm": p["sgu_norm"][l].reshape(1, SGU_WIDTH), "sgu_w": p["sgu_w"][l].astype(BF16), "sgu_b": sgu_b,
        "wg": wg.astype(BF16), "bg": bg,
        "gla_out_norm": p["gla_out_norm"][l][None, :], "w_out": p["w_out"][l].astype(BF16),
        "wr_hi": wr_hi, "wr_lo": wr_lo, "br": jnp.broadcast_to(br[:, None], (ROUTER_ROWS, LANES)),
    }


def _router_constants():
    t = np.arange(T_LOC)
    lst = (t[:, None] < t[None, :]).astype(np.float32)
    e = np.arange(MOE_EXPERTS)
    sl = (e[None, :] < e[:, None]).astype(np.float32)
    return {"lst": jnp.asarray(lst, BF16), "sl": jnp.asarray(sl, BF16)}


def _dispatch_tables(cnt, r_max):
    n_tiles = cnt.shape[0]
    cnt = cnt.astype(jnp.int32)
    loc_off = jnp.cumsum(cnt, axis=1) - cnt
    rows_e = jnp.sum(cnt, axis=0)
    tiles_e = (rows_e + TM_FFN - 1) // TM_FFN
    tile_start = jnp.cumsum(tiles_e) - tiles_e
    base_e = tile_start * TM_FFN
    dst = base_e[None, :] + jnp.cumsum(cnt, axis=0) - cnt
    n_used = jnp.sum(tiles_e)
    n_mt = r_max // TM_FFN
    ti = jnp.arange(n_mt, dtype=jnp.int32)
    tile_blk = jnp.minimum(ti, n_used - 1)
    tile_e = jnp.sum(tile_blk[:, None] >= (tile_start + tiles_e)[None, :], axis=1).astype(jnp.int32)
    tile_e = jnp.minimum(tile_e, MOE_EXPERTS - 1)
    zdst = base_e + rows_e
    zcnt = tiles_e * TM_FFN - rows_e
    i32 = lambda a: a.astype(jnp.int32)
    sizes = jnp.asarray(_PIECE_SIZES, jnp.int32)
    has = (cnt[:, :, None] & sizes) != 0
    done = cnt[:, :, None] & ~(2 * sizes - 1)
    pos = jnp.cumsum(has, axis=1) - 1
    pick = has[..., None] & (pos[..., None] == jnp.arange(MOE_EXPERTS))
    listed = lambda a: jnp.sum(jnp.where(pick, (a[:, :, None] + done)[..., None], 0), axis=1)
    return {"ploc": i32(listed(loc_off).reshape(-1)), "pdst": i32(listed(dst).reshape(-1)),
            "pn": i32(jnp.sum(has, axis=1).reshape(-1)), "zdst": i32(zdst), "zcnt": i32(zcnt),
            "tile_e": tile_e, "tile_blk": i32(tile_blk), "n_used": i32(n_used.reshape(1))}


def _moe(h2s, metas, cnts, x1s, n_batch, mod_rows, mod_row0s, lw):
    meta = jnp.concatenate(metas, axis=0)
    cnt = jnp.concatenate(cnts, axis=0)
    n_tiles = meta.shape[0]
    worst_rows = n_tiles * SLOTS + MOE_EXPERTS * (TM_FFN - BF16_ROWS)
    r_max = -(-worst_rows // TM_FFN) * TM_FFN
    tabs = _dispatch_tables(cnt[:, :, 0], r_max)
    xs = _dispatch(h2s, meta, tabs, r_max)
    y = _ffn(xs, lw["w1"], lw["w3"], lw["w2"], tabs)
    metac = jnp.swapaxes(meta, 1, 2)
    return _combine(y, metac, x1s, n_batch, mod_rows, mod_row0s, tabs)


def kernel(x, c, ctx, c_ctx, norm1_g, norm2_g, ada_w, ada_b, w_in, mla_q_norm, mla_w_uq, mla_kv_norm, mla_w_ukv,
           mla_q_gain, mla_k_gain, sgu_norm, sgu_w, sgu_b, gla_wg_f, gla_bg_f, gla_wg_b, gla_bg_b, gla_out_norm,
           w_out, moe_w_group, moe_b_group, moe_w_expert, moe_b_expert, moe_w1, moe_w3, moe_w2):
    p = dict(norm1_g=norm1_g, norm2_g=norm2_g, w_in=w_in, mla_q_norm=mla_q_norm, mla_w_uq=mla_w_uq,
             mla_kv_norm=mla_kv_norm, mla_w_ukv=mla_w_ukv, mla_q_gain=mla_q_gain, mla_k_gain=mla_k_gain,
             sgu_norm=sgu_norm, sgu_w=sgu_w, sgu_b=sgu_b, gla_wg_f=gla_wg_f, gla_bg_f=gla_bg_f,
             gla_wg_b=gla_wg_b, gla_bg_b=gla_bg_b, gla_out_norm=gla_out_norm, w_out=w_out,
             moe_w_group=moe_w_group, moe_b_group=moe_b_group, moe_w_expert=moe_w_expert,
             moe_b_expert=moe_b_expert, moe_w1=moe_w1, moe_w3=moe_w3, moe_w2=moe_w2)
    n_batch, seq, d = x.shape
    n_ctx = ctx.shape[1]
    depth = ada_w.shape[0]
    assert d == D_MODEL and n_batch <= CTX_MOD_ROW and seq % T_LOC == 0 and seq % TM_IN == 0 and n_ctx % TM_SUB == 0
    assert seq % (GLA_STEP_CHUNKS * GLA_CHUNK) == 0 and n_ctx % (GLA_STEP_CHUNKS * GLA_CHUNK) == 0
    assert (n_batch * n_ctx) % T_LOC == 0

    c_rows = jnp.zeros((MOD_ROWS, d), F32).at[0:n_batch].set(c).at[CTX_MOD_ROW].set(c_ctx)
    mod = _ada(c_rows, ada_w, ada_b)

    cos, sin = _rope_tables(seq)
    ones_c = jnp.ones((min(TM_IN, n_ctx), MLA_ROPE), F32)
    gla_consts_np = _gla_constants()
    gla_consts = (jnp.asarray(gla_consts_np[0], BF16), jnp.asarray(gla_consts_np[1], F32))
    rconst = _router_constants()
    zero_state = jnp.zeros((2, n_batch, GLA_DV, GLA_QK_W), F32)

    x2 = x.reshape(n_batch * seq, d)
    ctx2 = ctx.reshape(n_batch * n_ctx, d)
    for l in range(depth):
        ctx_out = l < depth - 1
        lw = _layer_weights(l, p)
        mod_rows = mod[l].reshape(MOD_ROWS, 1, 6 * d)
        tabs_x = _position_tables(cos, sin, mla_q_gain[l], mla_k_gain[l])
        tabs_c = _position_tables(ones_c, 0.0 * ones_c, mla_q_gain[l], mla_k_gain[l])

        q_x, k_x, v_x, b_x, gla_x, r_x = _inproj(x2, n_batch, mod_rows, 0, lw, tabs_x)
        q_c, k_c, v_c, b_c, gla_c, r_c = _inproj(ctx2, n_batch, mod_rows, CTX_MOD_ROW, lw, tabs_c)

        of_c, ob_c, s_ctx = _gla(gla_c.reshape(n_batch, n_ctx, -1), zero_state, gla_consts)
        of_x, ob_x, _ = _gla(gla_x.reshape(n_batch, seq, -1), s_ctx, gla_consts)
        a_x, lw["w1"], lw["w3"], lw["w2"] = _attention(q_x, [k_x, k_c], [v_x, v_c],
                                                       casts=(moe_w1, moe_w3, moe_w2), cast_layer=l)

        x1, h2, meta, cnt = _outproj(a_x.reshape(n_batch * seq, -1), b_x, of_x.reshape(n_batch * seq, -1),
                                     ob_x.reshape(n_batch * seq, -1), r_x, x2, n_batch, mod_rows, 0, lw, rconst)
        if ctx_out:
            a_c, = _attention(q_c, [k_c], [v_c])
            c1, h2c, meta_c, cnt_c = _outproj(a_c.reshape(n_batch * n_ctx, -1), b_c,
                                              of_c.reshape(n_batch * n_ctx, -1), ob_c.reshape(n_batch * n_ctx, -1),
                                              r_c, ctx2, n_batch, mod_rows, CTX_MOD_ROW, lw, rconst)
            x2, ctx2 = _moe([h2, h2c], [meta, meta_c], [cnt, cnt_c], [x1, c1], n_batch, mod_rows, [0, CTX_MOD_ROW], lw)
        else:
            x2, = _moe([h2], [meta], [cnt], [x1], n_batch, mod_rows, [0], lw)
    return x2.reshape(n_batch, seq, d)
```

```python
import functools

import numpy as np
import jax
import jax.numpy as jnp
from jax import lax
from jax.experimental import pallas as pl
from jax.experimental.pallas import tpu as pltpu

F32 = jnp.float32
BF16 = jnp.bfloat16

D_MODEL = 2048
GRID_W = 64
EPS = 1e-6

MLA_HEADS = 8
MLA_Q_RANK = 512
MLA_KV_RANK = 512
MLA_NOPE = 128
MLA_ROPE = 64
MLA_V = 128
MLA_V_AUG = 2 * MLA_V
MLA_QK = MLA_NOPE + MLA_ROPE
MLA_SCALE = MLA_QK ** -0.5
ROPE_THETA = 10000.0
LOG2_E = float(np.log2(np.e))

SGU_WIDTH = 512
SGU_GROUPS = 4
SGU_GROUP_DIM = SGU_WIDTH // SGU_GROUPS
SGU_CHUNK = 128

GLA_HEADS = 4
GLA_DK = 64
GLA_DV = 128
GLA_GATE_RANK = 16
GLA_GATE_NORM = 16.0
GLA_CHUNK = 64
GLA_STEP_CHUNKS = 4
GLA_QK_W = GLA_HEADS * GLA_DK
GLA_V_W = GLA_HEADS * GLA_DV

MOE_GROUPS = 4
MOE_PER_GROUP = 4
MOE_EXPERTS = MOE_GROUPS * MOE_PER_GROUP
D_EXPERT = 1024

_IN_SPLITS = (512, 512, 64, 512, 512, 256, 256, 512, 512, 16, 16)
_IN_OFF = np.concatenate([[0], np.cumsum(_IN_SPLITS)]).tolist()

Z_CQ, Z_CKV, Z_ZU, Z_ZV = 0, 512, 1024, 1536
Z_GQ, Z_GK, Z_GV, Z_GR = 2048, 2304, 2560, 3072
Z_KPE, Z_GG = 3584, 3712
Z_WIDTH = 3840

MLA_Q_COLS = MLA_HEADS * (MLA_NOPE + 2 * MLA_ROPE)
MLA_KV_COLS = MLA_HEADS * (MLA_NOPE + MLA_V)

LANES = 128
SUBLANES = 8
BF16_ROWS = 16
VMEM_LIMIT = 56 * 1024 * 1024
VMEM_LIMIT_INPROJ = 61 * 1024 * 1024

MOD_ROWS = SUBLANES
CTX_MOD_ROW = 4
ROUTER_ROWS = 2 * BF16_ROWS
META_ROWS = SUBLANES

ADA_TN = 1536
TM_IN = 512
TM_SUB = 256
TQ_ATTN = 1024
TK_ATTN = 2048
T_LOC = 512
SLOTS = 2 * T_LOC + MOE_EXPERTS * BF16_ROWS
TM_FFN = 512
F_CHUNK = 256
_PIECE_SIZES = (512, 256, 128, 64, 32, 16)


def _cparams(sem, vmem=VMEM_LIMIT):
    return pltpu.CompilerParams(dimension_semantics=sem, vmem_limit_bytes=vmem)


def _silu(x):
    return x / (1.0 + jnp.exp(-x))


def _gelu_tanh(x):
    return 0.5 * x * (1.0 + jnp.tanh(np.sqrt(2.0 / np.pi).astype(np.float32) * (x + 0.044715 * (x * x * x))))


def _log_sigmoid(x):
    return jnp.minimum(x, 0.0) - jnp.log(1.0 + jnp.exp(-jnp.abs(x)))


def _dot(a, b):
    return jnp.dot(a, b, preferred_element_type=F32)


def _dot_nt(a, b):
    return lax.dot_general(a, b, (((1,), (1,)), ((), ())), preferred_element_type=F32)


def _dot_tn(a, b):
    return lax.dot_general(a, b, (((0,), (0,)), ((), ())), preferred_element_type=F32)


def _ada_kernel(c_ref, w_ref, b_ref, o_ref):
    a = _silu(c_ref[...]).astype(BF16)
    o_ref[0] = _dot(a, w_ref[0].astype(BF16)) + b_ref[0]


def _ada(c_rows, ada_w, ada_b):
    depth, d, n = ada_w.shape
    tn = ADA_TN
    return pl.pallas_call(
        _ada_kernel,
        grid=(depth, n // tn),
        in_specs=[
            pl.BlockSpec((MOD_ROWS, d), lambda l, j: (0, 0)),
            pl.BlockSpec((1, d, tn), lambda l, j: (l, 0, j)),
            pl.BlockSpec((1, 1, tn), lambda l, j: (l, 0, j)),
        ],
        out_specs=pl.BlockSpec((1, MOD_ROWS, tn), lambda l, j: (l, 0, j)),
        out_shape=jax.ShapeDtypeStruct((depth, MOD_ROWS, n), F32),
        compiler_params=_cparams(("arbitrary", "arbitrary")),
        name="ada",
    )(c_rows, ada_w, ada_b.reshape(depth, 1, n))


def _inproj_kernel(x_ref, mod_ref, n1_ref, win_ref, qn_ref, wuq_ref, kvn_ref, wukv_ref,
                   gqn_ref, gkn_ref, cq_ref, sq_ref, tk_ref,
                   sgn_ref, sgw_ref, sgb_ref, wg_ref, bg_ref,
                   q_out, k_out, v_out, b_out, gla_out, r_out):
    for t in range(x_ref.shape[0] // TM_SUB):
        _inproj_rows(slice(t * TM_SUB, (t + 1) * TM_SUB), x_ref, mod_ref, n1_ref, win_ref, qn_ref, wuq_ref,
                     kvn_ref, wukv_ref, gqn_ref, gkn_ref, cq_ref, sq_ref, tk_ref, sgn_ref, sgw_ref, sgb_ref,
                     wg_ref, bg_ref, q_out, k_out, v_out, b_out, gla_out, r_out)


def _inproj_rows(rs, x_ref, mod_ref, n1_ref, win_ref, qn_ref, wuq_ref, kvn_ref, wukv_ref,
                 gqn_ref, gkn_ref, cq_ref, sq_ref, tk_ref, sgn_ref, sgw_ref, sgb_ref, wg_ref, bg_ref,
                 q_out, k_out, v_out, b_out, gla_out, r_out):
    d = D_MODEL
    tm = rs.stop - rs.start
    x = x_ref[rs, :]
    shift = mod_ref[0, :, 0:d]
    scale = mod_ref[0, :, d:2 * d]
    ms = jnp.mean(x * x, axis=-1, keepdims=True)
    h = (x * lax.rsqrt(ms + EPS) * n1_ref[...]) * (1.0 + scale) + shift
    hb = h.astype(BF16)
    z_mla = _dot(hb, win_ref[:, 0:Z_ZU])
    z_end = _dot(hb, win_ref[:, Z_KPE:Z_WIDTH])
    z_mid = _dot(hb, win_ref[:, Z_ZU:Z_KPE])

    cq = z_mla[:, Z_CQ:Z_CQ + MLA_Q_RANK]
    ckv = z_mla[:, Z_CKV:Z_CKV + MLA_KV_RANK]
    cqn = cq * lax.rsqrt(jnp.mean(cq * cq, axis=-1, keepdims=True) + EPS) * qn_ref[...]
    ckvn = ckv * lax.rsqrt(jnp.mean(ckv * ckv, axis=-1, keepdims=True) + EPS) * kvn_ref[...]
    qa = _dot(cqn.astype(BF16), wuq_ref[...])
    kv = _dot(ckvn.astype(BF16), wukv_ref[...])
    npe = MLA_HEADS * MLA_NOPE
    rw = MLA_HEADS * MLA_ROPE
    q_pe = qa[:, npe:npe + rw]
    q_pe_sq = q_pe * q_pe
    lane_pair = lax.broadcasted_iota(jnp.int32, (1, LANES), 1)
    cq = jnp.concatenate([cq_ref[rs, :]] * (rw // LANES), axis=-1)
    sq = jnp.concatenate([sq_ref[rs, :]] * (rw // LANES), axis=-1)
    q_rot = q_pe * cq + qa[:, npe + rw:npe + 2 * rw] * sq
    zk = z_end[:, 0:2 * MLA_ROPE]
    tkz = zk * tk_ref[rs, :]
    k_rot = tkz[:, 0:MLA_ROPE] + tkz[:, MLA_ROPE:2 * MLA_ROPE]
    k_pe = zk[:, 0:MLA_ROPE]
    ss_kpe = jnp.sum(k_pe * k_pe, axis=-1, keepdims=True)
    gqn = gqn_ref[...]
    gkn = gkn_ref[...]
    for hh in range(MLA_HEADS):
        qn_h = qa[:, hh * MLA_NOPE:(hh + 1) * MLA_NOPE]
        pair = q_pe_sq[:, (hh // 2) * LANES:(hh // 2 + 1) * LANES]
        own = jnp.where((lane_pair < MLA_ROPE) == (hh % 2 == 0), pair, 0.0)
        ssq = jnp.sum(qn_h * qn_h + own, axis=-1, keepdims=True)
        rq = lax.rsqrt(ssq * (1.0 / MLA_QK) + EPS) * (MLA_SCALE * LOG2_E)
        q_h = jnp.concatenate([qn_h * gqn * rq, q_rot[:, hh * MLA_ROPE:(hh + 1) * MLA_ROPE] * rq], axis=-1)
        q_out[0, hh, rs, :] = q_h.astype(BF16)
        kn_h = kv[:, hh * MLA_NOPE:(hh + 1) * MLA_NOPE]
        ssk = jnp.sum(kn_h * kn_h, axis=-1, keepdims=True) + ss_kpe
        rk = lax.rsqrt(ssk * (1.0 / MLA_QK) + EPS)
        k_h = jnp.concatenate([kn_h * gkn * rk, k_rot * rk], axis=-1)
        k_out[0, hh, rs, :] = k_h.astype(BF16)
        v_h = kv[:, npe + hh * MLA_V:npe + (hh + 1) * MLA_V]
        v_out[0, hh, rs, :] = jnp.concatenate([v_h, jnp.ones_like(v_h)], axis=-1).astype(BF16)

    u = _gelu_tanh(z_mid[:, 0:SGU_WIDTH])
    vv = _gelu_tanh(z_mid[:, Z_ZV - Z_ZU:Z_ZV - Z_ZU + SGU_WIDTH])
    sgn = sgn_ref[...]
    for c in range(tm // SGU_CHUNK):
        rows = slice(c * SGU_CHUNK, (c + 1) * SGU_CHUNK)
        outs = []
        for g in range(SGU_GROUPS):
            cols = slice(g * SGU_GROUP_DIM, (g + 1) * SGU_GROUP_DIM)
            seg = vv[rows, cols]
            vn = seg * lax.rsqrt(jnp.mean(seg * seg, axis=-1, keepdims=True) + EPS) * sgn[:, cols]
            outs.append(_dot(sgw_ref[g], vn.astype(BF16)))
        sp = jnp.concatenate(outs, axis=-1) + sgb_ref[...]
        b_out[rs.start + c * SGU_CHUNK:rs.start + (c + 1) * SGU_CHUNK, :] = (u[rows, :] * sp).astype(BF16)

    gl = _dot(z_end[:, Z_GG - Z_KPE:Z_GG - Z_KPE + LANES].astype(BF16), wg_ref[...]) + bg_ref[...]
    gdec = _log_sigmoid(gl) * (1.0 / GLA_GATE_NORM)
    gla_out[rs, 0:2 * GLA_QK_W] = z_mid[:, Z_GQ - Z_ZU:Z_GQ - Z_ZU + 2 * GLA_QK_W]
    gla_out[rs, 2 * GLA_QK_W:4 * GLA_QK_W] = gdec
    gla_out[rs, 4 * GLA_QK_W:4 * GLA_QK_W + GLA_V_W] = z_mid[:, Z_GV - Z_ZU:Z_GV - Z_ZU + GLA_V_W]
    r_out[rs, :] = z_mid[:, Z_GR - Z_ZU:Z_GR - Z_ZU + GLA_V_W]


def _inproj(x2, n_batch, mod_rows, mod_row0, lw, tabs):
    n_total = x2.shape[0]
    n_tok = n_total // n_batch
    tm = min(TM_IN, n_tok)
    tpb = n_tok // tm
    tab_blocks = tabs["cq"].shape[0] // tm
    if mod_row0 == 0:
        mod_map = lambda i: (i // tpb, 0, 0)
    else:
        mod_map = lambda i: (mod_row0, 0, 0)
    const2 = lambda i: (0, 0)
    tab_map = lambda i: (i % tab_blocks, 0)
    tok_map = lambda i: (i, 0)
    head_map = lambda i: (i // tpb, 0, i % tpb, 0)
    in_specs = [
        pl.BlockSpec((tm, D_MODEL), tok_map),
        pl.BlockSpec((1, 1, 6 * D_MODEL), mod_map),
        pl.BlockSpec((1, D_MODEL), const2),
        pl.BlockSpec((D_MODEL, Z_WIDTH), const2, pipeline_mode=pl.Buffered(1)),
        pl.BlockSpec((1, MLA_Q_RANK), const2),
        pl.BlockSpec((MLA_Q_RANK, MLA_Q_COLS), const2, pipeline_mode=pl.Buffered(1)),
        pl.BlockSpec((1, MLA_KV_RANK), const2),
        pl.BlockSpec((MLA_KV_RANK, MLA_KV_COLS), const2, pipeline_mode=pl.Buffered(1)),
        pl.BlockSpec((1, MLA_NOPE), const2),
        pl.BlockSpec((1, MLA_NOPE), const2),
        pl.BlockSpec((tm, LANES), tab_map),
        pl.BlockSpec((tm, LANES), tab_map),
        pl.BlockSpec((tm, 2 * MLA_ROPE), tab_map),
        pl.BlockSpec((1, SGU_WIDTH), const2),
        pl.BlockSpec((SGU_GROUPS, SGU_CHUNK, SGU_CHUNK), lambda i: (0, 0, 0)),
        pl.BlockSpec((SGU_CHUNK, SGU_WIDTH), const2),
        pl.BlockSpec((LANES, 2 * GLA_QK_W), const2),
        pl.BlockSpec((1, 2 * GLA_QK_W), const2),
    ]
    out_shape = [
        jax.ShapeDtypeStruct((n_batch, MLA_HEADS, n_tok, MLA_QK), BF16),
        jax.ShapeDtypeStruct((n_batch, MLA_HEADS, n_tok, MLA_QK), BF16),
        jax.ShapeDtypeStruct((n_batch, MLA_HEADS, n_tok, MLA_V_AUG), BF16),
        jax.ShapeDtypeStruct((n_total, SGU_WIDTH), BF16),
        jax.ShapeDtypeStruct((n_total, 4 * GLA_QK_W + GLA_V_W), F32),
        jax.ShapeDtypeStruct((n_total, GLA_V_W), F32),
    ]
    out_specs = [
        pl.BlockSpec((1, MLA_HEADS, tm, MLA_QK), head_map),
        pl.BlockSpec((1, MLA_HEADS, tm, MLA_QK), head_map),
        pl.BlockSpec((1, MLA_HEADS, tm, MLA_V_AUG), head_map),
        pl.BlockSpec((tm, SGU_WIDTH), tok_map),
        pl.BlockSpec((tm, 4 * GLA_QK_W + GLA_V_W), tok_map),
        pl.BlockSpec((tm, GLA_V_W), tok_map),
    ]
    return pl.pallas_call(
        _inproj_kernel,
        grid=(n_total // tm,),
        in_specs=in_specs,
        out_specs=out_specs,
        out_shape=out_shape,
        compiler_params=_cparams(("arbitrary",), VMEM_LIMIT_INPROJ),
        name="inproj",
    )(x2, mod_rows, lw["n1"], lw["w_in"], lw["q_norm"], lw["w_uq"], lw["kv_norm"], lw["w_ukv"],
      lw["gq_nope"], lw["gk_nope"], tabs["cq"], tabs["sq"], tabs["tk"],
      lw["sgu_norm"], lw["sgu_w"], lw["sgu_b"], lw["wg"], lw["bg"])


_GLA_LEVELS = (32, 16, 8, 4, 2, 1)
_N_LEV = len(_GLA_LEVELS)
_G_SPLIT = 3
_R_QI = _N_LEV * GLA_CHUNK
_R_KS = _R_QI + GLA_CHUNK
_R_TOT = _R_KS + GLA_CHUNK
_M_ROWS = _R_TOT + SUBLANES


def _gla_constants():
    c = GLA_CHUNK
    t = np.arange(c)[:, None]
    j = np.arange(c)[None, :]
    lev, masks = [], []
    for m in _GLA_LEVELS:
        upper = (t // m) % 2 == 1
        bound = (t // (2 * m)) * 2 * m + m - 1
        lev.append(np.where(upper, (j > bound) & (j <= t), (j > t) & (j <= bound)).astype(np.float32))
        s = j
        masks.append(((t // (2 * m) == s // (2 * m)) & upper & ((s // m) % 2 == 0)).astype(np.float32))
    masks.append((t == j).astype(np.float32))
    mqi = (j <= t).astype(np.float32)
    mks = (j > t).astype(np.float32)
    tot = np.ones((SUBLANES, c), np.float32)
    flip = lambda a: a[::-1, ::-1]
    m_f = np.concatenate(lev + [mqi, mks, tot], axis=0)
    m_b = np.concatenate([flip(a) for a in lev] + [flip(mqi), flip(mks), tot], axis=0)
    mask_f = np.stack([np.tile(a, (GLA_HEADS, 1)) for a in masks])
    mask_b = np.stack([np.tile(flip(a), (GLA_HEADS, 1)) for a in masks])
    m_all = np.tile(np.stack([m_f, m_b]), (1, 1, _G_SPLIT))
    return m_all, np.stack([mask_f, mask_b])


def _stack_heads(a, lane_head):
    return jnp.concatenate([jnp.where(lane_head == hh, a, jnp.zeros_like(a)) for hh in range(GLA_HEADS)], axis=0)


def _gla_chunk(q, k, g, v, st, mall, masks, lane_head):
    c = GLA_CHUNK
    g_hi = g.astype(BF16)
    r1 = g - g_hi.astype(F32)
    g_mid = r1.astype(BF16)
    g_lo = (r1 - g_mid.astype(F32)).astype(BF16)
    e = _dot(mall, jnp.concatenate([g_hi, g_mid, g_lo], axis=0))
    a = masks[_N_LEV] * _dot_nt(_stack_heads(q.astype(BF16), lane_head), k.astype(BF16))
    for lev in range(_N_LEV):
        ex = jnp.exp(e[lev * c:(lev + 1) * c])
        a = a + masks[lev] * _dot_nt(_stack_heads((q * ex).astype(BF16), lane_head), (k * ex).astype(BF16))
    qi = (q * jnp.exp(e[_R_QI:_R_QI + c])).astype(BF16)
    o_inter = _dot_nt(_stack_heads(qi, lane_head), st.astype(BF16))
    vb = v.astype(BF16)
    a16 = a.astype(BF16)
    outs = []
    for hh in range(GLA_HEADS):
        rows = slice(hh * c, (hh + 1) * c)
        outs.append(o_inter[rows] + _dot(a16[rows], vb[:, hh * GLA_DV:(hh + 1) * GLA_DV]))
    o = jnp.concatenate(outs, axis=-1)
    kd = (k * jnp.exp(e[_R_KS:_R_KS + c])).astype(BF16)
    upd = _dot_tn(vb, kd)
    st_new = st * jnp.exp(e[_R_TOT:_R_TOT + 1])
    for hh in range(GLA_HEADS):
        st_new = st_new + jnp.where(lane_head == hh, upd[hh * GLA_DV:(hh + 1) * GLA_DV], 0.0)
    return o, st_new


def _gla_kernel(fwd_ref, bwd_ref, s0_ref, mall_ref, mask_ref, of_ref, ob_ref, sfin_ref, st_ref):
    step = pl.program_id(0)
    n_batch = fwd_ref.shape[0]

    @pl.when(step == 0)
    def _():
        st_ref[...] = s0_ref[...]

    lane_head = lax.broadcasted_iota(jnp.int32, (1, GLA_QK_W), 1) // GLA_DK
    w = GLA_QK_W
    n_sub = fwd_ref.shape[1] // GLA_CHUNK
    for b in range(n_batch):
        for direction, (in_ref, o_ref) in enumerate(((fwd_ref, of_ref), (bwd_ref, ob_ref))):
            mall = mall_ref[direction]
            masks = [mask_ref[direction, i] for i in range(_N_LEV + 1)]
            st = st_ref[direction, b]
            for sub in range(n_sub):
                c0 = (sub if direction == 0 else n_sub - 1 - sub) * GLA_CHUNK
                rows = slice(c0, c0 + GLA_CHUNK)
                q = in_ref[b, rows, 0:w]
                k = in_ref[b, rows, w:2 * w]
                g = in_ref[b, rows, (2 + direction) * w:(3 + direction) * w]
                v = in_ref[b, rows, 4 * w:4 * w + GLA_V_W]
                o, st = _gla_chunk(q, k, g, v, st, mall, masks, lane_head)
                o_ref[b, rows, :] = o
            st_ref[direction, b] = st

    @pl.when(step == pl.num_programs(0) - 1)
    def _():
        sfin_ref[...] = st_ref[...]


def _gla(gla_in, s0, consts):
    n_batch, n_tok, width = gla_in.shape
    blk = GLA_STEP_CHUNKS * GLA_CHUNK
    nc = n_tok // blk
    mall, masks = consts
    st_shape = (2, n_batch, GLA_DV, GLA_QK_W)
    return pl.pallas_call(
        _gla_kernel,
        grid=(nc,),
        in_specs=[
            pl.BlockSpec((n_batch, blk, width), lambda c: (0, c, 0)),
            pl.BlockSpec((n_batch, blk, width), lambda c: (0, nc - 1 - c, 0)),
            pl.BlockSpec(st_shape, lambda c: (0, 0, 0, 0)),
            pl.BlockSpec(mall.shape, lambda c: (0, 0, 0)),
            pl.BlockSpec(masks.shape, lambda c: (0, 0, 0, 0)),
        ],
        out_specs=[
            pl.BlockSpec((n_batch, blk, GLA_V_W), lambda c: (0, c, 0)),
            pl.BlockSpec((n_batch, blk, GLA_V_W), lambda c: (0, nc - 1 - c, 0)),
            pl.BlockSpec(st_shape, lambda c: (0, 0, 0, 0)),
        ],
        out_shape=[
            jax.ShapeDtypeStruct((n_batch, n_tok, GLA_V_W), F32),
            jax.ShapeDtypeStruct((n_batch, n_tok, GLA_V_W), F32),
            jax.ShapeDtypeStruct(st_shape, F32),
        ],
        scratch_shapes=[pltpu.VMEM(st_shape, F32)],
        compiler_params=_cparams(("arbitrary",)),
        name="gla",
    )(gla_in, gla_in, s0, mall, masks)


def _attn_kernel(*refs, n_seg, n_cast):
    q_ref = refs[0]
    k_refs = refs[1:1 + n_seg]
    v_refs = refs[1 + n_seg:1 + 2 * n_seg]
    cast_in = refs[1 + 2 * n_seg:1 + 2 * n_seg + n_cast]
    o_ref = refs[1 + 2 * n_seg + n_cast]
    cast_out = refs[2 + 2 * n_seg + n_cast:]
    for src, dst in zip(cast_in, cast_out):
        dst[...] = src[...].astype(BF16)
    q = q_ref[0, 0]
    n_main = k_refs[0].shape[2]
    tk = min(TK_ATTN, n_main)
    groups = [[(0, slice(c * tk, (c + 1) * tk))] for c in range(n_main // tk)]
    groups[0] += [(i, slice(0, k_refs[i].shape[2])) for i in range(1, n_seg)]
    m = acc = None
    for group in groups:
        scores = [_dot_nt(q, k_refs[i][0, 0, keys, :]) for i, keys in group]
        m_g = functools.reduce(jnp.maximum, [jnp.max(s, axis=-1, keepdims=True) for s in scores])
        m_new = m_g if m is None else jnp.maximum(m, m_g)
        acc_g = functools.reduce(jnp.add, [_dot(jnp.exp2(s - m_new).astype(BF16), v_refs[i][0, 0, keys, :])
                                           for s, (i, keys) in zip(scores, group)])
        acc = acc_g if m is None else jnp.exp2(m - m_new) * acc + acc_g
        m = m_new
    o_ref[0] = (acc[:, 0:MLA_V] / acc[:, MLA_V:MLA_V_AUG]).astype(BF16)


def _attention(q, ks, vs, casts=(), cast_layer=0):
    n_batch, n_heads, nq, _ = q.shape
    tq = min(TQ_ATTN, nq)
    n_seg = len(ks)
    n_q = nq // tq
    n_steps = n_batch * n_heads * n_q
    in_specs = [pl.BlockSpec((1, 1, tq, MLA_QK), lambda b, h, i: (b, h, i, 0))]
    for k in ks:
        in_specs.append(pl.BlockSpec((1, 1, k.shape[2], MLA_QK), lambda b, h, i: (b, h, 0, 0)))
    for v in vs:
        in_specs.append(pl.BlockSpec((1, 1, v.shape[2], MLA_V_AUG), lambda b, h, i: (b, h, 0, 0)))
    out_specs = [pl.BlockSpec((1, tq, MLA_V), lambda b, h, i: (b, i, h))]
    out_shape = [jax.ShapeDtypeStruct((n_batch, nq, n_heads * MLA_V), BF16)]
    slabs = []
    for w in casts:
        depth, cols = w.shape[0], w.shape[-1]
        rows = w.size // (depth * cols * n_steps)
        assert rows * depth * cols * n_steps == w.size and rows % BF16_ROWS == 0
        slabs.append(w.reshape(depth * n_steps, rows, cols))
        in_specs.append(pl.BlockSpec(
            (1, rows, cols), lambda b, h, i: (cast_layer * n_steps + (b * n_heads + h) * n_q + i, 0, 0)))
        out_specs.append(pl.BlockSpec((1, rows, cols), lambda b, h, i: ((b * n_heads + h) * n_q + i, 0, 0)))
        out_shape.append(jax.ShapeDtypeStruct((n_steps, rows, cols), BF16))
    outs = pl.pallas_call(
        functools.partial(_attn_kernel, n_seg=n_seg, n_cast=len(casts)),
        grid=(n_batch, n_heads, n_q),
        in_specs=in_specs,
        out_specs=out_specs,
        out_shape=out_shape,
        compiler_params=_cparams(("arbitrary", "arbitrary", "arbitrary")),
        name="attn",
    )(q, *ks, *vs, *slabs)
    return [outs[0]] + [o.reshape(w.shape[1:]) for o, w in zip(outs[1:], casts)]


def _outproj_kernel(a_ref, b_ref, of_ref, ob_ref, r_ref, x_ref, mod_ref, gon_ref, n2_ref, wout_ref,
                    wr_hi_ref, wr_lo_ref, br_ref, lst_ref, sl_ref,
                    x1_out, h2_out, meta_out, cnt_out):
    d = D_MODEL
    tm = x_ref.shape[0]
    gate1 = mod_ref[0, :, 2 * d:3 * d]
    shift2 = mod_ref[0, :, 3 * d:4 * d]
    scale2 = mod_ref[0, :, 4 * d:5 * d]

    og = of_ref[...] + ob_ref[...]
    gon = gon_ref[...]
    parts = []
    for hh in range(GLA_HEADS):
        seg = og[:, hh * GLA_DV:(hh + 1) * GLA_DV]
        parts.append(seg * lax.rsqrt(jnp.mean(seg * seg, axis=-1, keepdims=True) + EPS) * gon)
    cg = jnp.concatenate(parts, axis=-1) * _silu(r_ref[...])
    na = MLA_HEADS * MLA_V
    y = _dot(a_ref[...], wout_ref[0:na, :])
    y = y + _dot(b_ref[...], wout_ref[na:na + SGU_WIDTH, :])
    y = y + _dot(cg.astype(BF16), wout_ref[na + SGU_WIDTH:na + SGU_WIDTH + GLA_V_W, :])
    x1 = x_ref[...] + gate1 * y
    x1_out[...] = x1

    ms = jnp.mean(x1 * x1, axis=-1, keepdims=True)
    h2 = (x1 * lax.rsqrt(ms + EPS) * n2_ref[...]) * (1.0 + scale2) + shift2
    h_hi = h2.astype(BF16)
    h2_out[...] = h_hi
    h_lo = (h2 - h_hi.astype(F32)).astype(BF16)

    wr_hi = wr_hi_ref[...]
    lt2 = _dot_nt(jnp.concatenate([wr_hi, wr_lo_ref[...]], axis=0), h_hi)
    lt = lt2[0:ROUTER_ROWS] + lt2[ROUTER_ROWS:2 * ROUTER_ROWS] + _dot_nt(wr_hi, h_lo) + br_ref[:, 0:1]
    grp = [lt[MOE_EXPERTS + g:MOE_EXPERTS + g + 1, :] for g in range(MOE_GROUPS)]
    gmax = functools.reduce(jnp.maximum, grp)
    gsel = jnp.full(gmax.shape, MOE_GROUPS - 1, jnp.int32)
    for g in range(MOE_GROUPS - 2, -1, -1):
        gsel = jnp.where(grp[g] == gmax, g, gsel)
    gsum = functools.reduce(jnp.add, [jnp.exp(gl - gmax) for gl in grp])
    g_w = 1.0 / gsum
    e_in = []
    for j in range(MOE_PER_GROUP):
        ej = lt[j:j + 1, :]
        for g in range(1, MOE_GROUPS):
            ej = jnp.where(gsel == g, lt[g * MOE_PER_GROUP + j:g * MOE_PER_GROUP + j + 1, :], ej)
        e_in.append(ej)
    m1 = functools.reduce(jnp.maximum, e_in)
    i1 = jnp.full(m1.shape, MOE_PER_GROUP - 1, jnp.int32)
    for j in range(MOE_PER_GROUP - 2, -1, -1):
        i1 = jnp.where(e_in[j] == m1, j, i1)
    e_rest = [jnp.where(i1 == j, -jnp.inf, e_in[j]) for j in range(MOE_PER_GROUP)]
    m2 = functools.reduce(jnp.maximum, e_rest)
    i2 = jnp.full(m2.shape, MOE_PER_GROUP - 1, jnp.int32)
    for j in range(MOE_PER_GROUP - 2, -1, -1):
        i2 = jnp.where(e_rest[j] == m2, j, i2)
    p2 = jnp.exp(m2 - m1)
    w1 = g_w / (1.0 + p2)
    w2 = g_w * p2 / (1.0 + p2)
    eid1 = gsel * MOE_PER_GROUP + i1
    eid2 = gsel * MOE_PER_GROUP + i2

    e_iota = lax.broadcasted_iota(jnp.int32, (MOE_EXPERTS, tm), 0)
    hit1 = e_iota == eid1
    hit2 = e_iota == eid2
    onehot = jnp.where(hit1, 1.0, 0.0) + jnp.where(hit2, 1.0, 0.0)
    rank = _dot(onehot.astype(BF16), lst_ref[...])
    counts = jnp.sum(onehot, axis=1, keepdims=True)
    padded = jnp.ceil(counts * (1.0 / BF16_ROWS)) * BF16_ROWS
    padded_l = jnp.broadcast_to(padded, (MOE_EXPERTS, LANES))
    off = _dot(sl_ref[...], padded_l.astype(BF16))
    pos_e = off[:, 0:1] + rank
    pos1 = jnp.sum(jnp.where(hit1, pos_e, 0.0), axis=0, keepdims=True)
    pos2 = jnp.sum(jnp.where(hit2, pos_e, 0.0), axis=0, keepdims=True)
    zero = jnp.zeros_like(pos1)
    meta_out[0] = jnp.concatenate(
        [pos1, pos2, w1, w2, eid1.astype(F32), eid2.astype(F32), zero, zero], axis=0)
    cnt_out[0] = padded_l


def _outproj(a, b, o_f, o_b, r, x2, n_batch, mod_rows, mod_row0, lw, rconst):
    n_total = x2.shape[0]
    n_tok = n_total // n_batch
    tm = T_LOC
    tpb = n_tok // tm
    if mod_row0 == 0:
        mod_map = lambda i: (i // tpb, 0, 0)
    else:
        mod_map = lambda i: (mod_row0, 0, 0)
    const2 = lambda i: (0, 0)
    tok_map = lambda i: (i, 0)
    n_tiles = n_total // tm
    in_specs = [
        pl.BlockSpec((tm, MLA_HEADS * MLA_V), tok_map),
        pl.BlockSpec((tm, SGU_WIDTH), tok_map),
        pl.BlockSpec((tm, GLA_V_W), tok_map),
        pl.BlockSpec((tm, GLA_V_W), tok_map),
        pl.BlockSpec((tm, GLA_V_W), tok_map),
        pl.BlockSpec((tm, D_MODEL), tok_map),
        pl.BlockSpec((1, 1, 6 * D_MODEL), mod_map),
        pl.BlockSpec((1, GLA_DV), const2),
        pl.BlockSpec((1, D_MODEL), const2),
        pl.BlockSpec((D_MODEL, D_MODEL), const2, pipeline_mode=pl.Buffered(1)),
        pl.BlockSpec((ROUTER_ROWS, D_MODEL), const2),
        pl.BlockSpec((ROUTER_ROWS, D_MODEL), const2),
        pl.BlockSpec((ROUTER_ROWS, LANES), const2),
        pl.BlockSpec((tm, tm), const2),
        pl.BlockSpec((MOE_EXPERTS, MOE_EXPERTS), const2),
    ]
    out_shape = [
        jax.ShapeDtypeStruct((n_total, D_MODEL), F32),
        jax.ShapeDtypeStruct((n_total, D_MODEL), BF16),
        jax.ShapeDtypeStruct((n_tiles, META_ROWS, tm), F32),
        jax.ShapeDtypeStruct((n_tiles, MOE_EXPERTS, LANES), F32),
    ]
    out_specs = [
        pl.BlockSpec((tm, D_MODEL), tok_map),
        pl.BlockSpec((tm, D_MODEL), tok_map),
        pl.BlockSpec((1, META_ROWS, tm), lambda i: (i, 0, 0)),
        pl.BlockSpec((1, MOE_EXPERTS, LANES), lambda i: (i, 0, 0)),
    ]
    return pl.pallas_call(
        _outproj_kernel,
        grid=(n_tiles,),
        in_specs=in_specs,
        out_specs=out_specs,
        out_shape=out_shape,
        compiler_params=_cparams(("arbitrary",)),
        name="outproj",
    )(a, b, o_f, o_b, r, x2, mod_rows, lw["gla_out_norm"], lw["n2"], lw["w_out"],
      lw["wr_hi"], lw["wr_lo"], lw["br"], rconst["lst"], rconst["sl"])


def _piece_copies(src_ref, dst_ref, sem, src_row, dst_row, n_rows):
    copies = []
    for size in _PIECE_SIZES:
        done = n_rows & ~(2 * size - 1)
        pred = (n_rows & size) != 0
        s0 = pl.multiple_of(src_row + done, BF16_ROWS)
        d0 = pl.multiple_of(dst_row + done, BF16_ROWS)
        cp = pltpu.make_async_copy(src_ref.at[pl.ds(s0, size)], dst_ref.at[pl.ds(d0, size)], sem)
        copies.append((pred, cp))
    return copies


def _list_base(tile, s):
    return tile * len(_PIECE_SIZES) + s


def _start_listed_pieces(src_ref, dst_ref, sem, src_tab, dst_tab, n_tab, tile):
    for s, size in enumerate(_PIECE_SIZES):
        base = _list_base(tile, s)

        def body(i, carry, size=size, base=base):
            idx = base * MOE_EXPERTS + i
            s0 = pl.multiple_of(src_tab[idx], BF16_ROWS)
            d0 = pl.multiple_of(dst_tab[idx], BF16_ROWS)
            pltpu.make_async_copy(src_ref.at[pl.ds(s0, size)], dst_ref.at[pl.ds(d0, size)], sem).start()
            return carry

        lax.fori_loop(0, n_tab[base], body, 0)


def _wait_listed_pieces(src_ref, dst_ref, sem, n_tab, tile):
    for s, size in enumerate(_PIECE_SIZES):
        def body(i, carry, size=size):
            pltpu.make_async_copy(src_ref.at[pl.ds(0, size)], dst_ref.at[pl.ds(0, size)], sem).wait()
            return carry

        lax.fori_loop(0, n_tab[_list_base(tile, s)], body, 0)


def _stream_starts(stream_tiles):
    return [sum(stream_tiles[:s]) for s in range(len(stream_tiles))]


def _stream_block(i, start, n):
    return jnp.clip(i - start, 0, n - 1)


def _dispatch_kernel(ploc_tab, pdst_tab, pn_tab, ztab_dst, ztab_cnt, n_used, *refs, stream_tiles):
    n_streams = len(stream_tiles)
    h2_refs = refs[:n_streams]
    meta_ref, xs_ref, xloc_ref, zeros_ref, sems = refs[n_streams:]
    j = pl.program_id(0)
    n_tiles = pl.num_programs(0)
    tm = h2_refs[0].shape[0]
    slot = j % 2

    @pl.when(j == 0)
    def _():
        zeros_ref[...] = jnp.zeros_like(zeros_ref)

        def fill(t, carry, start):
            cp = pltpu.make_async_copy(zeros_ref, xs_ref.at[pl.ds(pl.multiple_of(t * TM_FFN, TM_FFN), TM_FFN)],
                                       sems.at[0])
            if start:
                cp.start()
            else:
                cp.wait()
            return carry
        lax.fori_loop(n_used[0], xs_ref.shape[0] // TM_FFN, functools.partial(fill, start=True), 0)
        lax.fori_loop(n_used[0], xs_ref.shape[0] // TM_FFN, functools.partial(fill, start=False), 0)
        zcopies = []
        for e in range(MOE_EXPERTS):
            zcopies += _piece_copies(zeros_ref, xs_ref, sems.at[0], 0, ztab_dst[e], ztab_cnt[e])
        for pred, cp in zcopies:
            @pl.when(pred)
            def _(cp=cp):
                cp.start()
        for pred, cp in zcopies:
            @pl.when(pred)
            def _(cp=cp):
                cp.wait()

    pos1 = meta_ref[0, 0:1, :]
    pos2 = meta_ref[0, 1:2, :]
    s_iota = lax.broadcasted_iota(jnp.int32, (SLOTS, tm), 0).astype(F32)
    perm = jnp.where((s_iota == pos1) | (s_iota == pos2), 1.0, 0.0).astype(BF16)
    h2 = h2_refs[0][...]
    for s, start in list(enumerate(_stream_starts(stream_tiles)))[1:]:
        h2 = jnp.where(j >= start, h2_refs[s][...], h2)
    xloc_ref[slot] = _dot(perm, h2).astype(BF16)

    @pl.when(j > 0)
    def _():
        _wait_listed_pieces(xloc_ref.at[1 - slot], xs_ref, sems.at[1 - slot], pn_tab, j - 1)

    _start_listed_pieces(xloc_ref.at[slot], xs_ref, sems.at[slot], ploc_tab, pdst_tab, pn_tab, j)

    @pl.when(j == n_tiles - 1)
    def _():
        _wait_listed_pieces(xloc_ref.at[slot], xs_ref, sems.at[slot], pn_tab, j)


def _dispatch(h2s, meta, tabs, r_max):
    n_tiles = meta.shape[0]
    tm = T_LOC
    stream_tiles = tuple(h.shape[0] // tm for h in h2s)
    assert sum(stream_tiles) == n_tiles
    h2_specs = [pl.BlockSpec((tm, D_MODEL), lambda i, *_, start=start, n=n: (_stream_block(i, start, n), 0))
                for start, n in zip(_stream_starts(stream_tiles), stream_tiles)]
    grid_spec = pltpu.PrefetchScalarGridSpec(
        num_scalar_prefetch=6,
        grid=(n_tiles,),
        in_specs=h2_specs + [pl.BlockSpec((1, META_ROWS, tm), lambda i, *_: (i, 0, 0))],
        out_specs=pl.BlockSpec(memory_space=pl.ANY),
        scratch_shapes=[pltpu.VMEM((2, SLOTS, D_MODEL), BF16), pltpu.VMEM((TM_FFN, D_MODEL), BF16),
                        pltpu.SemaphoreType.DMA((2,))],
    )
    return pl.pallas_call(
        functools.partial(_dispatch_kernel, stream_tiles=stream_tiles),
        grid_spec=grid_spec,
        out_shape=jax.ShapeDtypeStruct((r_max, D_MODEL), BF16),
        compiler_params=_cparams(("arbitrary",)),
        name="dispatch",
    )(tabs["ploc"], tabs["pdst"], tabs["pn"], tabs["zdst"], tabs["zcnt"], tabs["n_used"], *h2s, meta)


def _ffn_kernel(tile_e, tile_blk, n_used, x_ref, w1_ref, w3_ref, w2_ref, y_ref):
    i = pl.program_id(0)

    @pl.when(i < n_used[0])
    def _():
        x = x_ref[...]
        acc = None
        for f in range(D_EXPERT // F_CHUNK):
            cols = slice(f * F_CHUNK, (f + 1) * F_CHUNK)
            h1 = _dot(x, w1_ref[0, :, cols])
            h3 = _dot(x, w3_ref[0, :, cols])
            act = (_silu(h1) * h3).astype(BF16)
            part = _dot(act, w2_ref[0, cols, :])
            acc = part if acc is None else acc + part
        y_ref[...] = acc.astype(BF16)

    @pl.when(i >= n_used[0])
    def _():
        y_ref[...] = jnp.zeros_like(y_ref)


def _ffn(xs, w1, w3, w2, tabs):
    r_max = xs.shape[0]
    n_mt = r_max // TM_FFN
    grid_spec = pltpu.PrefetchScalarGridSpec(
        num_scalar_prefetch=3,
        grid=(n_mt,),
        in_specs=[
            pl.BlockSpec((TM_FFN, D_MODEL), lambda i, te, tb, nu: (tb[i], 0)),
            pl.BlockSpec((1, D_MODEL, D_EXPERT), lambda i, te, tb, nu: (te[i], 0, 0)),
            pl.BlockSpec((1, D_MODEL, D_EXPERT), lambda i, te, tb, nu: (te[i], 0, 0)),
            pl.BlockSpec((1, D_EXPERT, D_MODEL), lambda i, te, tb, nu: (te[i], 0, 0)),
        ],
        out_specs=pl.BlockSpec((TM_FFN, D_MODEL), lambda i, te, tb, nu: (i, 0)),
    )
    return pl.pallas_call(
        _ffn_kernel,
        grid_spec=grid_spec,
        out_shape=jax.ShapeDtypeStruct((r_max, D_MODEL), BF16),
        compiler_params=_cparams(("arbitrary",)),
        name="ffn",
    )(tabs["tile_e"], tabs["tile_blk"], tabs["n_used"], xs, w1, w3, w2)


def _combine_kernel(ploc_tab, pdst_tab, pn_tab, y_ref, metac_ref, mod_ref, *refs, stream_tiles):
    n_streams = len(stream_tiles)
    x1_refs = refs[:n_streams]
    out_refs = refs[n_streams:2 * n_streams]
    yloc_ref, sems = refs[2 * n_streams:]
    j = pl.program_id(0)
    n_tiles = pl.num_programs(0)
    d = D_MODEL
    tm = x1_refs[0].shape[0]
    slot = j % 2

    @pl.when(j == 0)
    def _():
        yloc_ref[...] = jnp.zeros_like(yloc_ref)
        _start_listed_pieces(y_ref, yloc_ref.at[0], sems.at[0], pdst_tab, ploc_tab, pn_tab, 0)

    @pl.when(j + 1 < n_tiles)
    def _():
        _start_listed_pieces(y_ref, yloc_ref.at[1 - slot], sems.at[1 - slot], pdst_tab, ploc_tab, pn_tab, j + 1)

    _wait_listed_pieces(y_ref, yloc_ref.at[slot], sems.at[slot], pn_tab, j)

    mc = metac_ref[0]
    s_iota = lax.broadcasted_iota(jnp.int32, (tm, SLOTS), 1).astype(F32)
    wc = jnp.where(s_iota == mc[:, 0:1], mc[:, 2:3], 0.0) + jnp.where(s_iota == mc[:, 1:2], mc[:, 3:4], 0.0)
    moe = _dot(wc.astype(BF16), yloc_ref[slot])
    gate2 = mod_ref[0, :, 5 * d:6 * d]
    for start, n, x1_ref, out_ref in zip(_stream_starts(stream_tiles), stream_tiles, x1_refs, out_refs):
        @pl.when((j >= start) & (j < start + n))
        def _(x1_ref=x1_ref, out_ref=out_ref):
            out_ref[...] = x1_ref[...] + gate2 * moe


def _combine(y, metac, x1s, n_batch, mod_rows, mod_row0s, tabs):
    tm = T_LOC
    stream_tiles = tuple(x1.shape[0] // tm for x1 in x1s)
    starts = _stream_starts(stream_tiles)
    n_tiles = sum(stream_tiles)

    def mod_map(i, *_):
        row = jnp.int32(0)
        for start, n, x1, row0 in zip(starts, stream_tiles, x1s, mod_row0s):
            tpb = x1.shape[0] // n_batch // tm
            row_s = _stream_block(i, start, n) // tpb if row0 == 0 else row0
            row = jnp.where(i >= start, row_s, row)
        return (row, 0, 0)

    tile_specs = [pl.BlockSpec((tm, D_MODEL), lambda i, *_, start=start, n=n: (_stream_block(i, start, n), 0))
                  for start, n in zip(starts, stream_tiles)]
    grid_spec = pltpu.PrefetchScalarGridSpec(
        num_scalar_prefetch=3,
        grid=(n_tiles,),
        in_specs=[
            pl.BlockSpec(memory_space=pl.ANY),
            pl.BlockSpec((1, tm, META_ROWS), lambda i, *_: (i, 0, 0)),
            pl.BlockSpec((1, 1, 6 * D_MODEL), mod_map),
        ] + tile_specs,
        out_specs=tile_specs,
        scratch_shapes=[pltpu.VMEM((2, SLOTS, D_MODEL), BF16), pltpu.SemaphoreType.DMA((2,))],
    )
    return pl.pallas_call(
        functools.partial(_combine_kernel, stream_tiles=stream_tiles),
        grid_spec=grid_spec,
        out_shape=[jax.ShapeDtypeStruct(x1.shape, F32) for x1 in x1s],
        compiler_params=_cparams(("arbitrary",)),
        name="combine",
    )(tabs["ploc"], tabs["pdst"], tabs["pn"], y, metac, mod_rows, *x1s)


def _rope_tables(n_tokens):
    rows = n_tokens // GRID_W
    row = jnp.repeat(jnp.arange(rows, dtype=jnp.int32), GRID_W)
    col = jnp.tile(jnp.arange(GRID_W, dtype=jnp.int32), rows)
    n_freq = MLA_ROPE // 4
    inv_freq = ROPE_THETA ** (-jnp.arange(n_freq, dtype=F32) / n_freq)
    ang_r = row[:, None].astype(F32) * inv_freq
    ang_c = col[:, None].astype(F32) * inv_freq
    cos = jnp.concatenate([jnp.cos(ang_r)] * 2 + [jnp.cos(ang_c)] * 2, axis=-1)
    sin = jnp.concatenate([-jnp.sin(ang_r), jnp.sin(ang_r), -jnp.sin(ang_c), jnp.sin(ang_c)], axis=-1)
    return cos, sin


_SWAP64 = np.concatenate([np.arange(16, 32), np.arange(0, 16), np.arange(48, 64), np.arange(32, 48)])


def _position_tables(cos, sin, q_gain, k_gain):
    gq = q_gain[MLA_NOPE:]
    gk = k_gain[MLA_NOPE:]
    cq = jnp.tile(cos * gq, (1, LANES // MLA_ROPE))
    sq = jnp.tile(sin * gq[_SWAP64], (1, LANES // MLA_ROPE))
    tk = jnp.concatenate([cos * gk, sin * gk[_SWAP64]], axis=-1)
    return {"cq": cq, "sq": sq, "tk": tk}


def _layer_weights(l, p):
    d = D_MODEL
    o = _IN_OFF
    w_in = p["w_in"][l]
    seg = lambda i: w_in[:, o[i]:o[i + 1]]
    kpe = seg(2)
    w_in_r = jnp.concatenate(
        [seg(0), seg(1), seg(3), seg(4), seg(5) * (GLA_DK ** -0.5), seg(6), seg(7), seg(8),
         kpe, kpe[:, _SWAP64], seg(9), seg(10), jnp.zeros((d, Z_WIDTH - Z_GG - 2 * GLA_GATE_RANK), F32)],
        axis=-1).astype(BF16)
    w_uq = p["mla_w_uq"][l].reshape(MLA_Q_RANK, MLA_HEADS, MLA_QK)
    q_nope = w_uq[:, :, :MLA_NOPE].reshape(MLA_Q_RANK, -1)
    q_pe = w_uq[:, :, MLA_NOPE:]
    w_uq_r = jnp.concatenate([q_nope, q_pe.reshape(MLA_Q_RANK, -1), q_pe[:, :, _SWAP64].reshape(MLA_Q_RANK, -1)],
                             axis=-1).astype(BF16)
    w_ukv = p["mla_w_ukv"][l].reshape(MLA_KV_RANK, MLA_HEADS, MLA_NOPE + MLA_V)
    w_ukv_r = jnp.concatenate([w_ukv[:, :, :MLA_NOPE].reshape(MLA_KV_RANK, -1),
                               w_ukv[:, :, MLA_NOPE:].reshape(MLA_KV_RANK, -1)], axis=-1).astype(BF16)
    wg = jnp.zeros((LANES, 2 * GLA_QK_W), F32)
    wg = wg.at[0:GLA_GATE_RANK, 0:GLA_QK_W].set(p["gla_wg_f"][l])
    wg = wg.at[GLA_GATE_RANK:2 * GLA_GATE_RANK, GLA_QK_W:].set(p["gla_wg_b"][l])
    bg = jnp.concatenate([p["gla_bg_f"][l], p["gla_bg_b"][l]])[None, :]
    sgu_b = jnp.repeat(p["sgu_b"][l].T, SGU_GROUP_DIM, axis=1)
    wr = jnp.zeros((ROUTER_ROWS, d), F32)
    wr = wr.at[0:MOE_EXPERTS].set(p["moe_w_expert"][l].T)
    wr = wr.at[MOE_EXPERTS:MOE_EXPERTS + MOE_GROUPS].set(p["moe_w_group"][l].T)
    wr_hi = wr.astype(BF16)
    wr_lo = (wr - wr_hi.astype(F32)).astype(BF16)
    br = jnp.zeros((ROUTER_ROWS,), F32).at[0:MOE_EXPERTS].set(p["moe_b_expert"][l])
    br = br.at[MOE_EXPERTS:MOE_EXPERTS + MOE_GROUPS].set(p["moe_b_group"][l])
    return {
        "n1": p["norm1_g"][l][None, :], "n2": p["norm2_g"][l][None, :],
        "w_in": w_in_r, "q_norm": p["mla_q_norm"][l][None, :], "kv_norm": p["mla_kv_norm"][l][None, :],
        "w_uq": w_uq_r, "w_ukv": w_ukv_r,
        "gq_nope": p["mla_q_gain"][l][None, :MLA_NOPE], "gk_nope": p["mla_k_gain"][l][None, :MLA_NOPE],
        "sgu_norm": p["sgu_norm"][l].reshape(1, SGU_WIDTH), "sgu_w": p["sgu_w"][l].astype(BF16), "sgu_b": sgu_b,
        "wg": wg.astype(BF16), "bg": bg,
        "gla_out_norm": p["gla_out_norm"][l][None, :], "w_out": p["w_out"][l].astype(BF16),
        "wr_hi": wr_hi, "wr_lo": wr_lo, "br": jnp.broadcast_to(br[:, None], (ROUTER_ROWS, LANES)),
    }


def _router_constants():
    t = np.arange(T_LOC)
    lst = (t[:, None] < t[None, :]).astype(np.float32)
    e = np.arange(MOE_EXPERTS)
    sl = (e[None, :] < e[:, None]).astype(np.float32)
    return {"lst": jnp.asarray(lst, BF16), "sl": jnp.asarray(sl, BF16)}


def _dispatch_tables(cnt, r_max):
    n_tiles = cnt.shape[0]
    cnt = cnt.astype(jnp.int32)
    loc_off = jnp.cumsum(cnt, axis=1) - cnt
    rows_e = jnp.sum(cnt, axis=0)
    tiles_e = (rows_e + TM_FFN - 1) // TM_FFN
    tile_start = jnp.cumsum(tiles_e) - tiles_e
    base_e = tile_start * TM_FFN
    dst = base_e[None, :] + jnp.cumsum(cnt, axis=0) - cnt
    n_used = jnp.sum(tiles_e)
    n_mt = r_max // TM_FFN
    ti = jnp.arange(n_mt, dtype=jnp.int32)
    tile_blk = jnp.minimum(ti, n_used - 1)
    tile_e = jnp.sum(tile_blk[:, None] >= (tile_start + tiles_e)[None, :], axis=1).astype(jnp.int32)
    tile_e = jnp.minimum(tile_e, MOE_EXPERTS - 1)
    zdst = base_e + rows_e
    zcnt = tiles_e * TM_FFN - rows_e
    i32 = lambda a: a.astype(jnp.int32)
    sizes = jnp.asarray(_PIECE_SIZES, jnp.int32)
    has = (cnt[:, :, None] & sizes) != 0
    done = cnt[:, :, None] & ~(2 * sizes - 1)
    pos = jnp.cumsum(has, axis=1) - 1
    pick = has[..., None] & (pos[..., None] == jnp.arange(MOE_EXPERTS))
    listed = lambda a: jnp.sum(jnp.where(pick, (a[:, :, None] + done)[..., None], 0), axis=1)
    return {"ploc": i32(listed(loc_off).reshape(-1)), "pdst": i32(listed(dst).reshape(-1)),
            "pn": i32(jnp.sum(has, axis=1).reshape(-1)), "zdst": i32(zdst), "zcnt": i32(zcnt),
            "tile_e": tile_e, "tile_blk": i32(tile_blk), "n_used": i32(n_used.reshape(1))}


def _moe(h2s, metas, cnts, x1s, n_batch, mod_rows, mod_row0s, lw):
    meta = jnp.concatenate(metas, axis=0)
    cnt = jnp.concatenate(cnts, axis=0)
    n_tiles = meta.shape[0]
    worst_rows = n_tiles * SLOTS + MOE_EXPERTS * (TM_FFN - BF16_ROWS)
    r_max = -(-worst_rows // TM_FFN) * TM_FFN
    tabs = _dispatch_tables(cnt[:, :, 0], r_max)
    xs = _dispatch(h2s, meta, tabs, r_max)
    y = _ffn(xs, lw["w1"], lw["w3"], lw["w2"], tabs)
    metac = jnp.swapaxes(meta, 1, 2)
    return _combine(y, metac, x1s, n_batch, mod_rows, mod_row0s, tabs)


def kernel(x, c, ctx, c_ctx, norm1_g, norm2_g, ada_w, ada_b, w_in, mla_q_norm, mla_w_uq, mla_kv_norm, mla_w_ukv,
           mla_q_gain, mla_k_gain, sgu_norm, sgu_w, sgu_b, gla_wg_f, gla_bg_f, gla_wg_b, gla_bg_b, gla_out_norm,
           w_out, moe_w_group, moe_b_group, moe_w_expert, moe_b_expert, moe_w1, moe_w3, moe_w2):
    p = dict(norm1_g=norm1_g, norm2_g=norm2_g, w_in=w_in, mla_q_norm=mla_q_norm, mla_w_uq=mla_w_uq,
             mla_kv_norm=mla_kv_norm, mla_w_ukv=mla_w_ukv, mla_q_gain=mla_q_gain, mla_k_gain=mla_k_gain,
             sgu_norm=sgu_norm, sgu_w=sgu_w, sgu_b=sgu_b, gla_wg_f=gla_wg_f, gla_bg_f=gla_bg_f,
             gla_wg_b=gla_wg_b, gla_bg_b=gla_bg_b, gla_out_norm=gla_out_norm, w_out=w_out,
             moe_w_group=moe_w_group, moe_b_group=moe_b_group, moe_w_expert=moe_w_expert,
             moe_b_expert=moe_b_expert, moe_w1=moe_w1, moe_w3=moe_w3, moe_w2=moe_w2)
    n_batch, seq, d = x.shape
    n_ctx = ctx.shape[1]
    depth = ada_w.shape[0]
    assert d == D_MODEL and n_batch <= CTX_MOD_ROW and seq % T_LOC == 0 and seq % TM_IN == 0 and n_ctx % TM_SUB == 0
    assert seq % (GLA_STEP_CHUNKS * GLA_CHUNK) == 0 and n_ctx % (GLA_STEP_CHUNKS * GLA_CHUNK) == 0
    assert (n_batch * n_ctx) % T_LOC == 0

    c_rows = jnp.zeros((MOD_ROWS, d), F32).at[0:n_batch].set(c).at[CTX_MOD_ROW].set(c_ctx)
    mod = _ada(c_rows, ada_w, ada_b)

    cos, sin = _rope_tables(seq)
    ones_c = jnp.ones((min(TM_IN, n_ctx), MLA_ROPE), F32)
    gla_consts_np = _gla_constants()
    gla_consts = (jnp.asarray(gla_consts_np[0], BF16), jnp.asarray(gla_consts_np[1], F32))
    rconst = _router_constants()
    zero_state = jnp.zeros((2, n_batch, GLA_DV, GLA_QK_W), F32)

    x2 = x.reshape(n_batch * seq, d)
    ctx2 = ctx.reshape(n_batch * n_ctx, d)
    for l in range(depth):
        ctx_out = l < depth - 1
        lw = _layer_weights(l, p)
        mod_rows = mod[l].reshape(MOD_ROWS, 1, 6 * d)
        tabs_x = _position_tables(cos, sin, mla_q_gain[l], mla_k_gain[l])
        tabs_c = _position_tables(ones_c, 0.0 * ones_c, mla_q_gain[l], mla_k_gain[l])

        q_x, k_x, v_x, b_x, gla_x, r_x = _inproj(x2, n_batch, mod_rows, 0, lw, tabs_x)
        q_c, k_c, v_c, b_c, gla_c, r_c = _inproj(ctx2, n_batch, mod_rows, CTX_MOD_ROW, lw, tabs_c)

        of_c, ob_c, s_ctx = _gla(gla_c.reshape(n_batch, n_ctx, -1), zero_state, gla_consts)
        of_x, ob_x, _ = _gla(gla_x.reshape(n_batch, seq, -1), s_ctx, gla_consts)
        a_x, lw["w1"], lw["w3"], lw["w2"] = _attention(q_x, [k_x, k_c], [v_x, v_c],
                                                       casts=(moe_w1, moe_w3, moe_w2), cast_layer=l)

        x1, h2, meta, cnt = _outproj(a_x.reshape(n_batch * seq, -1), b_x, of_x.reshape(n_batch * seq, -1),
                                     ob_x.reshape(n_batch * seq, -1), r_x, x2, n_batch, mod_rows, 0, lw, rconst)
        if ctx_out:
            a_c, = _attention(q_c, [k_c], [v_c])
            c1, h2c, meta_c, cnt_c = _outproj(a_c.reshape(n_batch * n_ctx, -1), b_c,
                                              of_c.reshape(n_batch * n_ctx, -1), ob_c.reshape(n_batch * n_ctx, -1),
                                              r_c, ctx2, n_batch, mod_rows, CTX_MOD_ROW, lw, rconst)
            x2, ctx2 = _moe([h2, h2c], [meta, meta_c], [cnt, cnt_c], [x1, c1], n_batch, mod_rows, [0, CTX_MOD_ROW], lw)
        else:
            x2, = _moe([h2], [meta], [cnt], [x1], n_batch, mod_rows, [0], lw)
    return x2.reshape(n_batch, seq, d)
```
